```python
import math
import jax, jax.numpy as jnp
from jax import lax
import numpy as np

D_MODEL = 4096
BATCH = 4
SEQ = 2048
DEPTH = 1

D_SSM = D_MODEL // 2
SSM_GROUP = 16
N_SSM_GROUPS = D_SSM // SSM_GROUP
SSM_STATE = 64
D_POOL = D_MODEL // 2
POOL_WINDOWS = (2, 4, 8, 16)
N_POOL_GROUPS = len(POOL_WINDOWS)
POOL_GROUP = D_POOL // N_POOL_GROUPS
D_IN = D_SSM + D_POOL + 2 * D_MODEL
D_FF = 11008
CONV_WIDTH = 3
EPS = 1e-6
MIN_NEG_REAL = -1e-4

kernel_name = "s5_pool_gated_hybrid_block"


def rms_norm(x, g):
    xf = x.astype(jnp.float32)
    r = xf * lax.rsqrt(jnp.mean(xf * xf, axis=-1, keepdims=True) + EPS)
    return (r * g.astype(jnp.float32)).astype(x.dtype)


def s5_branch(u, lam_re, lam_im, log_step, b_re, b_im, c_re, c_im, d, glu_w, glu_b):
    bsz, L, _ = u.shape
    uf = u.astype(jnp.float32)
    ug = uf.reshape(bsz, L, N_SSM_GROUPS, SSM_GROUP)
    lr = jnp.minimum(lam_re.astype(jnp.float32), MIN_NEG_REAL)
    li = lam_im.astype(jnp.float32)
    dt = jnp.exp(log_step.astype(jnp.float32))[:, None]
    mag = jnp.exp(lr * dt)
    ang = li * dt
    ab_re = mag * jnp.cos(ang)
    ab_im = mag * jnp.sin(ang)
    nr = ab_re - 1.0
    ni = ab_im
    den = lr * lr + li * li
    f_re = (nr * lr + ni * li) / den
    f_im = (ni * lr - nr * li) / den
    br = b_re.astype(jnp.float32)
    bi = b_im.astype(jnp.float32)
    bb_re = f_re[..., None] * br - f_im[..., None] * bi
    bb_im = f_re[..., None] * bi + f_im[..., None] * br
    bu_re = jnp.einsum('blgj,gpj->blgp', ug, bb_re)
    bu_im = jnp.einsum('blgj,gpj->blgp', ug, bb_im)
    a_re = jnp.broadcast_to(ab_re, (1, L) + ab_re.shape)
    a_im = jnp.broadcast_to(ab_im, (1, L) + ab_im.shape)

    def combine(e1, e2):
        a1r, a1i, b1r, b1i = e1
        a2r, a2i, b2r, b2i = e2
        return (a2r * a1r - a2i * a1i,
                a2r * a1i + a2i * a1r,
                a2r * b1r - a2i * b1i + b2r,
                a2r * b1i + a2i * b1r + b2i)

    _, _, s_re, s_im = lax.associative_scan(combine, (a_re, a_im, bu_re, bu_im), axis=1)
    y = (jnp.einsum('blgp,gjp->blgj', s_re, c_re.astype(jnp.float32))
         - jnp.einsum('blgp,gjp->blgj', s_im, c_im.astype(jnp.float32)))
    y = y.reshape(bsz, L, D_SSM) + d.astype(jnp.float32) * uf
    y = jax.nn.gelu(y, approximate=True)
    y = y * jax.nn.sigmoid(y @ glu_w.astype(jnp.float32) + glu_b.astype(jnp.float32))
    return y.astype(u.dtype)


def pool_branch(v, pool_w, pool_b, pool_scale):
    bsz, L, _ = v.shape
    vf = v.astype(jnp.float32).reshape(bsz, L, N_POOL_GROUPS, POOL_GROUP)
    cs = jnp.pad(jnp.cumsum(vf, axis=1), ((0, 0), (1, 0), (0, 0), (0, 0)))
    t = jnp.arange(1, L + 1)
    pooled = []
    for gi, win in enumerate(POOL_WINDOWS):
        start = jnp.maximum(t - win, 0)
        s = cs[:, 1:, gi] - jnp.take(cs[:, :, gi], start, axis=1)
        cnt = (t - start).astype(jnp.float32)
        pooled.append(s / cnt[None, :, None])
    z = jnp.stack(pooled, axis=2) - vf
    z = jnp.einsum('blgc,gcd->blgd', z, pool_w.astype(jnp.float32)) + pool_b.astype(jnp.float32)
    z = z.reshape(bsz, L, D_POOL) * pool_scale.astype(jnp.float32)
    return z.astype(v.dtype)


def causal_depthwise_conv(u, w, b):
    L = u.shape[1]
    up = jnp.pad(u, ((0, 0), (CONV_WIDTH - 1, 0), (0, 0)))
    y = b
    for k in range(CONV_WIDTH):
        y = y + up[:, k:k + L] * w[k]
    return y


def setup_inputs(seed: int = 0) -> dict:
    key = jax.random.key(seed)
    ks = jax.random.split(key, 32)
    f32 = jnp.float32
    nrm = lambda k, shape, s: (jax.random.normal(k, shape, f32) * s).astype(f32)
    gain = lambda k: (1.0 + 0.02 * jax.random.normal(k, (DEPTH, D_MODEL), f32)).astype(f32)
    G, P, GC = N_SSM_GROUPS, SSM_STATE, SSM_GROUP
    n_idx = jnp.arange(P, dtype=f32)
    lam_re = -0.5 + 0.01 * jax.random.normal(ks[3], (DEPTH, G, P), f32)
    lam_im = math.pi * n_idx[None, None, :] + 0.01 * jax.random.normal(ks[4], (DEPTH, G, P), f32)
    log_step = jax.random.uniform(ks[5], (DEPTH, G), f32, math.log(1e-3), math.log(1e-1))
    return {
        "x": nrm(ks[0], (BATCH, SEQ, D_MODEL), 1.0),
        "norm_pre_mix": gain(ks[1]),
        "w_in": nrm(ks[2], (DEPTH, D_MODEL, D_IN), D_MODEL ** -0.5),
        "ssm_lambda_re": lam_re,
        "ssm_lambda_im": lam_im,
        "ssm_log_step": log_step,
        "ssm_b_re": nrm(ks[6], (DEPTH, G, P, GC), (2.0 * GC) ** -0.5),
        "ssm_b_im": nrm(ks[7], (DEPTH, G, P, GC), (2.0 * GC) ** -0.5),
        "ssm_c_re": nrm(ks[8], (DEPTH, G, GC, P), (2.0 * P) ** -0.5),
        "ssm_c_im": nrm(ks[9], (DEPTH, G, GC, P), (2.0 * P) ** -0.5),
        "ssm_d": nrm(ks[10], (DEPTH, D_SSM), 1.0),
        "ssm_glu_w": nrm(ks[11], (DEPTH, D_SSM, D_SSM), D_SSM ** -0.5),
        "ssm_glu_b": nrm(ks[12], (DEPTH, D_SSM), 0.02),
        "pool_w": nrm(ks[13], (DEPTH, N_POOL_GROUPS, POOL_GROUP, POOL_GROUP), POOL_GROUP ** -0.5),
        "pool_b": nrm(ks[14], (DEPTH, N_POOL_GROUPS, POOL_GROUP), 0.02),
        "pool_scale": (1.0 + 0.1 * jax.random.normal(ks[15], (DEPTH, D_POOL), f32)).astype(f32),
        "w_branch_ssm": nrm(ks[16], (DEPTH, D_SSM, D_MODEL), D_SSM ** -0.5),
        "w_branch_pool": nrm(ks[17], (DEPTH, D_POOL, D_MODEL), D_POOL ** -0.5),
        "w_out": nrm(ks[18], (DEPTH, D_MODEL, D_MODEL), D_MODEL ** -0.5),
        "norm_post_mix": gain(ks[19]),
        "norm_pre_ffn": gain(ks[20]),
        "w_up": nrm(ks[21], (DEPTH, D_MODEL, 2 * D_FF), D_MODEL ** -0.5),
        "ffn_conv_w": nrm(ks[22], (DEPTH, CONV_WIDTH, 2 * D_FF), CONV_WIDTH ** -0.5),
        "ffn_conv_b": nrm(ks[23], (DEPTH, 2 * D_FF), 0.02),
        "w_down": nrm(ks[24], (DEPTH, D_FF, D_MODEL), D_FF ** -0.5),
        "norm_post_ffn": gain(ks[25]),
    }


def reference(x, norm_pre_mix, w_in, ssm_lambda_re, ssm_lambda_im, ssm_log_step,
              ssm_b_re, ssm_b_im, ssm_c_re, ssm_c_im, ssm_d, ssm_glu_w, ssm_glu_b,
              pool_w, pool_b, pool_scale, w_branch_ssm, w_branch_pool, w_out,
              norm_post_mix, norm_pre_ffn, w_up, ffn_conv_w, ffn_conv_b, w_down,
              norm_post_ffn):
    h = x
    for i in range(DEPTH):
        a = rms_norm(h, norm_pre_mix[i])
        proj = a @ w_in[i]
        u_ssm = proj[..., :D_SSM]
        u_pool = proj[..., D_SSM:D_SSM + D_POOL]
        g_ssm = proj[..., D_SSM + D_POOL:D_SSM + D_POOL + D_MODEL]
        g_pool = proj[..., D_SSM + D_POOL + D_MODEL:]
        y_ssm = s5_branch(u_ssm, ssm_lambda_re[i], ssm_lambda_im[i], ssm_log_step[i],
                          ssm_b_re[i], ssm_b_im[i], ssm_c_re[i], ssm_c_im[i], ssm_d[i],
                          ssm_glu_w[i], ssm_glu_b[i]) @ w_branch_ssm[i]
        y_pool = pool_branch(u_pool, pool_w[i], pool_b[i], pool_scale[i]) @ w_branch_pool[i]
        merged = jax.nn.sigmoid(g_ssm) * y_ssm + jax.nn.sigmoid(g_pool) * y_pool
        h = h + rms_norm(merged @ w_out[i], norm_post_mix[i])
        c = rms_norm(h, norm_pre_ffn[i])
        up = causal_depthwise_conv(c @ w_up[i], ffn_conv_w[i], ffn_conv_b[i])
        f = jax.nn.gelu(up[..., :D_FF], approximate=True) * up[..., D_FF:]
        h = h + rms_norm(f @ w_down[i], norm_post_ffn[i])
    return h
```

```python
import functools
import math

import jax
import jax.numpy as jnp
from jax import lax
from jax.experimental import pallas as pl
from jax.experimental.pallas import tpu as pltpu

F32 = jnp.float32
BF16 = jnp.bfloat16

EPS = 1e-6
MIN_NEG_REAL = -1e-4
SSM_GROUP = 16
SSM_STATE = 64
POOL_WINDOWS = (2, 4, 8, 16)
CONV_WIDTH = 3
CHUNK = 16
N_SCAN_PASSES = 7
LANES = 128

V7X_VMEM_BYTES = 64 * 1024 * 1024
VMEM_CEILING = V7X_VMEM_BYTES - 6 * 1024 * 1024


def _vmem_limit(pipelined_bytes, scratch_bytes, temp_bytes):
    return min(2 * pipelined_bytes + scratch_bytes + temp_bytes + (4 << 20), VMEM_CEILING)


def _params(semantics, vmem):
    return pltpu.CompilerParams(dimension_semantics=semantics, vmem_limit_bytes=vmem)


def _rms(xf, g):
    ms = jnp.mean(xf * xf, axis=-1, keepdims=True)
    return xf * lax.rsqrt(ms + EPS) * g


NORM_ROWS = 32


def _for_row_blocks(n_rows, body):
    def step(r, carry):
        body(pl.ds(pl.multiple_of(r * NORM_ROWS, NORM_ROWS), NORM_ROWS))
        return carry
    lax.fori_loop(0, n_rows // NORM_ROWS, step, 0)


def _sigmoid(x):
    return 1.0 / (1.0 + jnp.exp(-x))


def _gelu_tanh(x):
    c = math.sqrt(2.0 / math.pi)
    return 0.5 * x * (1.0 + jnp.tanh(c * (x + 0.044715 * (x * x * x))))


def _norm_kernel(x_ref, g_ref, o_ref):
    def rows(sl):
        o_ref[sl, :] = _rms(x_ref[sl, :], g_ref[...]).astype(BF16)
    _for_row_blocks(x_ref.shape[0], rows)


def _norm(x2, g, tm=512):
    t, d = x2.shape
    return pl.pallas_call(
        _norm_kernel,
        out_shape=jax.ShapeDtypeStruct((t, d), BF16),
        grid=(t // tm,),
        in_specs=[pl.BlockSpec((tm, d), lambda i: (i, 0)),
                  pl.BlockSpec((1, d), lambda i: (0, 0))],
        out_specs=pl.BlockSpec((tm, d), lambda i: (i, 0)),
        compiler_params=_params(("arbitrary",), _vmem_limit(tm * d * 6, 0, tm * d * 8)),
        name="norm",
    )(x2, g)


def _inproj_kernel(a_ref, w_ref, o_ref, *, n_plain):
    n = pl.program_id(1)
    p = jnp.dot(a_ref[...], w_ref[...], preferred_element_type=F32)

    @pl.when(n < n_plain)
    def _():
        o_ref[...] = p.astype(BF16)

    @pl.when(n >= n_plain)
    def _():
        o_ref[...] = _sigmoid(p).astype(BF16)


def _inproj(a, w, n_plain_cols, tm=1024, tn=512):
    t, d = a.shape
    n = w.shape[1]
    return pl.pallas_call(
        functools.partial(_inproj_kernel, n_plain=n_plain_cols // tn),
        out_shape=jax.ShapeDtypeStruct((t, n), BF16),
        grid=(t // tm, n // tn),
        in_specs=[pl.BlockSpec((tm, d), lambda i, j: (i, 0)),
                  pl.BlockSpec((d, tn), lambda i, j: (0, j))],
        out_specs=pl.BlockSpec((tm, tn), lambda i, j: (i, j)),
        compiler_params=_params(("arbitrary", "arbitrary"),
                                _vmem_limit((tm * d + d * tn + tm * tn) * 2, 0, tm * tn * 8)),
        name="inproj",
    )(a, w)


def _inproj_ssm_kernel(a_ref, w_ref, o_ref, p_sc):
    chunks = o_ref.shape[-1]
    p = jnp.dot(a_ref[...], w_ref[...], preferred_element_type=F32)
    n_slabs = p_sc.shape[0]
    for s in range(n_slabs):
        p_sc[s] = p[:, s * LANES:(s + 1) * LANES]
    gps = LANES // SSM_GROUP
    for q in range(CHUNK):
        for s in range(n_slabs):
            blk = p_sc[s, pl.ds(q, chunks, stride=CHUNK), :]
            o_ref[s * gps:(s + 1) * gps, q] = (
                blk.T.reshape(gps, SSM_GROUP, chunks).astype(BF16))


def _inproj_ssm(a, w, seq_len, tn=256):
    t, d = a.shape
    n = w.shape[1]
    g = n // SSM_GROUP
    chunks = seq_len // CHUNK
    return pl.pallas_call(
        _inproj_ssm_kernel,
        out_shape=jax.ShapeDtypeStruct((g, CHUNK, SSM_GROUP, t // CHUNK), BF16),
        grid=(t // seq_len, n // tn),
        in_specs=[pl.BlockSpec((seq_len, d), lambda bi, j: (bi, 0)),
                  pl.BlockSpec((d, tn), lambda bi, j: (0, j))],
        out_specs=pl.BlockSpec((tn // SSM_GROUP, CHUNK, SSM_GROUP, chunks),
                               lambda bi, j: (j, 0, 0, bi)),
        scratch_shapes=[pltpu.VMEM((tn // LANES, seq_len, LANES), F32)],
        compiler_params=_params(("arbitrary", "arbitrary"),
                                _vmem_limit((seq_len * d + d * tn + tn * seq_len) * 2,
                                            seq_len * tn * 4, seq_len * tn * 24)),
        name="inproj_ssm",
    )(a, w)


def _ssm_prep_kernel(lre_ref, lim_ref, ls_ref, btr_ref, bti_ref, cr_ref, ci_ref,
                     m_ref, rt_ref, et_ref, w_ref, *, groups):
    p2 = 2 * SSM_STATE
    cq = CHUNK * SSM_GROUP
    lane = lax.broadcasted_iota(jnp.int32, (1, p2), 1)
    is_re = lane < SSM_STATE
    sgn = jnp.where(is_re, 1.0, -1.0).astype(F32)
    lane_blk = lax.broadcasted_iota(jnp.int32, (cq, cq), 1) // SSM_GROUP
    row8 = lax.broadcasted_iota(jnp.int32, (8, p2), 0)
    is_re8 = lax.broadcasted_iota(jnp.int32, (8, p2), 1) < SSM_STATE

    for gi in range(groups):
        lr = jnp.minimum(lre_ref[gi], MIN_NEG_REAL)
        li = lim_ref[gi]
        dt = jnp.exp(ls_ref[gi])
        mag = jnp.exp(lr * dt)
        ang = li * dt
        ar = mag * jnp.cos(ang)
        ai = mag * jnp.sin(ang)
        nr = ar - 1.0
        ni = ai
        den = lr * lr + li * li
        f_re = (nr * lr + ni * li) / den
        f_im = (ni * lr - nr * li) / den
        btr = btr_ref[gi]
        bti = bti_ref[gi]
        bbr = f_re * btr - f_im * bti
        bbi = f_re * bti + f_im * btr
        cr = cr_ref[gi]
        ci = ci_ref[gi]

        pa = [jnp.where(is_re, 1.0, 0.0).astype(F32)]
        pb = [jnp.where(is_re, 0.0, 1.0).astype(F32)]
        for _ in range(CHUNK):
            a_prev, b_prev = pa[-1], pb[-1]
            pa.append(a_prev * ar + b_prev * ai)
            pb.append(b_prev * ar - a_prev * ai)

        gk = [sgn * (cr * pa[k] + ci * pb[k]) for k in range(CHUNK + 1)]
        fs = jnp.concatenate(gk[:CHUNK], axis=0)
        et = jnp.concatenate(gk[1:], axis=0)
        rtt = jnp.concatenate(
            [bbr * pa[CHUNK - 1 - q] + bbi * pb[CHUNK - 1 - q] for q in range(CHUNK)], axis=0)
        bst = jnp.where(is_re, bbr, bbi)
        bst_t = jnp.concatenate([bst] * CHUNK, axis=0)
        kw = lax.dot_general(fs, bst_t, (((1,), (1,)), ((), ())),
                             precision=lax.Precision.HIGHEST, preferred_element_type=F32)
        m = jnp.zeros((cq, cq), F32)
        for q in range(CHUNK):
            if q == 0:
                shifted = kw
            else:
                shifted = jnp.concatenate(
                    [jnp.zeros((q * SSM_GROUP, cq), F32), kw[:cq - q * SSM_GROUP]], axis=0)
            m = jnp.where(lane_blk == q, shifted, m)
        m_ref[gi] = m.astype(BF16)
        rt_ref[gi] = rtt.T.astype(BF16)
        et_ref[gi] = et.astype(BF16)

        wa = jnp.broadcast_to(pa[CHUNK], (8, p2))
        rows = jnp.zeros((8, p2), F32)
        for k in range(N_SCAN_PASSES):
            rows = jnp.where(row8 == k, wa, rows)
            swapped = pltpu.roll(wa, SSM_STATE, 1)
            re2 = jnp.where(is_re8, wa, swapped)
            im2 = jnp.where(is_re8, swapped, wa)
            wb = jnp.where(is_re8, -swapped, swapped)
            wa = wa * re2 + wb * im2
        full = jnp.concatenate([rows, jnp.zeros((p2 - 8, p2), F32)], axis=0)
        w_ref[gi] = full.T


def _ssm_prep(lam_re, lam_im, log_step, b_re, b_im, c_re, c_im, groups=8):
    g, p = lam_re.shape
    p2 = 2 * p
    cq = CHUNK * SSM_GROUP
    dup = lambda v: jnp.concatenate([v, v], axis=-1)
    lre = dup(lam_re)[:, None, :]
    lim = dup(lam_im)[:, None, :]
    ls = jnp.broadcast_to(log_step[:, None, None], (g, 1, p2))
    btr = dup(jnp.swapaxes(b_re, 1, 2))
    bti = dup(jnp.swapaxes(b_im, 1, 2))
    cr = dup(c_re)
    ci = dup(c_im)
    row = pl.BlockSpec((groups, 1, p2), lambda i: (i, 0, 0))
    mat = pl.BlockSpec((groups, SSM_GROUP, p2), lambda i: (i, 0, 0))
    return pl.pallas_call(
        functools.partial(_ssm_prep_kernel, groups=groups),
        out_shape=(jax.ShapeDtypeStruct((g, cq, cq), BF16),
                   jax.ShapeDtypeStruct((g, p2, cq), BF16),
                   jax.ShapeDtypeStruct((g, cq, p2), BF16),
                   jax.ShapeDtypeStruct((g, p2, p2), F32)),
        grid=(g // groups,),
        in_specs=[row, row, row, mat, mat, mat, mat],
        out_specs=(pl.BlockSpec((groups, cq, cq), lambda i: (i, 0, 0)),
                   pl.BlockSpec((groups, p2, cq), lambda i: (i, 0, 0)),
                   pl.BlockSpec((groups, cq, p2), lambda i: (i, 0, 0)),
                   pl.BlockSpec((groups, p2, p2), lambda i: (i, 0, 0))),
        compiler_params=_params(("arbitrary",), _vmem_limit(groups * cq * cq * 4, 0, 8 << 20)),
        name="ssm_prep",
    )(lre, lim, ls, btr, bti, cr, ci)


def _ssm_kernel(u_ref, m_ref, rt_ref, et_ref, w_ref, d_ref, o_ref, *, groups, chunks_per_seq):
    cq = CHUNK * SSM_GROUP
    nc = u_ref.shape[-1]
    pos = lax.broadcasted_iota(jnp.int32, (SSM_STATE, nc), 1) % chunks_per_seq
    for gi in range(groups):
        x3 = u_ref[gi]
        x = x3.reshape(cq, nc)
        y = jnp.dot(m_ref[gi], x, preferred_element_type=F32)
        r = jnp.dot(rt_ref[gi], x, preferred_element_type=F32)
        s_re, s_im = r[:SSM_STATE], r[SSM_STATE:]
        for k in range(N_SCAN_PASSES):
            dist = 1 << k
            wr = w_ref[gi, 0:SSM_STATE, k:k + 1]
            wi = w_ref[gi, SSM_STATE:2 * SSM_STATE, k:k + 1]
            keep = pos >= dist
            sh_re = jnp.where(keep, pltpu.roll(s_re, dist, 1), 0.0)
            sh_im = jnp.where(keep, pltpu.roll(s_im, dist, 1), 0.0)
            s_re, s_im = (s_re + wr * sh_re - wi * sh_im,
                          s_im + wr * sh_im + wi * sh_re)
        first = pos >= 1
        sp = jnp.concatenate([jnp.where(first, pltpu.roll(s_re, 1, 1), 0.0),
                              jnp.where(first, pltpu.roll(s_im, 1, 1), 0.0)], axis=0)
        y = y + jnp.dot(et_ref[gi], sp.astype(BF16), preferred_element_type=F32)
        y3 = y.reshape(CHUNK, SSM_GROUP, nc) + d_ref[gi][None] * x3.astype(F32)
        o_ref[gi] = _gelu_tanh(y3).astype(BF16)


def _ssm(ut, m, rt, et, w, d, chunks_per_seq, groups=4):
    g, q, j, nc = ut.shape
    cq = q * j
    p2 = rt.shape[1]
    blk = lambda *s: pl.BlockSpec((groups,) + s, lambda i: (i,) + (0,) * len(s))
    return pl.pallas_call(
        functools.partial(_ssm_kernel, groups=groups, chunks_per_seq=chunks_per_seq),
        out_shape=jax.ShapeDtypeStruct(ut.shape, BF16),
        grid=(g // groups,),
        in_specs=[blk(q, j, nc), blk(cq, cq), blk(p2, cq), blk(cq, p2), blk(p2, p2), blk(j, 1)],
        out_specs=blk(q, j, nc),
        compiler_params=_params(("arbitrary",), _vmem_limit(groups * cq * nc * 8, 0, 8 << 20)),
        name="ssm",
    )(ut, m, rt, et, w, d)


def _glu_kernel(y_ref, w_ref, b_ref, o_ref, y_sc, *, tn):
    n = pl.program_id(1)
    chunks = y_ref.shape[-1]
    n_slabs = y_sc.shape[0]

    @pl.when(n == 0)
    def _():
        for q in range(CHUNK):
            yq = y_ref[:, q]
            yq = yq.reshape(yq.shape[0] * yq.shape[1], chunks).astype(F32).T
            for s in range(n_slabs):
                y_sc[s, pl.ds(q, chunks, stride=CHUNK), :] = yq[:, s * LANES:(s + 1) * LANES]

    y = jnp.concatenate([y_sc[s] for s in range(n_slabs)], axis=1).astype(BF16)
    z = jnp.dot(y, w_ref[...], preferred_element_type=F32) + b_ref[...]
    per_tile = tn // LANES
    yn = jnp.concatenate([y_sc[n * per_tile + s] for s in range(per_tile)], axis=1)
    o_ref[...] = (yn * _sigmoid(z)).astype(BF16)


def _glu(yt, w, b, seq_len, tn=512):
    g, q, j, nc = yt.shape
    d = g * j
    chunks = seq_len // q
    t = nc * q
    return pl.pallas_call(
        functools.partial(_glu_kernel, tn=tn),
        out_shape=jax.ShapeDtypeStruct((t, d), BF16),
        grid=(nc // chunks, d // tn),
        in_specs=[pl.BlockSpec((g, q, j, chunks), lambda bi, n: (0, 0, 0, bi)),
                  pl.BlockSpec((d, tn), lambda bi, n: (0, n)),
                  pl.BlockSpec((1, tn), lambda bi, n: (0, n))],
        out_specs=pl.BlockSpec((seq_len, tn), lambda bi, n: (bi, n)),
        scratch_shapes=[pltpu.VMEM((d // LANES, seq_len, LANES), F32)],
        compiler_params=_params(("arbitrary", "arbitrary"),
                                _vmem_limit((seq_len * d + d * tn + seq_len * tn) * 2,
                                            seq_len * d * 4, seq_len * d * 2 + seq_len * tn * 12)),
        name="glu",
    )(yt, w, b)


def _pool_kernel(u_ref, w_ref, b_ref, s_ref, o_ref, ext_sc, *, tiles_per_seq, halo):
    i = pl.program_id(0)
    tm = u_ref.shape[0]
    gp = w_ref.shape[1]

    @pl.when(i % tiles_per_seq == 0)
    def _():
        ext_sc[0:halo] = jnp.zeros((halo, ext_sc.shape[1]), BF16)

    @pl.when(i % tiles_per_seq != 0)
    def _():
        ext_sc[0:halo] = ext_sc[tm:tm + halo]

    ext_sc[halo:halo + tm] = u_ref[...]

    t_loc = lax.broadcasted_iota(jnp.int32, (tm, tm + halo), 0)
    s_loc = lax.broadcasted_iota(jnp.int32, (tm, tm + halo), 1)
    lag = t_loc + halo - s_loc
    t_seq = (i % tiles_per_seq) * tm + lax.broadcasted_iota(jnp.int32, (tm, 1), 0)
    for gi, win in enumerate(POOL_WINDOWS):
        band = jnp.where((lag >= 0) & (lag < win), 1.0, 0.0).astype(BF16)
        cols = slice(gi * gp, (gi + 1) * gp)
        wsum = jnp.dot(band, ext_sc[:, cols], preferred_element_type=F32)
        cnt = jnp.minimum(t_seq + 1, win).astype(F32)
        z = wsum / cnt - u_ref[:, cols].astype(F32)
        z = jnp.dot(z.astype(BF16), w_ref[gi], preferred_element_type=F32) + b_ref[gi]
        o_ref[:, cols] = (z * s_ref[:, cols]).astype(BF16)


def _pool(prest, w, b, scale, seq_len, d_pool, tm=512, halo=128):
    t = prest.shape[0]
    ng, gp, _ = w.shape
    return pl.pallas_call(
        functools.partial(_pool_kernel, tiles_per_seq=seq_len // tm, halo=halo),
        out_shape=jax.ShapeDtypeStruct((t, d_pool), BF16),
        grid=(t // tm,),
        in_specs=[pl.BlockSpec((tm, d_pool), lambda i: (i, 0)),
                  pl.BlockSpec((ng, gp, gp), lambda i: (0, 0, 0)),
                  pl.BlockSpec((ng, 1, gp), lambda i: (0, 0, 0)),
                  pl.BlockSpec((1, d_pool), lambda i: (0, 0))],
        out_specs=pl.BlockSpec((tm, d_pool), lambda i: (i, 0)),
        scratch_shapes=[pltpu.VMEM((tm + halo, d_pool), BF16)],
        compiler_params=_params(("arbitrary",),
                                _vmem_limit(tm * d_pool * 4 + ng * gp * gp * 2,
                                            (tm + halo) * d_pool * 2, 8 << 20)),
        name="pool",
    )(prest, w, b, scale)


def _mix_kernel(ys_ref, po_ref, gs_ref, gp_ref, x_ref, wbs_ref, wbp_ref, wo_ref, g_ref, o_ref):
    n = pl.program_id(1)

    @pl.when(n == 0)
    def _():
        o_ref[...] = jnp.zeros_like(o_ref)

    ms = jnp.dot(ys_ref[...], wbs_ref[...], preferred_element_type=F32)
    mp = jnp.dot(po_ref[...], wbp_ref[...], preferred_element_type=F32)
    merged = gs_ref[...].astype(F32) * ms + gp_ref[...].astype(F32) * mp
    o_ref[...] += jnp.dot(merged.astype(BF16), wo_ref[...], preferred_element_type=F32)

    @pl.when(n == pl.num_programs(1) - 1)
    def _():
        def rows(sl):
            o_ref[sl, :] = x_ref[sl, :] + _rms(o_ref[sl, :], g_ref[...])
        _for_row_blocks(o_ref.shape[0], rows)


def _mix(ys, po, prest, x2, wbs, wbp, wo, g, gate_col0, tm=512, tn=256):
    t, d = x2.shape
    ds_ = ys.shape[1]
    dp = po.shape[1]
    c0 = gate_col0 // tn
    c1 = (gate_col0 + d) // tn
    return pl.pallas_call(
        _mix_kernel,
        out_shape=jax.ShapeDtypeStruct((t, d), F32),
        grid=(t // tm, d // tn),
        in_specs=[pl.BlockSpec((tm, ds_), lambda i, n: (i, 0)),
                  pl.BlockSpec((tm, dp), lambda i, n: (i, 0)),
                  pl.BlockSpec((tm, tn), lambda i, n: (i, c0 + n)),
                  pl.BlockSpec((tm, tn), lambda i, n: (i, c1 + n)),
                  pl.BlockSpec((tm, d), lambda i, n: (i, 0)),
                  pl.BlockSpec((ds_, tn), lambda i, n: (0, n)),
                  pl.BlockSpec((dp, tn), lambda i, n: (0, n)),
                  pl.BlockSpec((tn, d), lambda i, n: (n, 0)),
                  pl.BlockSpec((1, d), lambda i, n: (0, 0))],
        out_specs=pl.BlockSpec((tm, d), lambda i, n: (i, 0)),
        compiler_params=_params(
            ("arbitrary", "arbitrary"),
            _vmem_limit(tm * (ds_ + dp) * 2 + tm * tn * 4 + tm * d * 8
                        + (ds_ + dp) * tn * 2 + tn * d * 2, 0, tm * tn * 16)),
        name="mix",
    )(ys, po, prest, prest, x2, wbs, wbp, wo, g)


def _ffn_kernel(h_ref, g3_ref, wa_ref, wb_ref, cwa_ref, cwb_ref, cba_ref, cbb_ref, wd_ref,
                g4_ref, o_ref, c_sc, carry_sc, *, tiles_per_seq, keep):
    i = pl.program_id(0)
    f = pl.program_id(1)
    tm = h_ref.shape[0]

    @pl.when(f == 0)
    def _():
        def rows(sl):
            c_sc[sl, :] = _rms(h_ref[sl, :], g3_ref[...]).astype(BF16)
            o_ref[sl, :] = jnp.zeros((NORM_ROWS, o_ref.shape[1]), F32)
        _for_row_blocks(tm, rows)

    seq_start = i % tiles_per_seq == 0

    def conv_half(w_ref, cw_ref, cb_ref, slot):
        up = jnp.dot(c_sc[...], w_ref[...], preferred_element_type=F32)
        prev = jnp.where(seq_start, 0.0, carry_sc[slot, f])
        carry_sc[slot, f] = up[tm - keep:]
        ext = jnp.concatenate([prev, up], axis=0)
        cw = cw_ref[...]
        out = cb_ref[...] + cw[CONV_WIDTH - 1:CONV_WIDTH] * up
        for k in range(CONV_WIDTH - 1):
            back = CONV_WIDTH - 1 - k
            out = out + cw[k:k + 1] * ext[keep - back:keep - back + tm]
        return out

    ua = conv_half(wa_ref, cwa_ref, cba_ref, 0)
    ub = conv_half(wb_ref, cwb_ref, cbb_ref, 1)
    fm = (_gelu_tanh(ua) * ub).astype(BF16)
    o_ref[...] += jnp.dot(fm, wd_ref[...], preferred_element_type=F32)

    @pl.when(f == pl.num_programs(1) - 1)
    def _():
        def rows(sl):
            o_ref[sl, :] = h_ref[sl, :] + _rms(o_ref[sl, :], g4_ref[...])
        _for_row_blocks(tm, rows)


def _ffn(h1, g3, w_up, conv_w, conv_b, w_down, g4, seq_len, tm=512, tf=256, keep=8):
    t, d = h1.shape
    d_ff = w_down.shape[0]
    nf = d_ff // tf
    return pl.pallas_call(
        functools.partial(_ffn_kernel, tiles_per_seq=seq_len // tm, keep=keep),
        out_shape=jax.ShapeDtypeStruct((t, d), F32),
        grid=(t // tm, nf),
        in_specs=[pl.BlockSpec((tm, d), lambda i, f: (i, 0)),
                  pl.BlockSpec((1, d), lambda i, f: (0, 0)),
                  pl.BlockSpec((d, tf), lambda i, f: (0, f)),
                  pl.BlockSpec((d, tf), lambda i, f: (0, nf + f)),
                  pl.BlockSpec((CONV_WIDTH, tf), lambda i, f: (0, f)),
                  pl.BlockSpec((CONV_WIDTH, tf), lambda i, f: (0, nf + f)),
                  pl.BlockSpec((1, tf), lambda i, f: (0, f)),
                  pl.BlockSpec((1, tf), lambda i, f: (0, nf + f)),
                  pl.BlockSpec((tf, d), lambda i, f: (f, 0)),
                  pl.BlockSpec((1, d), lambda i, f: (0, 0))],
        out_specs=pl.BlockSpec((tm, d), lambda i, f: (i, 0)),
        scratch_shapes=[pltpu.VMEM((tm, d), BF16),
                        pltpu.VMEM((2, nf, keep, tf), F32)],
        compiler_params=_params(
            ("arbitrary", "arbitrary"),
            _vmem_limit(tm * d * 8 + d * tf * 4 + tf * d * 2,
                        tm * d * 2 + 2 * nf * keep * tf * 4, tm * tf * 32)),
        name="ffn",
    )(h1, g3, w_up, w_up, conv_w, conv_w, conv_b, conv_b, w_down, g4)


def kernel(x, norm_pre_mix, w_in, ssm_lambda_re, ssm_lambda_im, ssm_log_step, ssm_b_re, ssm_b_im,
           ssm_c_re, ssm_c_im, ssm_d, ssm_glu_w, ssm_glu_b, pool_w, pool_b, pool_scale,
           w_branch_ssm, w_branch_pool, w_out, norm_post_mix, norm_pre_ffn, w_up, ffn_conv_w,
           ffn_conv_b, w_down, norm_post_ffn):
    bsz, seq_len, d = x.shape
    depth = w_in.shape[0]
    d_ssm = ssm_d.shape[1]
    d_pool = pool_scale.shape[1]
    n_groups = d_ssm // SSM_GROUP
    assert seq_len // CHUNK == 1 << N_SCAN_PASSES
    t = bsz * seq_len
    h = x.reshape(t, d)
    row = lambda v: v.reshape(1, -1)
    for i in range(depth):
        w_in_b = w_in[i].astype(BF16)
        a = _norm(h, row(norm_pre_mix[i]))
        prest = _inproj(a, w_in_b[:, d_ssm:], d_pool)
        ut = _inproj_ssm(a, w_in_b[:, :d_ssm], seq_len)
        m, rt, et, wsc = _ssm_prep(ssm_lambda_re[i], ssm_lambda_im[i], ssm_log_step[i],
                                   ssm_b_re[i], ssm_b_im[i], ssm_c_re[i], ssm_c_im[i])
        yt = _ssm(ut, m, rt, et, wsc, ssm_d[i].reshape(n_groups, SSM_GROUP, 1),
                  seq_len // CHUNK)
        ys = _glu(yt, ssm_glu_w[i].astype(BF16), row(ssm_glu_b[i]), seq_len)
        po = _pool(prest, pool_w[i].astype(BF16), pool_b[i][:, None, :], row(pool_scale[i]),
                   seq_len, d_pool)
        h = _mix(ys, po, prest, h, w_branch_ssm[i].astype(BF16), w_branch_pool[i].astype(BF16),
                 w_out[i].astype(BF16), row(norm_post_mix[i]), d_pool)
        h = _ffn(h, row(norm_pre_ffn[i]), w_up[i].astype(BF16), ffn_conv_w[i], row(ffn_conv_b[i]),
                 w_down[i].astype(BF16), row(norm_post_ffn[i]), seq_len)
    return h.reshape(bsz, seq_len, d)
```

```python
import functools
import math

import jax
import jax.numpy as jnp
from jax import lax
from jax.experimental import pallas as pl
from jax.experimental.pallas import tpu as pltpu

F32 = jnp.float32
BF16 = jnp.bfloat16

EPS = 1e-6
MIN_NEG_REAL = -1e-4
SSM_GROUP = 16
SSM_STATE = 64
POOL_WINDOWS = (2, 4, 8, 16)
CONV_WIDTH = 3
CHUNK = 16
N_SCAN_PASSES = 7
LANES = 128
FFN_TILE = 512

V7X_VMEM_BYTES = 64 * 1024 * 1024
VMEM_CEILING = V7X_VMEM_BYTES - 6 * 1024 * 1024


def _vmem_limit(pipelined_bytes, scratch_bytes, temp_bytes):
    return min(2 * pipelined_bytes + scratch_bytes + temp_bytes + (4 << 20), VMEM_CEILING)


def _params(semantics, vmem):
    return pltpu.CompilerParams(dimension_semantics=semantics, vmem_limit_bytes=vmem)


def _rms(xf, g):
    ms = jnp.mean(xf * xf, axis=-1, keepdims=True)
    return xf * lax.rsqrt(ms + EPS) * g


NORM_ROWS = 32


def _for_row_blocks(n_rows, body):
    def step(r, carry):
        body(pl.ds(pl.multiple_of(r * NORM_ROWS, NORM_ROWS), NORM_ROWS))
        return carry
    lax.fori_loop(0, n_rows // NORM_ROWS, step, 0)


def _sigmoid(x):
    return 1.0 / (1.0 + jnp.exp(-x))


def _gelu_tanh(x):
    c = math.sqrt(2.0 / math.pi)
    return 0.5 * x * (1.0 + jnp.tanh(c * (x + 0.044715 * (x * x * x))))


def _norm_kernel(x_ref, g_ref, o_ref):
    def rows(sl):
        o_ref[sl, :] = _rms(x_ref[sl, :], g_ref[...]).astype(BF16)
    _for_row_blocks(x_ref.shape[0], rows)


def _norm(x2, g, tm=512):
    t, d = x2.shape
    return pl.pallas_call(
        _norm_kernel,
        out_shape=jax.ShapeDtypeStruct((t, d), BF16),
        grid=(t // tm,),
        in_specs=[pl.BlockSpec((tm, d), lambda i: (i, 0)),
                  pl.BlockSpec((1, d), lambda i: (0, 0))],
        out_specs=pl.BlockSpec((tm, d), lambda i: (i, 0)),
        compiler_params=_params(("arbitrary",), _vmem_limit(tm * d * 6, 0, tm * d * 8)),
        name="norm",
    )(x2, g)


def _inproj_kernel(a_ref, w_ref, o_ref, *, n_plain):
    n = pl.program_id(1)
    p = jnp.dot(a_ref[...], w_ref[...], preferred_element_type=F32)

    @pl.when(n < n_plain)
    def _():
        o_ref[...] = p.astype(BF16)

    @pl.when(n >= n_plain)
    def _():
        o_ref[...] = _sigmoid(p).astype(BF16)


def _inproj(a, w, n_plain_cols, tm=1024, tn=512):
    t, d = a.shape
    n = w.shape[1]
    return pl.pallas_call(
        functools.partial(_inproj_kernel, n_plain=n_plain_cols // tn),
        out_shape=jax.ShapeDtypeStruct((t, n), BF16),
        grid=(t // tm, n // tn),
        in_specs=[pl.BlockSpec((tm, d), lambda i, j: (i, 0)),
                  pl.BlockSpec((d, tn), lambda i, j: (0, j))],
        out_specs=pl.BlockSpec((tm, tn), lambda i, j: (i, j)),
        compiler_params=_params(("arbitrary", "arbitrary"),
                                _vmem_limit((tm * d + d * tn + tm * tn) * 2, 0, tm * tn * 8)),
        name="inproj",
    )(a, w)


def _inproj_ssm_kernel(a_ref, w_ref, o_ref, p_sc):
    chunks = o_ref.shape[-1]
    p = jnp.dot(a_ref[...], w_ref[...], preferred_element_type=F32)
    n_slabs = p_sc.shape[0]
    for s in range(n_slabs):
        p_sc[s] = p[:, s * LANES:(s + 1) * LANES]
    gps = LANES // SSM_GROUP
    for q in range(CHUNK):
        for s in range(n_slabs):
            blk = p_sc[s, pl.ds(q, chunks, stride=CHUNK), :]
            o_ref[s * gps:(s + 1) * gps, q] = (
                blk.T.reshape(gps, SSM_GROUP, chunks).astype(BF16))


def _inproj_ssm(a, w, seq_len, tn=256):
    t, d = a.shape
    n = w.shape[1]
    g = n // SSM_GROUP
    chunks = seq_len // CHUNK
    return pl.pallas_call(
        _inproj_ssm_kernel,
        out_shape=jax.ShapeDtypeStruct((g, CHUNK, SSM_GROUP, t // CHUNK), BF16),
        grid=(t // seq_len, n // tn),
        in_specs=[pl.BlockSpec((seq_len, d), lambda bi, j: (bi, 0)),
                  pl.BlockSpec((d, tn), lambda bi, j: (0, j))],
        out_specs=pl.BlockSpec((tn // SSM_GROUP, CHUNK, SSM_GROUP, chunks),
                               lambda bi, j: (j, 0, 0, bi)),
        scratch_shapes=[pltpu.VMEM((tn // LANES, seq_len, LANES), F32)],
        compiler_params=_params(("arbitrary", "arbitrary"),
                                _vmem_limit((seq_len * d + d * tn + tn * seq_len) * 2,
                                            seq_len * tn * 4, seq_len * tn * 24)),
        name="inproj_ssm",
    )(a, w)


def _ssm_prep_kernel(lre_ref, lim_ref, ls_ref, btr_ref, bti_ref, cr_ref, ci_ref,
                     m_ref, rt_ref, et_ref, w_ref, *, groups):
    p2 = 2 * SSM_STATE
    cq = CHUNK * SSM_GROUP
    lane = lax.broadcasted_iota(jnp.int32, (1, p2), 1)
    is_re = lane < SSM_STATE
    sgn = jnp.where(is_re, 1.0, -1.0).astype(F32)
    lane_blk = lax.broadcasted_iota(jnp.int32, (cq, cq), 1) // SSM_GROUP
    row8 = lax.broadcasted_iota(jnp.int32, (8, p2), 0)
    is_re8 = lax.broadcasted_iota(jnp.int32, (8, p2), 1) < SSM_STATE

    for gi in range(groups):
        lr = jnp.minimum(lre_ref[gi], MIN_NEG_REAL)
        li = lim_ref[gi]
        dt = jnp.exp(ls_ref[gi])
        mag = jnp.exp(lr * dt)
        ang = li * dt
        ar = mag * jnp.cos(ang)
        ai = mag * jnp.sin(ang)
        nr = ar - 1.0
        ni = ai
        den = lr * lr + li * li
        f_re = (nr * lr + ni * li) / den
        f_im = (ni * lr - nr * li) / den
        btr = btr_ref[gi]
        bti = bti_ref[gi]
        bbr = f_re * btr - f_im * bti
        bbi = f_re * bti + f_im * btr
        cr = cr_ref[gi]
        ci = ci_ref[gi]

        pa = [jnp.where(is_re, 1.0, 0.0).astype(F32)]
        pb = [jnp.where(is_re, 0.0, 1.0).astype(F32)]
        for _ in range(CHUNK):
            a_prev, b_prev = pa[-1], pb[-1]
            pa.append(a_prev * ar + b_prev * ai)
            pb.append(b_prev * ar - a_prev * ai)

        gk = [sgn * (cr * pa[k] + ci * pb[k]) for k in range(CHUNK + 1)]
        fs = jnp.concatenate(gk[:CHUNK], axis=0)
        et = jnp.concatenate(gk[1:], axis=0)
        rtt = jnp.concatenate(
            [bbr * pa[CHUNK - 1 - q] + bbi * pb[CHUNK - 1 - q] for q in range(CHUNK)], axis=0)
        bst = jnp.where(is_re, bbr, bbi)
        bst_t = jnp.concatenate([bst] * CHUNK, axis=0)
        kw = lax.dot_general(fs, bst_t, (((1,), (1,)), ((), ())),
                             precision=lax.Precision.HIGHEST, preferred_element_type=F32)
        m = jnp.zeros((cq, cq), F32)
        for q in range(CHUNK):
            if q == 0:
                shifted = kw
            else:
                shifted = jnp.concatenate(
                    [jnp.zeros((q * SSM_GROUP, cq), F32), kw[:cq - q * SSM_GROUP]], axis=0)
            m = jnp.where(lane_blk == q, shifted, m)
        m_ref[gi] = m.astype(BF16)
        rt_ref[gi] = rtt.T.astype(BF16)
        et_ref[gi] = et.astype(BF16)

        wa = jnp.broadcast_to(pa[CHUNK], (8, p2))
        rows = jnp.zeros((8, p2), F32)
        for k in range(N_SCAN_PASSES):
            rows = jnp.where(row8 == k, wa, rows)
            swapped = pltpu.roll(wa, SSM_STATE, 1)
            re2 = jnp.where(is_re8, wa, swapped)
            im2 = jnp.where(is_re8, swapped, wa)
            wb = jnp.where(is_re8, -swapped, swapped)
            wa = wa * re2 + wb * im2
        full = jnp.concatenate([rows, jnp.zeros((p2 - 8, p2), F32)], axis=0)
        w_ref[gi] = full.T


def _ssm_prep(lam_re, lam_im, log_step, b_re, b_im, c_re, c_im, groups=8):
    g, p = lam_re.shape
    p2 = 2 * p
    cq = CHUNK * SSM_GROUP
    dup = lambda v: jnp.concatenate([v, v], axis=-1)
    lre = dup(lam_re)[:, None, :]
    lim = dup(lam_im)[:, None, :]
    ls = jnp.broadcast_to(log_step[:, None, None], (g, 1, p2))
    btr = dup(jnp.swapaxes(b_re, 1, 2))
    bti = dup(jnp.swapaxes(b_im, 1, 2))
    cr = dup(c_re)
    ci = dup(c_im)
    row = pl.BlockSpec((groups, 1, p2), lambda i: (i, 0, 0))
    mat = pl.BlockSpec((groups, SSM_GROUP, p2), lambda i: (i, 0, 0))
    return pl.pallas_call(
        functools.partial(_ssm_prep_kernel, groups=groups),
        out_shape=(jax.ShapeDtypeStruct((g, cq, cq), BF16),
                   jax.ShapeDtypeStruct((g, p2, cq), BF16),
                   jax.ShapeDtypeStruct((g, cq, p2), BF16),
                   jax.ShapeDtypeStruct((g, p2, p2), F32)),
        grid=(g // groups,),
        in_specs=[row, row, row, mat, mat, mat, mat],
        out_specs=(pl.BlockSpec((groups, cq, cq), lambda i: (i, 0, 0)),
                   pl.BlockSpec((groups, p2, cq), lambda i: (i, 0, 0)),
                   pl.BlockSpec((groups, cq, p2), lambda i: (i, 0, 0)),
                   pl.BlockSpec((groups, p2, p2), lambda i: (i, 0, 0))),
        compiler_params=_params(("arbitrary",), _vmem_limit(groups * cq * cq * 4, 0, 8 << 20)),
        name="ssm_prep",
    )(lre, lim, ls, btr, bti, cr, ci)


def _ssm_kernel(u_ref, m_ref, rt_ref, et_ref, w_ref, d_ref, o_ref, *, groups, chunks_per_seq):
    cq = CHUNK * SSM_GROUP
    nc = u_ref.shape[-1]
    pos = lax.broadcasted_iota(jnp.int32, (SSM_STATE, nc), 1) % chunks_per_seq
    for gi in range(groups):
        x3 = u_ref[gi]
        x = x3.reshape(cq, nc)
        y = jnp.dot(m_ref[gi], x, preferred_element_type=F32)
        r = jnp.dot(rt_ref[gi], x, preferred_element_type=F32)
        s_re, s_im = r[:SSM_STATE], r[SSM_STATE:]
        for k in range(N_SCAN_PASSES):
            dist = 1 << k
            wr = w_ref[gi, 0:SSM_STATE, k:k + 1]
            wi = w_ref[gi, SSM_STATE:2 * SSM_STATE, k:k + 1]
            keep = pos >= dist
            sh_re = jnp.where(keep, pltpu.roll(s_re, dist, 1), 0.0)
            sh_im = jnp.where(keep, pltpu.roll(s_im, dist, 1), 0.0)
            s_re, s_im = (s_re + wr * sh_re - wi * sh_im,
                          s_im + wr * sh_im + wi * sh_re)
        first = pos >= 1
        sp = jnp.concatenate([jnp.where(first, pltpu.roll(s_re, 1, 1), 0.0),
                              jnp.where(first, pltpu.roll(s_im, 1, 1), 0.0)], axis=0)
        y = y + jnp.dot(et_ref[gi], sp.astype(BF16), preferred_element_type=F32)
        y3 = y.reshape(CHUNK, SSM_GROUP, nc) + d_ref[gi][None] * x3.astype(F32)
        o_ref[gi] = _gelu_tanh(y3).astype(BF16)


def _ssm(ut, m, rt, et, w, d, chunks_per_seq, groups=4):
    g, q, j, nc = ut.shape
    cq = q * j
    p2 = rt.shape[1]
    blk = lambda *s: pl.BlockSpec((groups,) + s, lambda i: (i,) + (0,) * len(s))
    return pl.pallas_call(
        functools.partial(_ssm_kernel, groups=groups, chunks_per_seq=chunks_per_seq),
        out_shape=jax.ShapeDtypeStruct(ut.shape, BF16),
        grid=(g // groups,),
        in_specs=[blk(q, j, nc), blk(cq, cq), blk(p2, cq), blk(cq, p2), blk(p2, p2), blk(j, 1)],
        out_specs=blk(q, j, nc),
        compiler_params=_params(("arbitrary",), _vmem_limit(groups * cq * nc * 8, 0, 8 << 20)),
        name="ssm",
    )(ut, m, rt, et, w, d)


def _glu_kernel(y_ref, w_ref, b_ref, o_ref, y_sc, *, tn):
    n = pl.program_id(1)
    chunks = y_ref.shape[-1]
    n_slabs = y_sc.shape[0]

    @pl.when(n == 0)
    def _():
        for q in range(CHUNK):
            yq = y_ref[:, q]
            yq = yq.reshape(yq.shape[0] * yq.shape[1], chunks).astype(F32).T
            for s in range(n_slabs):
                y_sc[s, pl.ds(q, chunks, stride=CHUNK), :] = yq[:, s * LANES:(s + 1) * LANES]

    y = jnp.concatenate([y_sc[s] for s in range(n_slabs)], axis=1).astype(BF16)
    z = jnp.dot(y, w_ref[...], preferred_element_type=F32) + b_ref[...]
    per_tile = tn // LANES
    yn = jnp.concatenate([y_sc[n * per_tile + s] for s in range(per_tile)], axis=1)
    o_ref[...] = (yn * _sigmoid(z)).astype(BF16)


def _glu(yt, w, b, seq_len, tn=512):
    g, q, j, nc = yt.shape
    d = g * j
    chunks = seq_len // q
    t = nc * q
    return pl.pallas_call(
        functools.partial(_glu_kernel, tn=tn),
        out_shape=jax.ShapeDtypeStruct((t, d), BF16),
        grid=(nc // chunks, d // tn),
        in_specs=[pl.BlockSpec((g, q, j, chunks), lambda bi, n: (0, 0, 0, bi)),
                  pl.BlockSpec((d, tn), lambda bi, n: (0, n)),
                  pl.BlockSpec((1, tn), lambda bi, n: (0, n))],
        out_specs=pl.BlockSpec((seq_len, tn), lambda bi, n: (bi, n)),
        scratch_shapes=[pltpu.VMEM((d // LANES, seq_len, LANES), F32)],
        compiler_params=_params(("arbitrary", "arbitrary"),
                                _vmem_limit((seq_len * d + d * tn + seq_len * tn) * 2,
                                            seq_len * d * 4, seq_len * d * 2 + seq_len * tn * 12)),
        name="glu",
    )(yt, w, b)


def _pool_kernel(u_ref, w_ref, b_ref, s_ref, o_ref, ext_sc, *, tiles_per_seq, halo):
    i = pl.program_id(0)
    tm = u_ref.shape[0]
    gp = w_ref.shape[1]

    @pl.when(i % tiles_per_seq == 0)
    def _():
        ext_sc[0:halo] = jnp.zeros((halo, ext_sc.shape[1]), BF16)

    @pl.when(i % tiles_per_seq != 0)
    def _():
        ext_sc[0:halo] = ext_sc[tm:tm + halo]

    ext_sc[halo:halo + tm] = u_ref[...]

    t_loc = lax.broadcasted_iota(jnp.int32, (tm, tm + halo), 0)
    s_loc = lax.broadcasted_iota(jnp.int32, (tm, tm + halo), 1)
    lag = t_loc + halo - s_loc
    t_seq = (i % tiles_per_seq) * tm + lax.broadcasted_iota(jnp.int32, (tm, 1), 0)
    for gi, win in enumerate(POOL_WINDOWS):
        band = jnp.where((lag >= 0) & (lag < win), 1.0, 0.0).astype(BF16)
        cols = slice(gi * gp, (gi + 1) * gp)
        wsum = jnp.dot(band, ext_sc[:, cols], preferred_element_type=F32)
        cnt = jnp.minimum(t_seq + 1, win).astype(F32)
        z = wsum / cnt - u_ref[:, cols].astype(F32)
        z = jnp.dot(z.astype(BF16), w_ref[gi], preferred_element_type=F32) + b_ref[gi]
        o_ref[:, cols] = (z * s_ref[:, cols]).astype(BF16)


def _pool(prest, w, b, scale, seq_len, d_pool, tm=512, halo=128):
    t = prest.shape[0]
    ng, gp, _ = w.shape
    return pl.pallas_call(
        functools.partial(_pool_kernel, tiles_per_seq=seq_len // tm, halo=halo),
        out_shape=jax.ShapeDtypeStruct((t, d_pool), BF16),
        grid=(t // tm,),
        in_specs=[pl.BlockSpec((tm, d_pool), lambda i: (i, 0)),
                  pl.BlockSpec((ng, gp, gp), lambda i: (0, 0, 0)),
                  pl.BlockSpec((ng, 1, gp), lambda i: (0, 0, 0)),
                  pl.BlockSpec((1, d_pool), lambda i: (0, 0))],
        out_specs=pl.BlockSpec((tm, d_pool), lambda i: (i, 0)),
        scratch_shapes=[pltpu.VMEM((tm + halo, d_pool), BF16)],
        compiler_params=_params(("arbitrary",),
                                _vmem_limit(tm * d_pool * 4 + ng * gp * gp * 2,
                                            (tm + halo) * d_pool * 2, 8 << 20)),
        name="pool",
    )(prest, w, b, scale)


def _mix_kernel(ys_ref, po_ref, gs_ref, gp_ref, x_ref, wbs_ref, wbp_ref, wo_ref, g_ref, o_ref,
                *, sub):
    n = pl.program_id(1)

    @pl.when(n == 0)
    def _():
        o_ref[...] = jnp.zeros_like(o_ref)

    acc = None
    for s in range(wo_ref.shape[0] // sub):
        cols = slice(s * sub, (s + 1) * sub)
        ms = jnp.dot(ys_ref[...], wbs_ref[:, cols], preferred_element_type=F32)
        mp = jnp.dot(po_ref[...], wbp_ref[:, cols], preferred_element_type=F32)
        merged = gs_ref[:, cols].astype(F32) * ms + gp_ref[:, cols].astype(F32) * mp
        part = jnp.dot(merged.astype(BF16), wo_ref[cols, :], preferred_element_type=F32)
        acc = part if acc is None else acc + part
    o_ref[...] += acc

    @pl.when(n == pl.num_programs(1) - 1)
    def _():
        def rows(sl):
            o_ref[sl, :] = x_ref[sl, :] + _rms(o_ref[sl, :], g_ref[...])
        _for_row_blocks(o_ref.shape[0], rows)


def _mix(ys, po, prest, x2, wbs, wbp, wo, g, gate_col0, tm=512, tn=512, sub=256):
    t, d = x2.shape
    ds_ = ys.shape[1]
    dp = po.shape[1]
    c0 = gate_col0 // tn
    c1 = (gate_col0 + d) // tn
    resident = pl.Buffered(1)
    return pl.pallas_call(
        functools.partial(_mix_kernel, sub=sub),
        out_shape=jax.ShapeDtypeStruct((t, d), F32),
        grid=(t // tm, d // tn),
        in_specs=[pl.BlockSpec((tm, ds_), lambda i, n: (i, 0)),
                  pl.BlockSpec((tm, dp), lambda i, n: (i, 0)),
                  pl.BlockSpec((tm, tn), lambda i, n: (i, c0 + n)),
                  pl.BlockSpec((tm, tn), lambda i, n: (i, c1 + n)),
                  pl.BlockSpec((tm, d), lambda i, n: (i, 0), pipeline_mode=resident),
                  pl.BlockSpec((ds_, tn), lambda i, n: (0, n)),
                  pl.BlockSpec((dp, tn), lambda i, n: (0, n)),
                  pl.BlockSpec((tn, d), lambda i, n: (n, 0)),
                  pl.BlockSpec((1, d), lambda i, n: (0, 0))],
        out_specs=pl.BlockSpec((tm, d), lambda i, n: (i, 0), pipeline_mode=resident),
        compiler_params=_params(
            ("arbitrary", "arbitrary"),
            _vmem_limit(tm * (ds_ + dp) * 2 + tm * tn * 4 + (ds_ + dp) * tn * 2 + tn * d * 2,
                        tm * d * 8, tm * tn * 16)),
        name="mix",
    )(ys, po, prest, prest, x2, wbs, wbp, wo, g)


def _ffn_kernel(h_ref, g3_ref, wa_ref, wb_ref, cwa_ref, cwb_ref, cba_ref, cbb_ref, wd_ref,
                g4_ref, o_ref, c_sc, carry_sc, *, tiles_per_seq, keep, sub):
    i = pl.program_id(0)
    f = pl.program_id(1)
    tm = h_ref.shape[0]
    n_sub = wd_ref.shape[0] // sub

    @pl.when(f == 0)
    def _():
        def rows(sl):
            c_sc[sl, :] = _rms(h_ref[sl, :], g3_ref[...]).astype(BF16)
            o_ref[sl, :] = jnp.zeros((NORM_ROWS, o_ref.shape[1]), F32)
        _for_row_blocks(tm, rows)

    seq_start = i % tiles_per_seq == 0

    def conv_half(w_ref, cw_ref, cb_ref, slot, s):
        cols = slice(s * sub, (s + 1) * sub)
        up = jnp.dot(c_sc[...], w_ref[:, cols], preferred_element_type=F32)
        tile = f * n_sub + s
        prev = jnp.where(seq_start, 0.0, carry_sc[slot, tile])
        carry_sc[slot, tile] = up[tm - keep:]
        ext = jnp.concatenate([prev, up], axis=0)
        cw = cw_ref[:, cols]
        out = cb_ref[:, cols] + cw[CONV_WIDTH - 1:CONV_WIDTH] * up
        for k in range(CONV_WIDTH - 1):
            back = CONV_WIDTH - 1 - k
            out = out + cw[k:k + 1] * ext[keep - back:keep - back + tm]
        return out

    acc = None
    for s in range(n_sub):
        ua = conv_half(wa_ref, cwa_ref, cba_ref, 0, s)
        ub = conv_half(wb_ref, cwb_ref, cbb_ref, 1, s)
        fm = (_gelu_tanh(ua) * ub).astype(BF16)
        part = jnp.dot(fm, wd_ref[s * sub:(s + 1) * sub, :], preferred_element_type=F32)
        acc = part if acc is None else acc + part
    o_ref[...] += acc

    @pl.when(f == pl.num_programs(1) - 1)
    def _():
        def rows(sl):
            o_ref[sl, :] = h_ref[sl, :] + _rms(o_ref[sl, :], g4_ref[...])
        _for_row_blocks(tm, rows)


def _pad_ffn_weights(w_up, conv_w, conv_b, w_down, tf):
    d, d_ff2 = w_up.shape
    d_ff = d_ff2 // 2
    pad = -d_ff % tf
    halves = lambda v: jnp.pad(v.reshape(v.shape[0], 2, d_ff), ((0, 0), (0, 0), (0, pad))
                               ).reshape(v.shape[0], 2 * (d_ff + pad))
    return (halves(w_up).astype(BF16), halves(conv_w), halves(conv_b),
            jnp.pad(w_down, ((0, pad), (0, 0))).astype(BF16))


def _ffn(h1, g3, w_up, conv_w, conv_b, w_down, g4, seq_len, tm=512, tf=512, sub=256, keep=8):
    t, d = h1.shape
    d_ff = w_down.shape[0]
    nf = d_ff // tf
    resident = pl.Buffered(1)
    return pl.pallas_call(
        functools.partial(_ffn_kernel, tiles_per_seq=seq_len // tm, keep=keep, sub=sub),
        out_shape=jax.ShapeDtypeStruct((t, d), F32),
        grid=(t // tm, nf),
        in_specs=[pl.BlockSpec((tm, d), lambda i, f: (i, 0), pipeline_mode=resident),
                  pl.BlockSpec((1, d), lambda i, f: (0, 0)),
                  pl.BlockSpec((d, tf), lambda i, f: (0, f)),
                  pl.BlockSpec((d, tf), lambda i, f: (0, nf + f)),
                  pl.BlockSpec((CONV_WIDTH, tf), lambda i, f: (0, f)),
                  pl.BlockSpec((CONV_WIDTH, tf), lambda i, f: (0, nf + f)),
                  pl.BlockSpec((1, tf), lambda i, f: (0, f)),
                  pl.BlockSpec((1, tf), lambda i, f: (0, nf + f)),
                  pl.BlockSpec((tf, d), lambda i, f: (f, 0)),
                  pl.BlockSpec((1, d), lambda i, f: (0, 0))],
        out_specs=pl.BlockSpec((tm, d), lambda i, f: (i, 0), pipeline_mode=resident),
        scratch_shapes=[pltpu.VMEM((tm, d), BF16),
                        pltpu.VMEM((2, d_ff // sub, keep, sub), F32)],
        compiler_params=_params(
            ("arbitrary", "arbitrary"),
            _vmem_limit(d * tf * 4 + tf * d * 2,
                        tm * d * 8 + tm * d * 2 + 2 * d_ff * keep * 4, tm * tf * 32)),
        name="ffn",
    )(h1, g3, w_up, w_up, conv_w, conv_w, conv_b, conv_b, w_down, g4)


def kernel(x, norm_pre_mix, w_in, ssm_lambda_re, ssm_lambda_im, ssm_log_step, ssm_b_re, ssm_b_im,
           ssm_c_re, ssm_c_im, ssm_d, ssm_glu_w, ssm_glu_b, pool_w, pool_b, pool_scale,
           w_branch_ssm, w_branch_pool, w_out, norm_post_mix, norm_pre_ffn, w_up, ffn_conv_w,
           ffn_conv_b, w_down, norm_post_ffn):
    bsz, seq_len, d = x.shape
    depth = w_in.shape[0]
    d_ssm = ssm_d.shape[1]
    d_pool = pool_scale.shape[1]
    n_groups = d_ssm // SSM_GROUP
    assert seq_len // CHUNK == 1 << N_SCAN_PASSES
    t = bsz * seq_len
    h = x.reshape(t, d)
    row = lambda v: v.reshape(1, -1)
    for i in range(depth):
        w_in_b = w_in[i].astype(BF16)
        a = _norm(h, row(norm_pre_mix[i]))
        prest = _inproj(a, w_in_b[:, d_ssm:], d_pool)
        ut = _inproj_ssm(a, w_in_b[:, :d_ssm], seq_len)
        m, rt, et, wsc = _ssm_prep(ssm_lambda_re[i], ssm_lambda_im[i], ssm_log_step[i],
                                   ssm_b_re[i], ssm_b_im[i], ssm_c_re[i], ssm_c_im[i])
        yt = _ssm(ut, m, rt, et, wsc, ssm_d[i].reshape(n_groups, SSM_GROUP, 1),
                  seq_len // CHUNK)
        ys = _glu(yt, ssm_glu_w[i].astype(BF16), row(ssm_glu_b[i]), seq_len)
        po = _pool(prest, pool_w[i].astype(BF16), pool_b[i][:, None, :], row(pool_scale[i]),
                   seq_len, d_pool)
        h = _mix(ys, po, prest, h, w_branch_ssm[i].astype(BF16), w_branch_pool[i].astype(BF16),
                 w_out[i].astype(BF16), row(norm_post_mix[i]), d_pool)
        ffn_w = _pad_ffn_weights(w_up[i], ffn_conv_w[i], row(ffn_conv_b[i]), w_down[i], FFN_TILE)
        h = _ffn(h, row(norm_pre_ffn[i]), *ffn_w, row(norm_post_ffn[i]), seq_len, tf=FFN_TILE)
    return h.reshape(bsz, seq_len, d)
```

```python
import functools
import math

import jax
import jax.numpy as jnp
from jax import lax
from jax.experimental import pallas as pl
from jax.experimental.pallas import tpu as pltpu

F32 = jnp.float32
BF16 = jnp.bfloat16

EPS = 1e-6
MIN_NEG_REAL = -1e-4
SSM_GROUP = 16
SSM_STATE = 64
POOL_WINDOWS = (2, 4, 8, 16)
CONV_WIDTH = 3
CHUNK = 16
N_SCAN_PASSES = 7
LANES = 128

V7X_VMEM_BYTES = 64 * 1024 * 1024
VMEM_CEILING = V7X_VMEM_BYTES - 6 * 1024 * 1024


def _vmem_limit(pipelined_bytes, scratch_bytes, temp_bytes):
    return min(2 * pipelined_bytes + scratch_bytes + temp_bytes + (4 << 20), VMEM_CEILING)


def _params(semantics, vmem):
    return pltpu.CompilerParams(dimension_semantics=semantics, vmem_limit_bytes=vmem)


def _rms(xf, g):
    ms = jnp.mean(xf * xf, axis=-1, keepdims=True)
    return xf * lax.rsqrt(ms + EPS) * g


NORM_ROWS = 32


def _for_row_blocks(n_rows, body):
    def step(r, carry):
        body(pl.ds(pl.multiple_of(r * NORM_ROWS, NORM_ROWS), NORM_ROWS))
        return carry
    lax.fori_loop(0, n_rows // NORM_ROWS, step, 0)


def _sigmoid(x):
    return 1.0 / (1.0 + jnp.exp(-x))


def _gelu_tanh(x):
    c = math.sqrt(2.0 / math.pi)
    return 0.5 * x * (1.0 + jnp.tanh(c * (x + 0.044715 * (x * x * x))))


def _norm_kernel(x_ref, g_ref, o_ref):
    def rows(sl):
        o_ref[sl, :] = _rms(x_ref[sl, :], g_ref[...]).astype(BF16)
    _for_row_blocks(x_ref.shape[0], rows)


def _norm(x2, g, tm=512):
    t, d = x2.shape
    return pl.pallas_call(
        _norm_kernel,
        out_shape=jax.ShapeDtypeStruct((t, d), BF16),
        grid=(t // tm,),
        in_specs=[pl.BlockSpec((tm, d), lambda i: (i, 0)),
                  pl.BlockSpec((1, d), lambda i: (0, 0))],
        out_specs=pl.BlockSpec((tm, d), lambda i: (i, 0)),
        compiler_params=_params(("arbitrary",), _vmem_limit(tm * d * 6, 0, tm * d * 8)),
        name="norm",
    )(x2, g)


def _inproj_kernel(a_ref, w_ref, o_ref, *, n_plain, row_blocks):
    gate = pl.program_id(1) >= n_plain
    w = w_ref[...].astype(BF16)
    rows = a_ref.shape[0] // row_blocks
    for r0 in range(0, a_ref.shape[0], rows):
        p = jnp.dot(a_ref[r0:r0 + rows, :], w, preferred_element_type=F32)
        o_ref[r0:r0 + rows, :] = jnp.where(gate, _sigmoid(p), p).astype(BF16)


def _inproj(a, w, col0, n_plain_cols, tm=1024, tn=512, row_blocks=4):
    t, d = a.shape
    n = w.shape[1] - col0
    c0 = col0 // tn
    return pl.pallas_call(
        functools.partial(_inproj_kernel, n_plain=n_plain_cols // tn, row_blocks=row_blocks),
        out_shape=jax.ShapeDtypeStruct((t, n), BF16),
        grid=(t // tm, n // tn),
        in_specs=[pl.BlockSpec((tm, d), lambda i, j: (i, 0)),
                  pl.BlockSpec((d, tn), lambda i, j: (0, c0 + j))],
        out_specs=pl.BlockSpec((tm, tn), lambda i, j: (i, j)),
        compiler_params=_params(("arbitrary", "arbitrary"),
                                _vmem_limit(tm * d * 2 + d * tn * 4 + tm * tn * 2, 0,
                                            d * tn * 2 + tm * tn * 8)),
        name="inproj",
    )(a, w)


def _inproj_ssm_kernel(a_ref, w_ref, o_ref, p_sc, *, row_blocks):
    chunks = o_ref.shape[-1]
    n_slabs = p_sc.shape[0]
    w = w_ref[...].astype(BF16)
    rows = a_ref.shape[0] // row_blocks
    for r0 in range(0, a_ref.shape[0], rows):
        p = jnp.dot(a_ref[r0:r0 + rows, :], w, preferred_element_type=F32)
        for s in range(n_slabs):
            p_sc[s, r0:r0 + rows] = p[:, s * LANES:(s + 1) * LANES]
    gps = LANES // SSM_GROUP
    for q in range(CHUNK):
        for s in range(n_slabs):
            blk = p_sc[s, pl.ds(q, chunks, stride=CHUNK), :]
            o_ref[s * gps:(s + 1) * gps, q] = (
                blk.T.reshape(gps, SSM_GROUP, chunks).astype(BF16))


def _inproj_ssm(a, w, n, seq_len, tn=256, row_blocks=8):
    t, d = a.shape
    g = n // SSM_GROUP
    chunks = seq_len // CHUNK
    return pl.pallas_call(
        functools.partial(_inproj_ssm_kernel, row_blocks=row_blocks),
        out_shape=jax.ShapeDtypeStruct((g, CHUNK, SSM_GROUP, t // CHUNK), BF16),
        grid=(t // seq_len, n // tn),
        in_specs=[pl.BlockSpec((seq_len, d), lambda bi, j: (bi, 0)),
                  pl.BlockSpec((d, tn), lambda bi, j: (0, j))],
        out_specs=pl.BlockSpec((tn // SSM_GROUP, CHUNK, SSM_GROUP, chunks),
                               lambda bi, j: (j, 0, 0, bi)),
        scratch_shapes=[pltpu.VMEM((tn // LANES, seq_len, LANES), F32)],
        compiler_params=_params(("arbitrary", "arbitrary"),
                                _vmem_limit(seq_len * d * 2 + d * tn * 4 + tn * seq_len * 2,
                                            seq_len * tn * 4, d * tn * 2 + seq_len * tn * 16)),
        name="inproj_ssm",
    )(a, w)


def _ssm_prep_kernel(lre_ref, lim_ref, ls_ref, btr_ref, bti_ref, cr_ref, ci_ref,
                     m_ref, rt_ref, et_ref, w_ref, *, groups):
    p2 = 2 * SSM_STATE
    cq = CHUNK * SSM_GROUP
    lane = lax.broadcasted_iota(jnp.int32, (1, p2), 1)
    is_re = lane < SSM_STATE
    sgn = jnp.where(is_re, 1.0, -1.0).astype(F32)
    lane_blk = lax.broadcasted_iota(jnp.int32, (cq, cq), 1) // SSM_GROUP
    row8 = lax.broadcasted_iota(jnp.int32, (8, p2), 0)
    is_re8 = lax.broadcasted_iota(jnp.int32, (8, p2), 1) < SSM_STATE

    for gi in range(groups):
        lr = jnp.minimum(lre_ref[gi], MIN_NEG_REAL)
        li = lim_ref[gi]
        dt = jnp.exp(ls_ref[gi])
        mag = jnp.exp(lr * dt)
        ang = li * dt
        ar = mag * jnp.cos(ang)
        ai = mag * jnp.sin(ang)
        nr = ar - 1.0
        ni = ai
        den = lr * lr + li * li
        f_re = (nr * lr + ni * li) / den
        f_im = (ni * lr - nr * li) / den
        btr = btr_ref[gi]
        bti = bti_ref[gi]
        bbr = f_re * btr - f_im * bti
        bbi = f_re * bti + f_im * btr
        cr = cr_ref[gi]
        ci = ci_ref[gi]

        pa = [jnp.where(is_re, 1.0, 0.0).astype(F32)]
        pb = [jnp.where(is_re, 0.0, 1.0).astype(F32)]
        for _ in range(CHUNK):
            a_prev, b_prev = pa[-1], pb[-1]
            pa.append(a_prev * ar + b_prev * ai)
            pb.append(b_prev * ar - a_prev * ai)

        gk = [sgn * (cr * pa[k] + ci * pb[k]) for k in range(CHUNK + 1)]
        fs = jnp.concatenate(gk[:CHUNK], axis=0)
        et = jnp.concatenate(gk[1:], axis=0)
        rtt = jnp.concatenate(
            [bbr * pa[CHUNK - 1 - q] + bbi * pb[CHUNK - 1 - q] for q in range(CHUNK)], axis=0)
        bst = jnp.where(is_re, bbr, bbi)
        bst_t = jnp.concatenate([bst] * CHUNK, axis=0)
        kw = lax.dot_general(fs, bst_t, (((1,), (1,)), ((), ())),
                             precision=lax.Precision.HIGHEST, preferred_element_type=F32)
        m = jnp.zeros((cq, cq), F32)
        for q in range(CHUNK):
            if q == 0:
                shifted = kw
            else:
                shifted = jnp.concatenate(
                    [jnp.zeros((q * SSM_GROUP, cq), F32), kw[:cq - q * SSM_GROUP]], axis=0)
            m = jnp.where(lane_blk == q, shifted, m)
        m_ref[gi] = m.astype(BF16)
        rt_ref[gi] = rtt.T.astype(BF16)
        et_ref[gi] = et.astype(BF16)

        wa = jnp.broadcast_to(pa[CHUNK], (8, p2))
        rows = jnp.zeros((8, p2), F32)
        for k in range(N_SCAN_PASSES):
            rows = jnp.where(row8 == k, wa, rows)
            swapped = pltpu.roll(wa, SSM_STATE, 1)
            re2 = jnp.where(is_re8, wa, swapped)
            im2 = jnp.where(is_re8, swapped, wa)
            wb = jnp.where(is_re8, -swapped, swapped)
            wa = wa * re2 + wb * im2
        full = jnp.concatenate([rows, jnp.zeros((p2 - 8, p2), F32)], axis=0)
        w_ref[gi] = full.T


def _ssm_prep(lam_re, lam_im, log_step, b_re, b_im, c_re, c_im, groups=8):
    g, p = lam_re.shape
    p2 = 2 * p
    cq = CHUNK * SSM_GROUP
    dup = lambda v: jnp.concatenate([v, v], axis=-1)
    lre = dup(lam_re)[:, None, :]
    lim = dup(lam_im)[:, None, :]
    ls = jnp.broadcast_to(log_step[:, None, None], (g, 1, p2))
    btr = dup(jnp.swapaxes(b_re, 1, 2))
    bti = dup(jnp.swapaxes(b_im, 1, 2))
    cr = dup(c_re)
    ci = dup(c_im)
    row = pl.BlockSpec((groups, 1, p2), lambda i: (i, 0, 0))
    mat = pl.BlockSpec((groups, SSM_GROUP, p2), lambda i: (i, 0, 0))
    return pl.pallas_call(
        functools.partial(_ssm_prep_kernel, groups=groups),
        out_shape=(jax.ShapeDtypeStruct((g, cq, cq), BF16),
                   jax.ShapeDtypeStruct((g, p2, cq), BF16),
                   jax.ShapeDtypeStruct((g, cq, p2), BF16),
                   jax.ShapeDtypeStruct((g, p2, p2), F32)),
        grid=(g // groups,),
        in_specs=[row, row, row, mat, mat, mat, mat],
        out_specs=(pl.BlockSpec((groups, cq, cq), lambda i: (i, 0, 0)),
                   pl.BlockSpec((groups, p2, cq), lambda i: (i, 0, 0)),
                   pl.BlockSpec((groups, cq, p2), lambda i: (i, 0, 0)),
                   pl.BlockSpec((groups, p2, p2), lambda i: (i, 0, 0))),
        compiler_params=_params(("arbitrary",), _vmem_limit(groups * cq * cq * 4, 0, 8 << 20)),
        name="ssm_prep",
    )(lre, lim, ls, btr, bti, cr, ci)


def _ssm_kernel(u_ref, m_ref, rt_ref, et_ref, w_ref, d_ref, o_ref, *, groups, chunks_per_seq):
    cq = CHUNK * SSM_GROUP
    nc = u_ref.shape[-1]
    pos = lax.broadcasted_iota(jnp.int32, (SSM_STATE, nc), 1) % chunks_per_seq
    for gi in range(groups):
        x3 = u_ref[gi]
        x = x3.reshape(cq, nc)
        y = jnp.dot(m_ref[gi], x, preferred_element_type=F32)
        r = jnp.dot(rt_ref[gi], x, preferred_element_type=F32)
        s_re, s_im = r[:SSM_STATE], r[SSM_STATE:]
        for k in range(N_SCAN_PASSES):
            dist = 1 << k
            wr = w_ref[gi, 0:SSM_STATE, k:k + 1]
            wi = w_ref[gi, SSM_STATE:2 * SSM_STATE, k:k + 1]
            keep = pos >= dist
            sh_re = jnp.where(keep, pltpu.roll(s_re, dist, 1), 0.0)
            sh_im = jnp.where(keep, pltpu.roll(s_im, dist, 1), 0.0)
            s_re, s_im = (s_re + wr * sh_re - wi * sh_im,
                          s_im + wr * sh_im + wi * sh_re)
        first = pos >= 1
        sp = jnp.concatenate([jnp.where(first, pltpu.roll(s_re, 1, 1), 0.0),
                              jnp.where(first, pltpu.roll(s_im, 1, 1), 0.0)], axis=0)
        y = y + jnp.dot(et_ref[gi], sp.astype(BF16), preferred_element_type=F32)
        y3 = y.reshape(CHUNK, SSM_GROUP, nc) + d_ref[gi][None] * x3.astype(F32)
        o_ref[gi] = _gelu_tanh(y3).astype(BF16)


def _ssm(ut, m, rt, et, w, d, chunks_per_seq, groups=4):
    g, q, j, nc = ut.shape
    cq = q * j
    p2 = rt.shape[1]
    blk = lambda *s: pl.BlockSpec((groups,) + s, lambda i: (i,) + (0,) * len(s))
    return pl.pallas_call(
        functools.partial(_ssm_kernel, groups=groups, chunks_per_seq=chunks_per_seq),
        out_shape=jax.ShapeDtypeStruct(ut.shape, BF16),
        grid=(g // groups,),
        in_specs=[blk(q, j, nc), blk(cq, cq), blk(p2, cq), blk(cq, p2), blk(p2, p2), blk(j, 1)],
        out_specs=blk(q, j, nc),
        compiler_params=_params(("arbitrary",), _vmem_limit(groups * cq * nc * 8, 0, 8 << 20)),
        name="ssm",
    )(ut, m, rt, et, w, d)


def _glu_kernel(y_ref, w_ref, b_ref, o_ref, y_sc, *, tn):
    n = pl.program_id(1)
    chunks = y_ref.shape[-1]
    n_slabs = y_sc.shape[0]

    @pl.when(n == 0)
    def _():
        for q in range(CHUNK):
            yq = y_ref[:, q]
            yq = yq.reshape(yq.shape[0] * yq.shape[1], chunks).astype(F32).T
            for s in range(n_slabs):
                y_sc[s, pl.ds(q, chunks, stride=CHUNK), :] = yq[:, s * LANES:(s + 1) * LANES]

    y = jnp.concatenate([y_sc[s] for s in range(n_slabs)], axis=1).astype(BF16)
    z = jnp.dot(y, w_ref[...], preferred_element_type=F32) + b_ref[...]
    per_tile = tn // LANES
    yn = jnp.concatenate([y_sc[n * per_tile + s] for s in range(per_tile)], axis=1)
    o_ref[...] = (yn * _sigmoid(z)).astype(BF16)


def _glu(yt, w, b, seq_len, tn=512):
    g, q, j, nc = yt.shape
    d = g * j
    chunks = seq_len // q
    t = nc * q
    return pl.pallas_call(
        functools.partial(_glu_kernel, tn=tn),
        out_shape=jax.ShapeDtypeStruct((t, d), BF16),
        grid=(nc // chunks, d // tn),
        in_specs=[pl.BlockSpec((g, q, j, chunks), lambda bi, n: (0, 0, 0, bi)),
                  pl.BlockSpec((d, tn), lambda bi, n: (0, n)),
                  pl.BlockSpec((1, tn), lambda bi, n: (0, n))],
        out_specs=pl.BlockSpec((seq_len, tn), lambda bi, n: (bi, n)),
        scratch_shapes=[pltpu.VMEM((d // LANES, seq_len, LANES), F32)],
        compiler_params=_params(("arbitrary", "arbitrary"),
                                _vmem_limit((seq_len * d + d * tn + seq_len * tn) * 2,
                                            seq_len * d * 4, seq_len * d * 2 + seq_len * tn * 12)),
        name="glu",
    )(yt, w, b)


def _pool_kernel(u_ref, w_ref, b_ref, s_ref, o_ref, ext_sc, *, tiles_per_seq, halo):
    i = pl.program_id(0)
    tm = u_ref.shape[0]
    gp = w_ref.shape[1]

    @pl.when(i % tiles_per_seq == 0)
    def _():
        ext_sc[0:halo] = jnp.zeros((halo, ext_sc.shape[1]), BF16)

    @pl.when(i % tiles_per_seq != 0)
    def _():
        ext_sc[0:halo] = ext_sc[tm:tm + halo]

    ext_sc[halo:halo + tm] = u_ref[...]

    t_loc = lax.broadcasted_iota(jnp.int32, (tm, tm + halo), 0)
    s_loc = lax.broadcasted_iota(jnp.int32, (tm, tm + halo), 1)
    lag = t_loc + halo - s_loc
    t_seq = (i % tiles_per_seq) * tm + lax.broadcasted_iota(jnp.int32, (tm, 1), 0)
    for gi, win in enumerate(POOL_WINDOWS):
        band = jnp.where((lag >= 0) & (lag < win), 1.0, 0.0).astype(BF16)
        cols = slice(gi * gp, (gi + 1) * gp)
        wsum = jnp.dot(band, ext_sc[:, cols], preferred_element_type=F32)
        cnt = jnp.minimum(t_seq + 1, win).astype(F32)
        z = wsum / cnt - u_ref[:, cols].astype(F32)
        z = jnp.dot(z.astype(BF16), w_ref[gi], preferred_element_type=F32) + b_ref[gi]
        o_ref[:, cols] = (z * s_ref[:, cols]).astype(BF16)


def _pool(prest, w, b, scale, seq_len, d_pool, tm=512, halo=128):
    t = prest.shape[0]
    ng, gp, _ = w.shape
    return pl.pallas_call(
        functools.partial(_pool_kernel, tiles_per_seq=seq_len // tm, halo=halo),
        out_shape=jax.ShapeDtypeStruct((t, d_pool), BF16),
        grid=(t // tm,),
        in_specs=[pl.BlockSpec((tm, d_pool), lambda i: (i, 0)),
                  pl.BlockSpec((ng, gp, gp), lambda i: (0, 0, 0)),
                  pl.BlockSpec((ng, 1, gp), lambda i: (0, 0, 0)),
                  pl.BlockSpec((1, d_pool), lambda i: (0, 0))],
        out_specs=pl.BlockSpec((tm, d_pool), lambda i: (i, 0)),
        scratch_shapes=[pltpu.VMEM((tm + halo, d_pool), BF16)],
        compiler_params=_params(("arbitrary",),
                                _vmem_limit(tm * d_pool * 4 + ng * gp * gp * 2,
                                            (tm + halo) * d_pool * 2, 8 << 20)),
        name="pool",
    )(prest, w, b, scale)


def _mix_kernel(ys_ref, po_ref, gs_ref, gp_ref, x_ref, wbs_ref, wbp_ref, wo_ref, g_ref, o_ref,
                *, sub):
    n = pl.program_id(1)

    @pl.when(n == 0)
    def _():
        o_ref[...] = jnp.zeros_like(o_ref)

    subs = [slice(s * sub, (s + 1) * sub) for s in range(wo_ref.shape[0] // sub)]
    branch = [(jnp.dot(ys_ref[...], wbs_ref[:, cols], preferred_element_type=F32),
               jnp.dot(po_ref[...], wbp_ref[:, cols], preferred_element_type=F32))
              for cols in subs]
    acc = None
    for cols, (ms, mp) in zip(subs, branch):
        merged = gs_ref[:, cols].astype(F32) * ms + gp_ref[:, cols].astype(F32) * mp
        part = jnp.dot(merged.astype(BF16), wo_ref[cols, :], preferred_element_type=F32)
        acc = part if acc is None else acc + part
    o_ref[...] += acc

    @pl.when(n == pl.num_programs(1) - 1)
    def _():
        def rows(sl):
            o_ref[sl, :] = x_ref[sl, :] + _rms(o_ref[sl, :], g_ref[...])
        _for_row_blocks(o_ref.shape[0], rows)


def _mix(ys, po, prest, x2, wbs, wbp, wo, g, gate_col0, tm=512, tn=512, sub=256):
    t, d = x2.shape
    ds_ = ys.shape[1]
    dp = po.shape[1]
    c0 = gate_col0 // tn
    c1 = (gate_col0 + d) // tn
    resident = pl.Buffered(1)
    return pl.pallas_call(
        functools.partial(_mix_kernel, sub=sub),
        out_shape=jax.ShapeDtypeStruct((t, d), F32),
        grid=(t // tm, d // tn),
        in_specs=[pl.BlockSpec((tm, ds_), lambda i, n: (i, 0)),
                  pl.BlockSpec((tm, dp), lambda i, n: (i, 0)),
                  pl.BlockSpec((tm, tn), lambda i, n: (i, c0 + n)),
                  pl.BlockSpec((tm, tn), lambda i, n: (i, c1 + n)),
                  pl.BlockSpec((tm, d), lambda i, n: (i, 0), pipeline_mode=resident),
                  pl.BlockSpec((ds_, tn), lambda i, n: (0, n)),
                  pl.BlockSpec((dp, tn), lambda i, n: (0, n)),
                  pl.BlockSpec((tn, d), lambda i, n: (n, 0)),
                  pl.BlockSpec((1, d), lambda i, n: (0, 0))],
        out_specs=pl.BlockSpec((tm, d), lambda i, n: (i, 0), pipeline_mode=resident),
        compiler_params=_params(
            ("arbitrary", "arbitrary"),
            _vmem_limit(tm * (ds_ + dp) * 2 + tm * tn * 4 + (ds_ + dp) * tn * 2 + tn * d * 2,
                        tm * d * 8, tm * tn * 16)),
        name="mix",
    )(ys, po, prest, prest, x2, wbs, wbp, wo, g)


def _ffn_kernel(*refs, n_sub, n_tiles, tiles_per_seq, keep, up_row_blocks):
    h_ref, g3_ref = refs[:2]
    groups = [refs[2 + k * n_sub:2 + (k + 1) * n_sub] for k in range(7)]
    wa_refs, wb_refs, cwa_refs, cwb_refs, cba_refs, cbb_refs, wd_refs = groups
    g4_ref, o_ref, c_sc, carry_sc, up_sc = refs[2 + 7 * n_sub:]
    i = pl.program_id(0)
    f = pl.program_id(1)
    tm = h_ref.shape[0]

    @pl.when(f == 0)
    def _():
        def rows(sl):
            c_sc[sl, :] = _rms(h_ref[sl, :], g3_ref[...]).astype(BF16)
            o_ref[sl, :] = jnp.zeros((NORM_ROWS, o_ref.shape[1]), F32)
        _for_row_blocks(tm, rows)

    seq_start = i % tiles_per_seq == 0

    mh = tm // up_row_blocks
    for s in range(n_sub):
        tile = f * n_sub + s
        for half, w_refs in enumerate((wa_refs, wb_refs)):
            slot = 2 * s + half
            up_sc[slot, 0:keep] = jnp.where(seq_start, 0.0, carry_sc[half, tile])
            for r0 in range(0, tm, mh):
                up = jnp.dot(c_sc[r0:r0 + mh, :], w_refs[s][...], preferred_element_type=F32)
                up_sc[slot, keep + r0:keep + r0 + mh] = up
            carry_sc[half, tile] = up[mh - keep:]

    def conv(slot, cw_ref, cb_ref):
        cw = cw_ref[...]
        out = cb_ref[...]
        for k in range(CONV_WIDTH):
            back = CONV_WIDTH - 1 - k
            out = out + cw[k:k + 1] * up_sc[slot, keep - back:keep - back + tm]
        return out

    acc = None
    for s in range(n_sub):
        ua = conv(2 * s, cwa_refs[s], cba_refs[s])
        ub = conv(2 * s + 1, cwb_refs[s], cbb_refs[s])
        fm = jnp.where(f * n_sub + s < n_tiles, _gelu_tanh(ua) * ub, 0.0).astype(BF16)
        part = jnp.dot(fm, wd_refs[s][...], preferred_element_type=F32)
        acc = part if acc is None else acc + part
    o_ref[...] += acc

    @pl.when(f == pl.num_programs(1) - 1)
    def _():
        def rows(sl):
            o_ref[sl, :] = h_ref[sl, :] + _rms(o_ref[sl, :], g4_ref[...])
        _for_row_blocks(tm, rows)


def _ffn(h1, g3, w_up, conv_w, conv_b, w_down, g4, seq_len, tm=512, sub=256, n_sub=2, keep=8,
         up_row_blocks=4):
    t, d = h1.shape
    d_ff = w_down.shape[0]
    n_tiles = d_ff // sub
    nf = pl.cdiv(n_tiles, n_sub)
    resident = pl.Buffered(1)

    def tile_of(s):
        return lambda f: jnp.minimum(f * n_sub + s, n_tiles - 1)

    def per_sub(shape, index):
        return [pl.BlockSpec(shape, functools.partial(index, tile_of(s))) for s in range(n_sub)]

    col_a = lambda tile, i, f: (0, tile(f))
    col_b = lambda tile, i, f: (0, n_tiles + tile(f))
    row_d = lambda tile, i, f: (tile(f), 0)
    in_specs = ([pl.BlockSpec((tm, d), lambda i, f: (i, 0), pipeline_mode=resident),
                 pl.BlockSpec((1, d), lambda i, f: (0, 0))]
                + per_sub((d, sub), col_a) + per_sub((d, sub), col_b)
                + per_sub((CONV_WIDTH, sub), col_a) + per_sub((CONV_WIDTH, sub), col_b)
                + per_sub((1, sub), col_a) + per_sub((1, sub), col_b)
                + per_sub((sub, d), row_d)
                + [pl.BlockSpec((1, d), lambda i, f: (0, 0))])
    operands = ([h1, g3] + [w_up] * (2 * n_sub) + [conv_w] * (2 * n_sub)
                + [conv_b] * (2 * n_sub) + [w_down] * n_sub + [g4])
    tf = sub * n_sub
    return pl.pallas_call(
        functools.partial(_ffn_kernel, n_sub=n_sub, n_tiles=n_tiles,
                          tiles_per_seq=seq_len // tm, keep=keep, up_row_blocks=up_row_blocks),
        out_shape=jax.ShapeDtypeStruct((t, d), F32),
        grid=(t // tm, nf),
        in_specs=in_specs,
        out_specs=pl.BlockSpec((tm, d), lambda i, f: (i, 0), pipeline_mode=resident),
        scratch_shapes=[pltpu.VMEM((tm, d), BF16),
                        pltpu.VMEM((2, nf * n_sub, keep, sub), F32),
                        pltpu.VMEM((2 * n_sub, keep + tm, sub), F32)],
        compiler_params=_params(
            ("arbitrary", "arbitrary"),
            _vmem_limit(d * tf * 4 + tf * d * 2,
                        tm * d * 8 + tm * d * 2 + 2 * nf * tf * keep * 4
                        + 2 * (keep + tm) * tf * 4, tm * tf * 24)),
        name="ffn",
    )(*operands)


def kernel(x, norm_pre_mix, w_in, ssm_lambda_re, ssm_lambda_im, ssm_log_step, ssm_b_re, ssm_b_im,
           ssm_c_re, ssm_c_im, ssm_d, ssm_glu_w, ssm_glu_b, pool_w, pool_b, pool_scale,
           w_branch_ssm, w_branch_pool, w_out, norm_post_mix, norm_pre_ffn, w_up, ffn_conv_w,
           ffn_conv_b, w_down, norm_post_ffn):
    bsz, seq_len, d = x.shape
    depth = w_in.shape[0]
    d_ssm = ssm_d.shape[1]
    d_pool = pool_scale.shape[1]
    n_groups = d_ssm // SSM_GROUP
    assert seq_len // CHUNK == 1 << N_SCAN_PASSES
    t = bsz * seq_len
    h = x.reshape(t, d)
    row = lambda v: v.reshape(1, -1)
    for i in range(depth):
        a = _norm(h, row(norm_pre_mix[i]))
        prest = _inproj(a, w_in[i], d_ssm, d_pool)
        ut = _inproj_ssm(a, w_in[i], d_ssm, seq_len)
        m, rt, et, wsc = _ssm_prep(ssm_lambda_re[i], ssm_lambda_im[i], ssm_log_step[i],
                                   ssm_b_re[i], ssm_b_im[i], ssm_c_re[i], ssm_c_im[i])
        yt = _ssm(ut, m, rt, et, wsc, ssm_d[i].reshape(n_groups, SSM_GROUP, 1),
                  seq_len // CHUNK)
        ys = _glu(yt, ssm_glu_w[i].astype(BF16), row(ssm_glu_b[i]), seq_len)
        po = _pool(prest, pool_w[i].astype(BF16), pool_b[i][:, None, :], row(pool_scale[i]),
                   seq_len, d_pool)
        h = _mix(ys, po, prest, h, w_branch_ssm[i].astype(BF16), w_branch_pool[i].astype(BF16),
                 w_out[i].astype(BF16), row(norm_post_mix[i]), d_pool)
        h = _ffn(h, row(norm_pre_ffn[i]), w_up[i].astype(BF16), ffn_conv_w[i], row(ffn_conv_b[i]),
                 w_down[i].astype(BF16), row(norm_post_ffn[i]), seq_len)
    return h.reshape(bsz, seq_len, d)
```

```python
import functools
import math

import jax
import jax.numpy as jnp
from jax import lax
from jax.experimental import pallas as pl
from jax.experimental.pallas import tpu as pltpu

F32 = jnp.float32
BF16 = jnp.bfloat16

EPS = 1e-6
MIN_NEG_REAL = -1e-4
SSM_GROUP = 16
SSM_STATE = 64
POOL_WINDOWS = (2, 4, 8, 16)
CONV_WIDTH = 3
CHUNK = 16
N_SCAN_PASSES = 7
LANES = 128

V7X_VMEM_BYTES = 64 * 1024 * 1024
VMEM_CEILING = V7X_VMEM_BYTES - 6 * 1024 * 1024


def _vmem_limit(pipelined_bytes, scratch_bytes, temp_bytes):
    return min(2 * pipelined_bytes + scratch_bytes + temp_bytes + (4 << 20), VMEM_CEILING)


def _params(semantics, vmem):
    return pltpu.CompilerParams(dimension_semantics=semantics, vmem_limit_bytes=vmem)


def _rms(xf, g):
    ms = jnp.mean(xf * xf, axis=-1, keepdims=True)
    return xf * lax.rsqrt(ms + EPS) * g


NORM_ROWS = 32


def _for_row_blocks(n_rows, body):
    def step(r, carry):
        body(pl.ds(pl.multiple_of(r * NORM_ROWS, NORM_ROWS), NORM_ROWS))
        return carry
    lax.fori_loop(0, n_rows // NORM_ROWS, step, 0)


def _sigmoid(x):
    return 1.0 / (1.0 + jnp.exp(-x))


def _gelu_tanh(x):
    c = math.sqrt(2.0 / math.pi)
    return 0.5 * x * (1.0 + jnp.tanh(c * (x + 0.044715 * (x * x * x))))


def _norm_kernel(x_ref, g_ref, o_ref):
    def rows(sl):
        o_ref[sl, :] = _rms(x_ref[sl, :], g_ref[...]).astype(BF16)
    _for_row_blocks(x_ref.shape[0], rows)


def _norm(x2, g, tm=512):
    t, d = x2.shape
    return pl.pallas_call(
        _norm_kernel,
        out_shape=jax.ShapeDtypeStruct((t, d), BF16),
        grid=(t // tm,),
        in_specs=[pl.BlockSpec((tm, d), lambda i: (i, 0)),
                  pl.BlockSpec((1, d), lambda i: (0, 0))],
        out_specs=pl.BlockSpec((tm, d), lambda i: (i, 0)),
        compiler_params=_params(("arbitrary",), _vmem_limit(tm * d * 6, 0, tm * d * 8)),
        name="norm",
    )(x2, g)


def _inproj_kernel(a_ref, w_ref, o_ref, *, n_plain, row_blocks):
    gate = pl.program_id(1) >= n_plain
    w = w_ref[...].astype(BF16)
    rows = a_ref.shape[0] // row_blocks
    for r0 in range(0, a_ref.shape[0], rows):
        p = jnp.dot(a_ref[r0:r0 + rows, :], w, preferred_element_type=F32)
        o_ref[r0:r0 + rows, :] = jnp.where(gate, _sigmoid(p), p).astype(BF16)


def _inproj(a, w, col0, n_plain_cols, tm=1024, tn=512, row_blocks=4):
    t, d = a.shape
    n = w.shape[1] - col0
    c0 = col0 // tn
    return pl.pallas_call(
        functools.partial(_inproj_kernel, n_plain=n_plain_cols // tn, row_blocks=row_blocks),
        out_shape=jax.ShapeDtypeStruct((t, n), BF16),
        grid=(t // tm, n // tn),
        in_specs=[pl.BlockSpec((tm, d), lambda i, j: (i, 0)),
                  pl.BlockSpec((d, tn), lambda i, j: (0, c0 + j))],
        out_specs=pl.BlockSpec((tm, tn), lambda i, j: (i, j)),
        compiler_params=_params(("arbitrary", "arbitrary"),
                                _vmem_limit(tm * d * 2 + d * tn * 4 + tm * tn * 2, 0,
                                            d * tn * 2 + tm * tn * 8)),
        name="inproj",
    )(a, w)


def _inproj_ssm_kernel(a_ref, w_ref, o_ref, p_sc, *, row_blocks):
    chunks = o_ref.shape[-1]
    n_slabs = p_sc.shape[0]
    w = w_ref[...].astype(BF16)
    rows = a_ref.shape[0] // row_blocks
    for r0 in range(0, a_ref.shape[0], rows):
        p = jnp.dot(a_ref[r0:r0 + rows, :], w, preferred_element_type=F32)
        for s in range(n_slabs):
            p_sc[s, r0:r0 + rows] = p[:, s * LANES:(s + 1) * LANES]
    gps = LANES // SSM_GROUP
    for q in range(CHUNK):
        for s in range(n_slabs):
            blk = p_sc[s, pl.ds(q, chunks, stride=CHUNK), :]
            o_ref[s * gps:(s + 1) * gps, q] = (
                blk.T.reshape(gps, SSM_GROUP, chunks).astype(BF16))


def _inproj_ssm(a, w, n, seq_len, tn=256, row_blocks=8):
    t, d = a.shape
    g = n // SSM_GROUP
    chunks = seq_len // CHUNK
    return pl.pallas_call(
        functools.partial(_inproj_ssm_kernel, row_blocks=row_blocks),
        out_shape=jax.ShapeDtypeStruct((g, CHUNK, SSM_GROUP, t // CHUNK), BF16),
        grid=(t // seq_len, n // tn),
        in_specs=[pl.BlockSpec((seq_len, d), lambda bi, j: (bi, 0)),
                  pl.BlockSpec((d, tn), lambda bi, j: (0, j))],
        out_specs=pl.BlockSpec((tn // SSM_GROUP, CHUNK, SSM_GROUP, chunks),
                               lambda bi, j: (j, 0, 0, bi)),
        scratch_shapes=[pltpu.VMEM((tn // LANES, seq_len, LANES), F32)],
        compiler_params=_params(("arbitrary", "arbitrary"),
                                _vmem_limit(seq_len * d * 2 + d * tn * 4 + tn * seq_len * 2,
                                            seq_len * tn * 4, d * tn * 2 + seq_len * tn * 16)),
        name="inproj_ssm",
    )(a, w)


def _ssm_prep_kernel(lre_ref, lim_ref, ls_ref, btr_ref, bti_ref, cr_ref, ci_ref,
                     m_ref, rt_ref, et_ref, w_ref, *, groups):
    p2 = 2 * SSM_STATE
    cq = CHUNK * SSM_GROUP
    lane = lax.broadcasted_iota(jnp.int32, (1, p2), 1)
    is_re = lane < SSM_STATE
    sgn = jnp.where(is_re, 1.0, -1.0).astype(F32)
    lane_blk = lax.broadcasted_iota(jnp.int32, (cq, cq), 1) // SSM_GROUP
    row8 = lax.broadcasted_iota(jnp.int32, (8, p2), 0)
    is_re8 = lax.broadcasted_iota(jnp.int32, (8, p2), 1) < SSM_STATE

    for gi in range(groups):
        lr = jnp.minimum(lre_ref[gi], MIN_NEG_REAL)
        li = lim_ref[gi]
        dt = jnp.exp(ls_ref[gi])
        mag = jnp.exp(lr * dt)
        ang = li * dt
        ar = mag * jnp.cos(ang)
        ai = mag * jnp.sin(ang)
        nr = ar - 1.0
        ni = ai
        den = lr * lr + li * li
        f_re = (nr * lr + ni * li) / den
        f_im = (ni * lr - nr * li) / den
        btr = btr_ref[gi]
        bti = bti_ref[gi]
        bbr = f_re * btr - f_im * bti
        bbi = f_re * bti + f_im * btr
        cr = cr_ref[gi]
        ci = ci_ref[gi]

        pa = [jnp.where(is_re, 1.0, 0.0).astype(F32)]
        pb = [jnp.where(is_re, 0.0, 1.0).astype(F32)]
        for _ in range(CHUNK):
            a_prev, b_prev = pa[-1], pb[-1]
            pa.append(a_prev * ar + b_prev * ai)
            pb.append(b_prev * ar - a_prev * ai)

        gk = [sgn * (cr * pa[k] + ci * pb[k]) for k in range(CHUNK + 1)]
        fs = jnp.concatenate(gk[:CHUNK], axis=0)
        et = jnp.concatenate(gk[1:], axis=0)
        rtt = jnp.concatenate(
            [bbr * pa[CHUNK - 1 - q] + bbi * pb[CHUNK - 1 - q] for q in range(CHUNK)], axis=0)
        bst = jnp.where(is_re, bbr, bbi)
        bst_t = jnp.concatenate([bst] * CHUNK, axis=0)
        kw = lax.dot_general(fs, bst_t, (((1,), (1,)), ((), ())),
                             precision=lax.Precision.HIGHEST, preferred_element_type=F32)
        m = jnp.zeros((cq, cq), F32)
        for q in range(CHUNK):
            if q == 0:
                shifted = kw
            else:
                shifted = jnp.concatenate(
                    [jnp.zeros((q * SSM_GROUP, cq), F32), kw[:cq - q * SSM_GROUP]], axis=0)
            m = jnp.where(lane_blk == q, shifted, m)
        m_ref[gi] = m.astype(BF16)
        rt_ref[gi] = rtt.T.astype(BF16)
        et_ref[gi] = et.astype(BF16)

        wa = jnp.broadcast_to(pa[CHUNK], (8, p2))
        rows = jnp.zeros((8, p2), F32)
        for k in range(N_SCAN_PASSES):
            rows = jnp.where(row8 == k, wa, rows)
            swapped = pltpu.roll(wa, SSM_STATE, 1)
            re2 = jnp.where(is_re8, wa, swapped)
            im2 = jnp.where(is_re8, swapped, wa)
            wb = jnp.where(is_re8, -swapped, swapped)
            wa = wa * re2 + wb * im2
        full = jnp.concatenate([rows, jnp.zeros((p2 - 8, p2), F32)], axis=0)
        w_ref[gi] = full.T


def _ssm_prep(lam_re, lam_im, log_step, b_re, b_im, c_re, c_im, groups=8):
    g, p = lam_re.shape
    p2 = 2 * p
    cq = CHUNK * SSM_GROUP
    dup = lambda v: jnp.concatenate([v, v], axis=-1)
    lre = dup(lam_re)[:, None, :]
    lim = dup(lam_im)[:, None, :]
    ls = jnp.broadcast_to(log_step[:, None, None], (g, 1, p2))
    btr = dup(jnp.swapaxes(b_re, 1, 2))
    bti = dup(jnp.swapaxes(b_im, 1, 2))
    cr = dup(c_re)
    ci = dup(c_im)
    row = pl.BlockSpec((groups, 1, p2), lambda i: (i, 0, 0))
    mat = pl.BlockSpec((groups, SSM_GROUP, p2), lambda i: (i, 0, 0))
    return pl.pallas_call(
        functools.partial(_ssm_prep_kernel, groups=groups),
        out_shape=(jax.ShapeDtypeStruct((g, cq, cq), BF16),
                   jax.ShapeDtypeStruct((g, p2, cq), BF16),
                   jax.ShapeDtypeStruct((g, cq, p2), BF16),
                   jax.ShapeDtypeStruct((g, p2, p2), F32)),
        grid=(g // groups,),
        in_specs=[row, row, row, mat, mat, mat, mat],
        out_specs=(pl.BlockSpec((groups, cq, cq), lambda i: (i, 0, 0)),
                   pl.BlockSpec((groups, p2, cq), lambda i: (i, 0, 0)),
                   pl.BlockSpec((groups, cq, p2), lambda i: (i, 0, 0)),
                   pl.BlockSpec((groups, p2, p2), lambda i: (i, 0, 0))),
        compiler_params=_params(("arbitrary",), _vmem_limit(groups * cq * cq * 4, 0, 8 << 20)),
        name="ssm_prep",
    )(lre, lim, ls, btr, bti, cr, ci)


def _ssm_kernel(u_ref, m_ref, rt_ref, et_ref, w_ref, d_ref, o_ref, *, groups, chunks_per_seq):
    cq = CHUNK * SSM_GROUP
    nc = u_ref.shape[-1]
    pos = lax.broadcasted_iota(jnp.int32, (SSM_STATE, nc), 1) % chunks_per_seq
    for gi in range(groups):
        x3 = u_ref[gi]
        x = x3.reshape(cq, nc)
        y = jnp.dot(m_ref[gi], x, preferred_element_type=F32)
        r = jnp.dot(rt_ref[gi], x, preferred_element_type=F32)
        s_re, s_im = r[:SSM_STATE], r[SSM_STATE:]
        for k in range(N_SCAN_PASSES):
            dist = 1 << k
            wr = w_ref[gi, 0:SSM_STATE, k:k + 1]
            wi = w_ref[gi, SSM_STATE:2 * SSM_STATE, k:k + 1]
            keep = pos >= dist
            sh_re = jnp.where(keep, pltpu.roll(s_re, dist, 1), 0.0)
            sh_im = jnp.where(keep, pltpu.roll(s_im, dist, 1), 0.0)
            s_re, s_im = (s_re + wr * sh_re - wi * sh_im,
                          s_im + wr * sh_im + wi * sh_re)
        first = pos >= 1
        sp = jnp.concatenate([jnp.where(first, pltpu.roll(s_re, 1, 1), 0.0),
                              jnp.where(first, pltpu.roll(s_im, 1, 1), 0.0)], axis=0)
        y = y + jnp.dot(et_ref[gi], sp.astype(BF16), preferred_element_type=F32)
        y3 = y.reshape(CHUNK, SSM_GROUP, nc) + d_ref[gi][None] * x3.astype(F32)
        o_ref[gi] = _gelu_tanh(y3).astype(BF16)


def _ssm(ut, m, rt, et, w, d, chunks_per_seq, groups=4):
    g, q, j, nc = ut.shape
    cq = q * j
    p2 = rt.shape[1]
    blk = lambda *s: pl.BlockSpec((groups,) + s, lambda i: (i,) + (0,) * len(s))
    return pl.pallas_call(
        functools.partial(_ssm_kernel, groups=groups, chunks_per_seq=chunks_per_seq),
        out_shape=jax.ShapeDtypeStruct(ut.shape, BF16),
        grid=(g // groups,),
        in_specs=[blk(q, j, nc), blk(cq, cq), blk(p2, cq), blk(cq, p2), blk(p2, p2), blk(j, 1)],
        out_specs=blk(q, j, nc),
        compiler_params=_params(("arbitrary",), _vmem_limit(groups * cq * nc * 8, 0, 8 << 20)),
        name="ssm",
    )(ut, m, rt, et, w, d)


def _glu_kernel(y_ref, w_ref, b_ref, o_ref, y_sc, *, tn):
    n = pl.program_id(1)
    chunks = y_ref.shape[-1]
    n_slabs = y_sc.shape[0]

    @pl.when(n == 0)
    def _():
        for q in range(CHUNK):
            yq = y_ref[:, q]
            yq = yq.reshape(yq.shape[0] * yq.shape[1], chunks).astype(F32).T
            for s in range(n_slabs):
                y_sc[s, pl.ds(q, chunks, stride=CHUNK), :] = yq[:, s * LANES:(s + 1) * LANES]

    y = jnp.concatenate([y_sc[s] for s in range(n_slabs)], axis=1).astype(BF16)
    z = jnp.dot(y, w_ref[...], preferred_element_type=F32) + b_ref[...]
    per_tile = tn // LANES
    yn = jnp.concatenate([y_sc[n * per_tile + s] for s in range(per_tile)], axis=1)
    o_ref[...] = (yn * _sigmoid(z)).astype(BF16)


def _glu(yt, w, b, seq_len, tn=512):
    g, q, j, nc = yt.shape
    d = g * j
    chunks = seq_len // q
    t = nc * q
    return pl.pallas_call(
        functools.partial(_glu_kernel, tn=tn),
        out_shape=jax.ShapeDtypeStruct((t, d), BF16),
        grid=(nc // chunks, d // tn),
        in_specs=[pl.BlockSpec((g, q, j, chunks), lambda bi, n: (0, 0, 0, bi)),
                  pl.BlockSpec((d, tn), lambda bi, n: (0, n)),
                  pl.BlockSpec((1, tn), lambda bi, n: (0, n))],
        out_specs=pl.BlockSpec((seq_len, tn), lambda bi, n: (bi, n)),
        scratch_shapes=[pltpu.VMEM((d // LANES, seq_len, LANES), F32)],
        compiler_params=_params(("arbitrary", "arbitrary"),
                                _vmem_limit((seq_len * d + d * tn + seq_len * tn) * 2,
                                            seq_len * d * 4, seq_len * d * 2 + seq_len * tn * 12)),
        name="glu",
    )(yt, w, b)


def _pool_kernel(u_ref, w_ref, b_ref, s_ref, o_ref, ext_sc, *, tiles_per_seq, halo):
    i = pl.program_id(0)
    tm = u_ref.shape[0]
    gp = w_ref.shape[1]

    @pl.when(i % tiles_per_seq == 0)
    def _():
        ext_sc[0:halo] = jnp.zeros((halo, ext_sc.shape[1]), BF16)

    @pl.when(i % tiles_per_seq != 0)
    def _():
        ext_sc[0:halo] = ext_sc[tm:tm + halo]

    ext_sc[halo:halo + tm] = u_ref[...]

    t_loc = lax.broadcasted_iota(jnp.int32, (tm, tm + halo), 0)
    s_loc = lax.broadcasted_iota(jnp.int32, (tm, tm + halo), 1)
    lag = t_loc + halo - s_loc
    t_seq = (i % tiles_per_seq) * tm + lax.broadcasted_iota(jnp.int32, (tm, 1), 0)
    for gi, win in enumerate(POOL_WINDOWS):
        band = jnp.where((lag >= 0) & (lag < win), 1.0, 0.0).astype(BF16)
        cols = slice(gi * gp, (gi + 1) * gp)
        wsum = jnp.dot(band, ext_sc[:, cols], preferred_element_type=F32)
        cnt = jnp.minimum(t_seq + 1, win).astype(F32)
        z = wsum / cnt - u_ref[:, cols].astype(F32)
        z = jnp.dot(z.astype(BF16), w_ref[gi], preferred_element_type=F32) + b_ref[gi]
        o_ref[:, cols] = (z * s_ref[:, cols]).astype(BF16)


def _pool(prest, w, b, scale, seq_len, d_pool, tm=512, halo=128):
    t = prest.shape[0]
    ng, gp, _ = w.shape
    return pl.pallas_call(
        functools.partial(_pool_kernel, tiles_per_seq=seq_len // tm, halo=halo),
        out_shape=jax.ShapeDtypeStruct((t, d_pool), BF16),
        grid=(t // tm,),
        in_specs=[pl.BlockSpec((tm, d_pool), lambda i: (i, 0)),
                  pl.BlockSpec((ng, gp, gp), lambda i: (0, 0, 0)),
                  pl.BlockSpec((ng, 1, gp), lambda i: (0, 0, 0)),
                  pl.BlockSpec((1, d_pool), lambda i: (0, 0))],
        out_specs=pl.BlockSpec((tm, d_pool), lambda i: (i, 0)),
        scratch_shapes=[pltpu.VMEM((tm + halo, d_pool), BF16)],
        compiler_params=_params(("arbitrary",),
                                _vmem_limit(tm * d_pool * 4 + ng * gp * gp * 2,
                                            (tm + halo) * d_pool * 2, 8 << 20)),
        name="pool",
    )(prest, w, b, scale)


def _mix_kernel(ys_ref, po_ref, gs_ref, gp_ref, x_ref, wbs_ref, wbp_ref, wo_ref, g_ref, o_ref,
                *, sub):
    n = pl.program_id(1)

    @pl.when(n == 0)
    def _():
        o_ref[...] = jnp.zeros_like(o_ref)

    subs = [slice(s * sub, (s + 1) * sub) for s in range(wo_ref.shape[0] // sub)]
    branch = [(jnp.dot(ys_ref[...], wbs_ref[:, cols], preferred_element_type=F32),
               jnp.dot(po_ref[...], wbp_ref[:, cols], preferred_element_type=F32))
              for cols in subs]
    merged = [(gs_ref[:, cols].astype(F32) * ms + gp_ref[:, cols].astype(F32) * mp).astype(BF16)
              for cols, (ms, mp) in zip(subs, branch)]
    o_ref[...] += jnp.dot(jnp.concatenate(merged, axis=1), wo_ref[...],
                          preferred_element_type=F32)

    @pl.when(n == pl.num_programs(1) - 1)
    def _():
        def rows(sl):
            o_ref[sl, :] = x_ref[sl, :] + _rms(o_ref[sl, :], g_ref[...])
        _for_row_blocks(o_ref.shape[0], rows)


def _mix(ys, po, prest, x2, wbs, wbp, wo, g, gate_col0, tm=512, tn=512, sub=256):
    t, d = x2.shape
    ds_ = ys.shape[1]
    dp = po.shape[1]
    c0 = gate_col0 // tn
    c1 = (gate_col0 + d) // tn
    resident = pl.Buffered(1)
    return pl.pallas_call(
        functools.partial(_mix_kernel, sub=sub),
        out_shape=jax.ShapeDtypeStruct((t, d), F32),
        grid=(t // tm, d // tn),
        in_specs=[pl.BlockSpec((tm, ds_), lambda i, n: (i, 0)),
                  pl.BlockSpec((tm, dp), lambda i, n: (i, 0)),
                  pl.BlockSpec((tm, tn), lambda i, n: (i, c0 + n)),
                  pl.BlockSpec((tm, tn), lambda i, n: (i, c1 + n)),
                  pl.BlockSpec((tm, d), lambda i, n: (i, 0), pipeline_mode=resident),
                  pl.BlockSpec((ds_, tn), lambda i, n: (0, n)),
                  pl.BlockSpec((dp, tn), lambda i, n: (0, n)),
                  pl.BlockSpec((tn, d), lambda i, n: (n, 0)),
                  pl.BlockSpec((1, d), lambda i, n: (0, 0))],
        out_specs=pl.BlockSpec((tm, d), lambda i, n: (i, 0), pipeline_mode=resident),
        compiler_params=_params(
            ("arbitrary", "arbitrary"),
            _vmem_limit(tm * (ds_ + dp) * 2 + tm * tn * 4 + (ds_ + dp) * tn * 2 + tn * d * 2,
                        tm * d * 8, tm * tn * 16)),
        name="mix",
    )(ys, po, prest, prest, x2, wbs, wbp, wo, g)


def _ffn_kernel(*refs, n_sub, n_tiles, tiles_per_seq, keep, up_row_blocks):
    h_ref, g3_ref = refs[:2]
    groups = [refs[2 + k * n_sub:2 + (k + 1) * n_sub] for k in range(7)]
    wa_refs, wb_refs, cwa_refs, cwb_refs, cba_refs, cbb_refs, wd_refs = groups
    g4_ref, o_ref, c_sc, carry_sc = refs[2 + 7 * n_sub:]
    i = pl.program_id(0)
    f = pl.program_id(1)
    tm = h_ref.shape[0]

    @pl.when(f == 0)
    def _():
        def rows(sl):
            c_sc[sl, :] = _rms(h_ref[sl, :], g3_ref[...]).astype(BF16)
            o_ref[sl, :] = jnp.zeros((NORM_ROWS, o_ref.shape[1]), F32)
        _for_row_blocks(tm, rows)

    seq_start = i % tiles_per_seq == 0

    rb = tm // up_row_blocks

    def up_product(w_ref):
        w = w_ref[...]
        return jnp.concatenate(
            [jnp.dot(c_sc[r0:r0 + rb, :], w, preferred_element_type=F32)
             for r0 in range(0, tm, rb)], axis=0)

    row_in_head = lax.broadcasted_iota(jnp.int32, (keep, wd_refs[0].shape[0]), 0)

    def conv(up, half, tile, cw_ref, cb_ref):
        prev = jnp.where(seq_start, 0.0, carry_sc[half, tile])
        carry_sc[half, tile] = up[tm - keep:]
        cw = cw_ref[...]
        z = cw[0:1] * up
        zp = cw[0:1] * prev
        for k in range(1, CONV_WIDTH):
            r = pltpu.roll(z, 1, 0)
            head = jnp.where(row_in_head == 0, zp[keep - 1:keep], r[0:keep])
            z = cw[k:k + 1] * up + jnp.concatenate([head, r[keep:]], axis=0)
            if k + 1 < CONV_WIDTH:
                zp = cw[k:k + 1] * prev + pltpu.roll(zp, 1, 0)
        return z + cb_ref[...]

    ups = [(up_product(wa_refs[s]), up_product(wb_refs[s])) for s in range(n_sub)]
    fms = []
    for s in range(n_sub):
        tile = f * n_sub + s
        ua = conv(ups[s][0], 0, tile, cwa_refs[s], cba_refs[s])
        ub = conv(ups[s][1], 1, tile, cwb_refs[s], cbb_refs[s])
        fms.append(jnp.where(tile < n_tiles, _gelu_tanh(ua) * ub, 0.0).astype(BF16))
    fm = jnp.concatenate(fms, axis=1)
    wd = jnp.concatenate([wd_refs[s][...] for s in range(n_sub)], axis=0)
    o_ref[...] += jnp.dot(fm, wd, preferred_element_type=F32)

    @pl.when(f == pl.num_programs(1) - 1)
    def _():
        def rows(sl):
            o_ref[sl, :] = h_ref[sl, :] + _rms(o_ref[sl, :], g4_ref[...])
        _for_row_blocks(tm, rows)


def _ffn(h1, g3, w_up, conv_w, conv_b, w_down, g4, seq_len, tm=512, sub=256, n_sub=2, keep=8,
         up_row_blocks=2):
    t, d = h1.shape
    d_ff = w_down.shape[0]
    n_tiles = d_ff // sub
    nf = pl.cdiv(n_tiles, n_sub)
    resident = pl.Buffered(1)

    def tile_of(s):
        return lambda f: jnp.minimum(f * n_sub + s, n_tiles - 1)

    def per_sub(shape, index):
        return [pl.BlockSpec(shape, functools.partial(index, tile_of(s))) for s in range(n_sub)]

    col_a = lambda tile, i, f: (0, tile(f))
    col_b = lambda tile, i, f: (0, n_tiles + tile(f))
    row_d = lambda tile, i, f: (tile(f), 0)
    in_specs = ([pl.BlockSpec((tm, d), lambda i, f: (i, 0), pipeline_mode=resident),
                 pl.BlockSpec((1, d), lambda i, f: (0, 0))]
                + per_sub((d, sub), col_a) + per_sub((d, sub), col_b)
                + per_sub((CONV_WIDTH, sub), col_a) + per_sub((CONV_WIDTH, sub), col_b)
                + per_sub((1, sub), col_a) + per_sub((1, sub), col_b)
                + per_sub((sub, d), row_d)
                + [pl.BlockSpec((1, d), lambda i, f: (0, 0))])
    operands = ([h1, g3] + [w_up] * (2 * n_sub) + [conv_w] * (2 * n_sub)
                + [conv_b] * (2 * n_sub) + [w_down] * n_sub + [g4])
    tf = sub * n_sub
    return pl.pallas_call(
        functools.partial(_ffn_kernel, n_sub=n_sub, n_tiles=n_tiles,
                          tiles_per_seq=seq_len // tm, keep=keep, up_row_blocks=up_row_blocks),
        out_shape=jax.ShapeDtypeStruct((t, d), F32),
        grid=(t // tm, nf),
        in_specs=in_specs,
        out_specs=pl.BlockSpec((tm, d), lambda i, f: (i, 0), pipeline_mode=resident),
        scratch_shapes=[pltpu.VMEM((tm, d), BF16),
                        pltpu.VMEM((2, nf * n_sub, keep, sub), F32)],
        compiler_params=_params(
            ("arbitrary", "arbitrary"),
            _vmem_limit(d * tf * 4 + tf * d * 2,
                        tm * d * 8 + tm * d * 2 + 2 * nf * tf * keep * 4, tm * tf * 32)),
        name="ffn",
    )(*operands)


def kernel(x, norm_pre_mix, w_in, ssm_lambda_re, ssm_lambda_im, ssm_log_step, ssm_b_re, ssm_b_im,
           ssm_c_re, ssm_c_im, ssm_d, ssm_glu_w, ssm_glu_b, pool_w, pool_b, pool_scale,
           w_branch_ssm, w_branch_pool, w_out, norm_post_mix, norm_pre_ffn, w_up, ffn_conv_w,
           ffn_conv_b, w_down, norm_post_ffn):
    bsz, seq_len, d = x.shape
    depth = w_in.shape[0]
    d_ssm = ssm_d.shape[1]
    d_pool = pool_scale.shape[1]
    n_groups = d_ssm // SSM_GROUP
    assert seq_len // CHUNK == 1 << N_SCAN_PASSES
    t = bsz * seq_len
    h = x.reshape(t, d)
    row = lambda v: v.reshape(1, -1)
    for i in range(depth):
        a = _norm(h, row(norm_pre_mix[i]))
        prest = _inproj(a, w_in[i], d_ssm, d_pool)
        ut = _inproj_ssm(a, w_in[i], d_ssm, seq_len)
        m, rt, et, wsc = _ssm_prep(ssm_lambda_re[i], ssm_lambda_im[i], ssm_log_step[i],
                                   ssm_b_re[i], ssm_b_im[i], ssm_c_re[i], ssm_c_im[i])
        yt = _ssm(ut, m, rt, et, wsc, ssm_d[i].reshape(n_groups, SSM_GROUP, 1),
                  seq_len // CHUNK)
        ys = _glu(yt, ssm_glu_w[i].astype(BF16), row(ssm_glu_b[i]), seq_len)
        po = _pool(prest, pool_w[i].astype(BF16), pool_b[i][:, None, :], row(pool_scale[i]),
                   seq_len, d_pool)
        h = _mix(ys, po, prest, h, w_branch_ssm[i].astype(BF16), w_branch_pool[i].astype(BF16),
                 w_out[i].astype(BF16), row(norm_post_mix[i]), d_pool)
        h = _ffn(h, row(norm_pre_ffn[i]), w_up[i].astype(BF16), ffn_conv_w[i], row(ffn_conv_b[i]),
                 w_down[i].astype(BF16), row(norm_post_ffn[i]), seq_len)
    return h.reshape(bsz, seq_len, d)
```

```python
import functools
import math

import jax
import jax.numpy as jnp
from jax import lax
from jax.experimental import pallas as pl
from jax.experimental.pallas import tpu as pltpu

F32 = jnp.float32
BF16 = jnp.bfloat16

EPS = 1e-6
MIN_NEG_REAL = -1e-4
SSM_GROUP = 16
SSM_STATE = 64
POOL_WINDOWS = (2, 4, 8, 16)
CONV_WIDTH = 3
CHUNK = 16
N_SCAN_PASSES = 7
LANES = 128

V7X_VMEM_BYTES = 64 * 1024 * 1024
VMEM_CEILING = V7X_VMEM_BYTES - 6 * 1024 * 1024


def _vmem_limit(pipelined_bytes, scratch_bytes, temp_bytes):
    return min(2 * pipelined_bytes + scratch_bytes + temp_bytes + (4 << 20), VMEM_CEILING)


def _params(semantics, vmem):
    return pltpu.CompilerParams(dimension_semantics=semantics, vmem_limit_bytes=vmem)


def _rms(xf, g):
    ms = jnp.mean(xf * xf, axis=-1, keepdims=True)
    return xf * lax.rsqrt(ms + EPS) * g


NORM_ROWS = 32


def _for_row_blocks(n_rows, body):
    def step(r, carry):
        body(pl.ds(pl.multiple_of(r * NORM_ROWS, NORM_ROWS), NORM_ROWS))
        return carry
    lax.fori_loop(0, n_rows // NORM_ROWS, step, 0)


def _sigmoid(x):
    return 1.0 / (1.0 + jnp.exp(-x))


def _gelu_tanh(x):
    c = math.sqrt(2.0 / math.pi)
    return 0.5 * x * (1.0 + jnp.tanh(c * (x + 0.044715 * (x * x * x))))


def _norm_kernel(x_ref, g_ref, o_ref):
    def rows(sl):
        o_ref[sl, :] = _rms(x_ref[sl, :], g_ref[...]).astype(BF16)
    _for_row_blocks(x_ref.shape[0], rows)


def _norm(x2, g, tm=512):
    t, d = x2.shape
    return pl.pallas_call(
        _norm_kernel,
        out_shape=jax.ShapeDtypeStruct((t, d), BF16),
        grid=(t // tm,),
        in_specs=[pl.BlockSpec((tm, d), lambda i: (i, 0)),
                  pl.BlockSpec((1, d), lambda i: (0, 0))],
        out_specs=pl.BlockSpec((tm, d), lambda i: (i, 0)),
        compiler_params=_params(("arbitrary",), _vmem_limit(tm * d * 6, 0, tm * d * 8)),
        name="norm",
    )(x2, g)


def _side_cast_specs(src, rows, grid):
    n_blocks = src.shape[0] // rows
    assert src.shape[0] % rows == 0 and n_blocks <= math.prod(grid)

    def index(*g):
        step = g[0]
        for k in range(1, len(grid)):
            step = step * grid[k] + g[k]
        return (jnp.minimum(step, n_blocks - 1), 0)

    spec = pl.BlockSpec((rows, src.shape[1]), index)
    return spec, spec, jax.ShapeDtypeStruct(src.shape, BF16)


def _inproj_kernel(a_ref, w_ref, cast_ref, o_ref, cast_o_ref, *, n_plain, row_blocks):
    gate = pl.program_id(1) >= n_plain
    w = w_ref[...].astype(BF16)
    rows = a_ref.shape[0] // row_blocks
    for r0 in range(0, a_ref.shape[0], rows):
        p = jnp.dot(a_ref[r0:r0 + rows, :], w, preferred_element_type=F32)
        o_ref[r0:r0 + rows, :] = jnp.where(gate, _sigmoid(p), p).astype(BF16)
    cast_o_ref[...] = cast_ref[...].astype(BF16)


def _inproj(a, w, col0, n_plain_cols, cast_src, cast_rows, tm=1024, tn=512, row_blocks=4):
    t, d = a.shape
    n = w.shape[1] - col0
    c0 = col0 // tn
    grid = (t // tm, n // tn)
    cast_in, cast_out, cast_shape = _side_cast_specs(cast_src, cast_rows, grid)
    cast_bytes = cast_rows * cast_src.shape[1] * 6
    return pl.pallas_call(
        functools.partial(_inproj_kernel, n_plain=n_plain_cols // tn, row_blocks=row_blocks),
        out_shape=(jax.ShapeDtypeStruct((t, n), BF16), cast_shape),
        grid=grid,
        in_specs=[pl.BlockSpec((tm, d), lambda i, j: (i, 0)),
                  pl.BlockSpec((d, tn), lambda i, j: (0, c0 + j)),
                  cast_in],
        out_specs=(pl.BlockSpec((tm, tn), lambda i, j: (i, j)), cast_out),
        compiler_params=_params(("arbitrary", "arbitrary"),
                                _vmem_limit(tm * d * 2 + d * tn * 4 + tm * tn * 2 + cast_bytes, 0,
                                            d * tn * 2 + tm * tn * 8)),
        name="inproj",
    )(a, w, cast_src)


def _inproj_ssm_kernel(a_ref, w_ref, o_ref, p_sc, *, row_blocks):
    chunks = o_ref.shape[-1]
    n_slabs = p_sc.shape[0]
    w = w_ref[...].astype(BF16)
    rows = a_ref.shape[0] // row_blocks
    for r0 in range(0, a_ref.shape[0], rows):
        p = jnp.dot(a_ref[r0:r0 + rows, :], w, preferred_element_type=F32)
        for s in range(n_slabs):
            p_sc[s, r0:r0 + rows] = p[:, s * LANES:(s + 1) * LANES]
    gps = LANES // SSM_GROUP
    for q in range(CHUNK):
        for s in range(n_slabs):
            blk = p_sc[s, pl.ds(q, chunks, stride=CHUNK), :]
            o_ref[s * gps:(s + 1) * gps, q] = (
                blk.T.reshape(gps, SSM_GROUP, chunks).astype(BF16))


def _inproj_ssm(a, w, n, seq_len, tn=256, row_blocks=8):
    t, d = a.shape
    g = n // SSM_GROUP
    chunks = seq_len // CHUNK
    return pl.pallas_call(
        functools.partial(_inproj_ssm_kernel, row_blocks=row_blocks),
        out_shape=jax.ShapeDtypeStruct((g, CHUNK, SSM_GROUP, t // CHUNK), BF16),
        grid=(t // seq_len, n // tn),
        in_specs=[pl.BlockSpec((seq_len, d), lambda bi, j: (bi, 0)),
                  pl.BlockSpec((d, tn), lambda bi, j: (0, j))],
        out_specs=pl.BlockSpec((tn // SSM_GROUP, CHUNK, SSM_GROUP, chunks),
                               lambda bi, j: (j, 0, 0, bi)),
        scratch_shapes=[pltpu.VMEM((tn // LANES, seq_len, LANES), F32)],
        compiler_params=_params(("arbitrary", "arbitrary"),
                                _vmem_limit(seq_len * d * 2 + d * tn * 4 + tn * seq_len * 2,
                                            seq_len * tn * 4, d * tn * 2 + seq_len * tn * 16)),
        name="inproj_ssm",
    )(a, w)


def _ssm_prep_kernel(lre_ref, lim_ref, ls_ref, btr_ref, bti_ref, cr_ref, ci_ref,
                     m_ref, rt_ref, et_ref, w_ref, *, groups):
    p2 = 2 * SSM_STATE
    cq = CHUNK * SSM_GROUP
    lane = lax.broadcasted_iota(jnp.int32, (1, p2), 1)
    is_re = lane < SSM_STATE
    sgn = jnp.where(is_re, 1.0, -1.0).astype(F32)
    lane_blk = lax.broadcasted_iota(jnp.int32, (cq, cq), 1) // SSM_GROUP
    row8 = lax.broadcasted_iota(jnp.int32, (8, p2), 0)
    is_re8 = lax.broadcasted_iota(jnp.int32, (8, p2), 1) < SSM_STATE

    for gi in range(groups):
        lr = jnp.minimum(lre_ref[gi], MIN_NEG_REAL)
        li = lim_ref[gi]
        dt = jnp.exp(ls_ref[gi])
        mag = jnp.exp(lr * dt)
        ang = li * dt
        ar = mag * jnp.cos(ang)
        ai = mag * jnp.sin(ang)
        nr = ar - 1.0
        ni = ai
        den = lr * lr + li * li
        f_re = (nr * lr + ni * li) / den
        f_im = (ni * lr - nr * li) / den
        btr = btr_ref[gi]
        bti = bti_ref[gi]
        bbr = f_re * btr - f_im * bti
        bbi = f_re * bti + f_im * btr
        cr = cr_ref[gi]
        ci = ci_ref[gi]

        pa = [jnp.where(is_re, 1.0, 0.0).astype(F32)]
        pb = [jnp.where(is_re, 0.0, 1.0).astype(F32)]
        for _ in range(CHUNK):
            a_prev, b_prev = pa[-1], pb[-1]
            pa.append(a_prev * ar + b_prev * ai)
            pb.append(b_prev * ar - a_prev * ai)

        gk = [sgn * (cr * pa[k] + ci * pb[k]) for k in range(CHUNK + 1)]
        fs = jnp.concatenate(gk[:CHUNK], axis=0)
        et = jnp.concatenate(gk[1:], axis=0)
        rtt = jnp.concatenate(
            [bbr * pa[CHUNK - 1 - q] + bbi * pb[CHUNK - 1 - q] for q in range(CHUNK)], axis=0)
        bst = jnp.where(is_re, bbr, bbi)
        bst_t = jnp.concatenate([bst] * CHUNK, axis=0)
        kw = lax.dot_general(fs, bst_t, (((1,), (1,)), ((), ())),
                             precision=lax.Precision.HIGHEST, preferred_element_type=F32)
        m = jnp.zeros((cq, cq), F32)
        for q in range(CHUNK):
            if q == 0:
                shifted = kw
            else:
                shifted = jnp.concatenate(
                    [jnp.zeros((q * SSM_GROUP, cq), F32), kw[:cq - q * SSM_GROUP]], axis=0)
            m = jnp.where(lane_blk == q, shifted, m)
        m_ref[gi] = m.astype(BF16)
        rt_ref[gi] = rtt.T.astype(BF16)
        et_ref[gi] = et.astype(BF16)

        wa = jnp.broadcast_to(pa[CHUNK], (8, p2))
        rows = jnp.zeros((8, p2), F32)
        for k in range(N_SCAN_PASSES):
            rows = jnp.where(row8 == k, wa, rows)
            swapped = pltpu.roll(wa, SSM_STATE, 1)
            re2 = jnp.where(is_re8, wa, swapped)
            im2 = jnp.where(is_re8, swapped, wa)
            wb = jnp.where(is_re8, -swapped, swapped)
            wa = wa * re2 + wb * im2
        full = jnp.concatenate([rows, jnp.zeros((p2 - 8, p2), F32)], axis=0)
        w_ref[gi] = full.T


def _ssm_prep(lam_re, lam_im, log_step, b_re, b_im, c_re, c_im, groups=8):
    g, p = lam_re.shape
    p2 = 2 * p
    cq = CHUNK * SSM_GROUP
    dup = lambda v: jnp.concatenate([v, v], axis=-1)
    lre = dup(lam_re)[:, None, :]
    lim = dup(lam_im)[:, None, :]
    ls = jnp.broadcast_to(log_step[:, None, None], (g, 1, p2))
    btr = dup(jnp.swapaxes(b_re, 1, 2))
    bti = dup(jnp.swapaxes(b_im, 1, 2))
    cr = dup(c_re)
    ci = dup(c_im)
    row = pl.BlockSpec((groups, 1, p2), lambda i: (i, 0, 0))
    mat = pl.BlockSpec((groups, SSM_GROUP, p2), lambda i: (i, 0, 0))
    return pl.pallas_call(
        functools.partial(_ssm_prep_kernel, groups=groups),
        out_shape=(jax.ShapeDtypeStruct((g, cq, cq), BF16),
                   jax.ShapeDtypeStruct((g, p2, cq), BF16),
                   jax.ShapeDtypeStruct((g, cq, p2), BF16),
                   jax.ShapeDtypeStruct((g, p2, p2), F32)),
        grid=(g // groups,),
        in_specs=[row, row, row, mat, mat, mat, mat],
        out_specs=(pl.BlockSpec((groups, cq, cq), lambda i: (i, 0, 0)),
                   pl.BlockSpec((groups, p2, cq), lambda i: (i, 0, 0)),
                   pl.BlockSpec((groups, cq, p2), lambda i: (i, 0, 0)),
                   pl.BlockSpec((groups, p2, p2), lambda i: (i, 0, 0))),
        compiler_params=_params(("arbitrary",), _vmem_limit(groups * cq * cq * 4, 0, 8 << 20)),
        name="ssm_prep",
    )(lre, lim, ls, btr, bti, cr, ci)


def _ssm_kernel(u_ref, m_ref, rt_ref, et_ref, w_ref, d_ref, o_ref, *, groups, chunks_per_seq):
    cq = CHUNK * SSM_GROUP
    nc = u_ref.shape[-1]
    pos = lax.broadcasted_iota(jnp.int32, (SSM_STATE, nc), 1) % chunks_per_seq
    for gi in range(groups):
        x3 = u_ref[gi]
        x = x3.reshape(cq, nc)
        y = jnp.dot(m_ref[gi], x, preferred_element_type=F32)
        r = jnp.dot(rt_ref[gi], x, preferred_element_type=F32)
        s_re, s_im = r[:SSM_STATE], r[SSM_STATE:]
        for k in range(N_SCAN_PASSES):
            dist = 1 << k
            wr = w_ref[gi, 0:SSM_STATE, k:k + 1]
            wi = w_ref[gi, SSM_STATE:2 * SSM_STATE, k:k + 1]
            keep = pos >= dist
            sh_re = jnp.where(keep, pltpu.roll(s_re, dist, 1), 0.0)
            sh_im = jnp.where(keep, pltpu.roll(s_im, dist, 1), 0.0)
            s_re, s_im = (s_re + wr * sh_re - wi * sh_im,
                          s_im + wr * sh_im + wi * sh_re)
        first = pos >= 1
        sp = jnp.concatenate([jnp.where(first, pltpu.roll(s_re, 1, 1), 0.0),
                              jnp.where(first, pltpu.roll(s_im, 1, 1), 0.0)], axis=0)
        y = y + jnp.dot(et_ref[gi], sp.astype(BF16), preferred_element_type=F32)
        y3 = y.reshape(CHUNK, SSM_GROUP, nc) + d_ref[gi][None] * x3.astype(F32)
        o_ref[gi] = _gelu_tanh(y3).astype(BF16)


def _ssm(ut, m, rt, et, w, d, chunks_per_seq, groups=4):
    g, q, j, nc = ut.shape
    cq = q * j
    p2 = rt.shape[1]
    blk = lambda *s: pl.BlockSpec((groups,) + s, lambda i: (i,) + (0,) * len(s))
    return pl.pallas_call(
        functools.partial(_ssm_kernel, groups=groups, chunks_per_seq=chunks_per_seq),
        out_shape=jax.ShapeDtypeStruct(ut.shape, BF16),
        grid=(g // groups,),
        in_specs=[blk(q, j, nc), blk(cq, cq), blk(p2, cq), blk(cq, p2), blk(p2, p2), blk(j, 1)],
        out_specs=blk(q, j, nc),
        compiler_params=_params(("arbitrary",), _vmem_limit(groups * cq * nc * 8, 0, 8 << 20)),
        name="ssm",
    )(ut, m, rt, et, w, d)


def _glu_kernel(y_ref, w_ref, b_ref, o_ref, y_sc, *, tn):
    n = pl.program_id(1)
    chunks = y_ref.shape[-1]
    n_slabs = y_sc.shape[0]

    @pl.when(n == 0)
    def _():
        for q in range(CHUNK):
            yq = y_ref[:, q]
            yq = yq.reshape(yq.shape[0] * yq.shape[1], chunks).astype(F32).T
            for s in range(n_slabs):
                y_sc[s, pl.ds(q, chunks, stride=CHUNK), :] = yq[:, s * LANES:(s + 1) * LANES]

    y = jnp.concatenate([y_sc[s] for s in range(n_slabs)], axis=1).astype(BF16)
    z = jnp.dot(y, w_ref[...], preferred_element_type=F32) + b_ref[...]
    per_tile = tn // LANES
    yn = jnp.concatenate([y_sc[n * per_tile + s] for s in range(per_tile)], axis=1)
    o_ref[...] = (yn * _sigmoid(z)).astype(BF16)


def _glu(yt, w, b, seq_len, tn=512):
    g, q, j, nc = yt.shape
    d = g * j
    chunks = seq_len // q
    t = nc * q
    return pl.pallas_call(
        functools.partial(_glu_kernel, tn=tn),
        out_shape=jax.ShapeDtypeStruct((t, d), BF16),
        grid=(nc // chunks, d // tn),
        in_specs=[pl.BlockSpec((g, q, j, chunks), lambda bi, n: (0, 0, 0, bi)),
                  pl.BlockSpec((d, tn), lambda bi, n: (0, n)),
                  pl.BlockSpec((1, tn), lambda bi, n: (0, n))],
        out_specs=pl.BlockSpec((seq_len, tn), lambda bi, n: (bi, n)),
        scratch_shapes=[pltpu.VMEM((d // LANES, seq_len, LANES), F32)],
        compiler_params=_params(("arbitrary", "arbitrary"),
                                _vmem_limit((seq_len * d + d * tn + seq_len * tn) * 2,
                                            seq_len * d * 4, seq_len * d * 2 + seq_len * tn * 12)),
        name="glu",
    )(yt, w, b)


def _pool_kernel(u_ref, w_ref, b_ref, s_ref, o_ref, ext_sc, *, tiles_per_seq, halo):
    i = pl.program_id(0)
    tm = u_ref.shape[0]
    gp = w_ref.shape[1]

    @pl.when(i % tiles_per_seq == 0)
    def _():
        ext_sc[0:halo] = jnp.zeros((halo, ext_sc.shape[1]), BF16)

    @pl.when(i % tiles_per_seq != 0)
    def _():
        ext_sc[0:halo] = ext_sc[tm:tm + halo]

    ext_sc[halo:halo + tm] = u_ref[...]

    t_loc = lax.broadcasted_iota(jnp.int32, (tm, tm + halo), 0)
    s_loc = lax.broadcasted_iota(jnp.int32, (tm, tm + halo), 1)
    lag = t_loc + halo - s_loc
    t_seq = (i % tiles_per_seq) * tm + lax.broadcasted_iota(jnp.int32, (tm, 1), 0)
    for gi, win in enumerate(POOL_WINDOWS):
        band = jnp.where((lag >= 0) & (lag < win), 1.0, 0.0).astype(BF16)
        cols = slice(gi * gp, (gi + 1) * gp)
        wsum = jnp.dot(band, ext_sc[:, cols], preferred_element_type=F32)
        cnt = jnp.minimum(t_seq + 1, win).astype(F32)
        z = wsum / cnt - u_ref[:, cols].astype(F32)
        z = jnp.dot(z.astype(BF16), w_ref[gi], preferred_element_type=F32) + b_ref[gi]
        o_ref[:, cols] = (z * s_ref[:, cols]).astype(BF16)


def _pool(prest, w, b, scale, seq_len, d_pool, tm=512, halo=128):
    t = prest.shape[0]
    ng, gp, _ = w.shape
    return pl.pallas_call(
        functools.partial(_pool_kernel, tiles_per_seq=seq_len // tm, halo=halo),
        out_shape=jax.ShapeDtypeStruct((t, d_pool), BF16),
        grid=(t // tm,),
        in_specs=[pl.BlockSpec((tm, d_pool), lambda i: (i, 0)),
                  pl.BlockSpec((ng, gp, gp), lambda i: (0, 0, 0)),
                  pl.BlockSpec((ng, 1, gp), lambda i: (0, 0, 0)),
                  pl.BlockSpec((1, d_pool), lambda i: (0, 0))],
        out_specs=pl.BlockSpec((tm, d_pool), lambda i: (i, 0)),
        scratch_shapes=[pltpu.VMEM((tm + halo, d_pool), BF16)],
        compiler_params=_params(("arbitrary",),
                                _vmem_limit(tm * d_pool * 4 + ng * gp * gp * 2,
                                            (tm + halo) * d_pool * 2, 8 << 20)),
        name="pool",
    )(prest, w, b, scale)


def _mix_kernel(ys_ref, po_ref, gs_ref, gp_ref, x_ref, wbs_ref, wbp_ref, wo_ref, g_ref, cast_ref,
                o_ref, cast_o_ref, *, sub):
    n = pl.program_id(1)
    cast_o_ref[...] = cast_ref[...].astype(BF16)

    @pl.when(n == 0)
    def _():
        o_ref[...] = jnp.zeros_like(o_ref)

    subs = [slice(s * sub, (s + 1) * sub) for s in range(wo_ref.shape[0] // sub)]
    branch = [(jnp.dot(ys_ref[...], wbs_ref[:, cols], preferred_element_type=F32),
               jnp.dot(po_ref[...], wbp_ref[:, cols], preferred_element_type=F32))
              for cols in subs]
    merged = [(gs_ref[:, cols].astype(F32) * ms + gp_ref[:, cols].astype(F32) * mp).astype(BF16)
              for cols, (ms, mp) in zip(subs, branch)]
    o_ref[...] += jnp.dot(jnp.concatenate(merged, axis=1), wo_ref[...],
                          preferred_element_type=F32)

    @pl.when(n == pl.num_programs(1) - 1)
    def _():
        def rows(sl):
            o_ref[sl, :] = x_ref[sl, :] + _rms(o_ref[sl, :], g_ref[...])
        _for_row_blocks(o_ref.shape[0], rows)


def _mix(ys, po, prest, x2, wbs, wbp, wo, g, gate_col0, cast_src, cast_rows,
         tm=512, tn=512, sub=256):
    t, d = x2.shape
    ds_ = ys.shape[1]
    dp = po.shape[1]
    c0 = gate_col0 // tn
    c1 = (gate_col0 + d) // tn
    resident = pl.Buffered(1)
    grid = (t // tm, d // tn)
    cast_in, cast_out, cast_shape = _side_cast_specs(cast_src, cast_rows, grid)
    cast_bytes = cast_rows * cast_src.shape[1] * 6
    return pl.pallas_call(
        functools.partial(_mix_kernel, sub=sub),
        out_shape=(jax.ShapeDtypeStruct((t, d), F32), cast_shape),
        grid=grid,
        in_specs=[pl.BlockSpec((tm, ds_), lambda i, n: (i, 0)),
                  pl.BlockSpec((tm, dp), lambda i, n: (i, 0)),
                  pl.BlockSpec((tm, tn), lambda i, n: (i, c0 + n)),
                  pl.BlockSpec((tm, tn), lambda i, n: (i, c1 + n)),
                  pl.BlockSpec((tm, d), lambda i, n: (i, 0), pipeline_mode=resident),
                  pl.BlockSpec((ds_, tn), lambda i, n: (0, n)),
                  pl.BlockSpec((dp, tn), lambda i, n: (0, n)),
                  pl.BlockSpec((tn, d), lambda i, n: (n, 0)),
                  pl.BlockSpec((1, d), lambda i, n: (0, 0)),
                  cast_in],
        out_specs=(pl.BlockSpec((tm, d), lambda i, n: (i, 0), pipeline_mode=resident), cast_out),
        compiler_params=_params(
            ("arbitrary", "arbitrary"),
            _vmem_limit(tm * (ds_ + dp) * 2 + tm * tn * 4 + (ds_ + dp) * tn * 2 + tn * d * 2
                        + cast_bytes, tm * d * 8, tm * tn * 16)),
        name="mix",
    )(ys, po, prest, prest, x2, wbs, wbp, wo, g, cast_src)


def _ffn_kernel(*refs, n_sub, n_tiles, tiles_per_seq, keep, up_row_blocks):
    h_ref, g3_ref = refs[:2]
    groups = [refs[2 + k * n_sub:2 + (k + 1) * n_sub] for k in range(7)]
    wa_refs, wb_refs, cwa_refs, cwb_refs, cba_refs, cbb_refs, wd_refs = groups
    g4_ref, o_ref, c_sc, carry_sc = refs[2 + 7 * n_sub:]
    i = pl.program_id(0)
    f = pl.program_id(1)
    tm = h_ref.shape[0]

    @pl.when(f == 0)
    def _():
        def rows(sl):
            c_sc[sl, :] = _rms(h_ref[sl, :], g3_ref[...]).astype(BF16)
            o_ref[sl, :] = jnp.zeros((NORM_ROWS, o_ref.shape[1]), F32)
        _for_row_blocks(tm, rows)

    seq_start = i % tiles_per_seq == 0

    rb = tm // up_row_blocks
    row_in_head = lax.broadcasted_iota(jnp.int32, (keep, wd_refs[0].shape[0]), 0)

    def up_blocks(w_ref):
        w = w_ref[...]
        return [jnp.dot(c_sc[r0:r0 + rb, :], w, preferred_element_type=F32)
                for r0 in range(0, tm, rb)]

    def conv(blocks, half, tile, cw_ref, cb_ref):
        prev = jnp.where(seq_start, 0.0, carry_sc[half, tile])
        carry_sc[half, tile] = blocks[-1][rb - keep:]
        cw = cw_ref[...]
        outs = []
        for up in blocks:
            z = cw[0:1] * up
            zp = cw[0:1] * prev
            for k in range(1, CONV_WIDTH):
                r = pltpu.roll(z, 1, 0)
                head = jnp.where(row_in_head == 0, zp[keep - 1:keep], r[0:keep])
                z = cw[k:k + 1] * up + jnp.concatenate([head, r[keep:]], axis=0)
                if k + 1 < CONV_WIDTH:
                    zp = cw[k:k + 1] * prev + pltpu.roll(zp, 1, 0)
            outs.append(z + cb_ref[...])
            prev = up[rb - keep:]
        return outs

    def step(n_act):
        ups = [(up_blocks(wa_refs[s]), up_blocks(wb_refs[s])) for s in range(n_act)]
        fms = []
        for s in range(n_act):
            tile = f * n_sub + s
            ua = conv(ups[s][0], 0, tile, cwa_refs[s], cba_refs[s])
            ub = conv(ups[s][1], 1, tile, cwb_refs[s], cbb_refs[s])
            fms.append(jnp.concatenate(
                [(_gelu_tanh(a) * b).astype(BF16) for a, b in zip(ua, ub)], axis=0))
        fm = jnp.concatenate(fms, axis=1)
        wd = jnp.concatenate([wd_refs[s][...] for s in range(n_act)], axis=0)
        o_ref[...] += jnp.dot(fm, wd, preferred_element_type=F32)

    last = pl.num_programs(1) - 1
    tail = n_tiles - (n_tiles - 1) // n_sub * n_sub
    if tail == n_sub:
        step(n_sub)
    else:
        pl.when(f < last)(lambda: step(n_sub))
        pl.when(f == last)(lambda: step(tail))

    @pl.when(f == pl.num_programs(1) - 1)
    def _():
        def rows(sl):
            o_ref[sl, :] = h_ref[sl, :] + _rms(o_ref[sl, :], g4_ref[...])
        _for_row_blocks(tm, rows)


def _ffn(h1, g3, w_up, conv_w, conv_b, w_down, g4, seq_len, tm=512, sub=256, n_sub=2, keep=8,
         up_row_blocks=2):
    t, d = h1.shape
    d_ff = w_down.shape[0]
    n_tiles = d_ff // sub
    nf = pl.cdiv(n_tiles, n_sub)
    resident = pl.Buffered(1)

    def tile_of(s):
        return lambda f: jnp.minimum(f * n_sub + s, n_tiles - 1)

    def per_sub(shape, index):
        return [pl.BlockSpec(shape, functools.partial(index, tile_of(s))) for s in range(n_sub)]

    col_a = lambda tile, i, f: (0, tile(f))
    col_b = lambda tile, i, f: (0, n_tiles + tile(f))
    row_d = lambda tile, i, f: (tile(f), 0)
    in_specs = ([pl.BlockSpec((tm, d), lambda i, f: (i, 0), pipeline_mode=resident),
                 pl.BlockSpec((1, d), lambda i, f: (0, 0))]
                + per_sub((d, sub), col_a) + per_sub((d, sub), col_b)
                + per_sub((CONV_WIDTH, sub), col_a) + per_sub((CONV_WIDTH, sub), col_b)
                + per_sub((1, sub), col_a) + per_sub((1, sub), col_b)
                + per_sub((sub, d), row_d)
                + [pl.BlockSpec((1, d), lambda i, f: (0, 0))])
    operands = ([h1, g3] + [w_up] * (2 * n_sub) + [conv_w] * (2 * n_sub)
                + [conv_b] * (2 * n_sub) + [w_down] * n_sub + [g4])
    tf = sub * n_sub
    return pl.pallas_call(
        functools.partial(_ffn_kernel, n_sub=n_sub, n_tiles=n_tiles,
                          tiles_per_seq=seq_len // tm, keep=keep, up_row_blocks=up_row_blocks),
        out_shape=jax.ShapeDtypeStruct((t, d), F32),
        grid=(t // tm, nf),
        in_specs=in_specs,
        out_specs=pl.BlockSpec((tm, d), lambda i, f: (i, 0), pipeline_mode=resident),
        scratch_shapes=[pltpu.VMEM((tm, d), BF16),
                        pltpu.VMEM((2, nf * n_sub, keep, sub), F32)],
        compiler_params=_params(
            ("arbitrary", "arbitrary"),
            _vmem_limit(d * tf * 4 + tf * d * 2,
                        tm * d * 8 + tm * d * 2 + 2 * nf * tf * keep * 4, tm * tf * 32)),
        name="ffn",
    )(*operands)


def kernel(x, norm_pre_mix, w_in, ssm_lambda_re, ssm_lambda_im, ssm_log_step, ssm_b_re, ssm_b_im,
           ssm_c_re, ssm_c_im, ssm_d, ssm_glu_w, ssm_glu_b, pool_w, pool_b, pool_scale,
           w_branch_ssm, w_branch_pool, w_out, norm_post_mix, norm_pre_ffn, w_up, ffn_conv_w,
           ffn_conv_b, w_down, norm_post_ffn):
    bsz, seq_len, d = x.shape
    depth = w_in.shape[0]
    d_ssm = ssm_d.shape[1]
    d_pool = pool_scale.shape[1]
    n_groups = d_ssm // SSM_GROUP
    assert seq_len // CHUNK == 1 << N_SCAN_PASSES
    t = bsz * seq_len
    h = x.reshape(t, d)
    row = lambda v: v.reshape(1, -1)
    for i in range(depth):
        a = _norm(h, row(norm_pre_mix[i]))
        prest, w_up_b = _inproj(a, w_in[i], d_ssm, d_pool, w_up[i], 32)
        ut = _inproj_ssm(a, w_in[i], d_ssm, seq_len)
        m, rt, et, wsc = _ssm_prep(ssm_lambda_re[i], ssm_lambda_im[i], ssm_log_step[i],
                                   ssm_b_re[i], ssm_b_im[i], ssm_c_re[i], ssm_c_im[i])
        yt = _ssm(ut, m, rt, et, wsc, ssm_d[i].reshape(n_groups, SSM_GROUP, 1),
                  seq_len // CHUNK)
        ys = _glu(yt, ssm_glu_w[i].astype(BF16), row(ssm_glu_b[i]), seq_len)
        po = _pool(prest, pool_w[i].astype(BF16), pool_b[i][:, None, :], row(pool_scale[i]),
                   seq_len, d_pool)
        h, w_down_b = _mix(ys, po, prest, h, w_branch_ssm[i].astype(BF16),
                           w_branch_pool[i].astype(BF16), w_out[i].astype(BF16),
                           row(norm_post_mix[i]), d_pool, w_down[i], 128)
        h = _ffn(h, row(norm_pre_ffn[i]), w_up_b, ffn_conv_w[i], row(ffn_conv_b[i]),
                 w_down_b, row(norm_post_ffn[i]), seq_len)
    return h.reshape(bsz, seq_len, d)
```

```python
import functools
import math

import jax
import jax.numpy as jnp
from jax import lax
from jax.experimental import pallas as pl
from jax.experimental.pallas import tpu as pltpu

F32 = jnp.float32
BF16 = jnp.bfloat16

EPS = 1e-6
MIN_NEG_REAL = -1e-4
SSM_GROUP = 16
SSM_STATE = 64
POOL_WINDOWS = (2, 4, 8, 16)
CONV_WIDTH = 3
CHUNK = 16
N_SCAN_PASSES = 7
LANES = 128

V7X_VMEM_BYTES = 64 * 1024 * 1024
VMEM_CEILING = V7X_VMEM_BYTES - 6 * 1024 * 1024


def _vmem_limit(pipelined_bytes, scratch_bytes, temp_bytes):
    return min(2 * pipelined_bytes + scratch_bytes + temp_bytes + (4 << 20), VMEM_CEILING)


def _params(semantics, vmem):
    return pltpu.CompilerParams(dimension_semantics=semantics, vmem_limit_bytes=vmem)


def _rms(xf, g):
    ms = jnp.mean(xf * xf, axis=-1, keepdims=True)
    return xf * lax.rsqrt(ms + EPS) * g


NORM_ROWS = 32


def _for_row_blocks(n_rows, body):
    def step(r, carry):
        body(pl.ds(pl.multiple_of(r * NORM_ROWS, NORM_ROWS), NORM_ROWS))
        return carry
    lax.fori_loop(0, n_rows // NORM_ROWS, step, 0)


def _sigmoid(x):
    return 1.0 / (1.0 + jnp.exp(-x))


def _gelu_tanh(x):
    c = math.sqrt(2.0 / math.pi)
    return 0.5 * x * (1.0 + jnp.tanh(c * (x + 0.044715 * (x * x * x))))


def _norm_kernel(x_ref, g_ref, o_ref):
    def rows(sl):
        o_ref[sl, :] = _rms(x_ref[sl, :], g_ref[...]).astype(BF16)
    _for_row_blocks(x_ref.shape[0], rows)


def _norm(x2, g, tm=512):
    t, d = x2.shape
    return pl.pallas_call(
        _norm_kernel,
        out_shape=jax.ShapeDtypeStruct((t, d), BF16),
        grid=(t // tm,),
        in_specs=[pl.BlockSpec((tm, d), lambda i: (i, 0)),
                  pl.BlockSpec((1, d), lambda i: (0, 0))],
        out_specs=pl.BlockSpec((tm, d), lambda i: (i, 0)),
        compiler_params=_params(("arbitrary",), _vmem_limit(tm * d * 6, 0, tm * d * 8)),
        name="norm",
    )(x2, g)


def _side_cast_specs(src, rows, grid):
    n_blocks = src.shape[0] // rows
    assert src.shape[0] % rows == 0 and n_blocks <= math.prod(grid)

    def index(*g):
        step = g[0]
        for k in range(1, len(grid)):
            step = step * grid[k] + g[k]
        return (jnp.minimum(step, n_blocks - 1), 0)

    spec = pl.BlockSpec((rows, src.shape[1]), index)
    return spec, spec, jax.ShapeDtypeStruct(src.shape, BF16)


def _inproj_kernel(a_ref, w_ref, cast_ref, o_ref, cast_o_ref, *, n_plain, row_blocks):
    gate = pl.program_id(1) >= n_plain
    w = w_ref[...].astype(BF16)
    rows = a_ref.shape[0] // row_blocks
    for r0 in range(0, a_ref.shape[0], rows):
        p = jnp.dot(a_ref[r0:r0 + rows, :], w, preferred_element_type=F32)
        o_ref[r0:r0 + rows, :] = jnp.where(gate, _sigmoid(p), p).astype(BF16)
    cast_o_ref[...] = cast_ref[...].astype(BF16)


def _inproj(a, w, col0, n_plain_cols, cast_src, cast_rows, tm=1024, tn=512, row_blocks=4):
    t, d = a.shape
    n = w.shape[1] - col0
    c0 = col0 // tn
    grid = (t // tm, n // tn)
    cast_in, cast_out, cast_shape = _side_cast_specs(cast_src, cast_rows, grid)
    cast_bytes = cast_rows * cast_src.shape[1] * 6
    return pl.pallas_call(
        functools.partial(_inproj_kernel, n_plain=n_plain_cols // tn, row_blocks=row_blocks),
        out_shape=(jax.ShapeDtypeStruct((t, n), BF16), cast_shape),
        grid=grid,
        in_specs=[pl.BlockSpec((tm, d), lambda i, j: (i, 0)),
                  pl.BlockSpec((d, tn), lambda i, j: (0, c0 + j)),
                  cast_in],
        out_specs=(pl.BlockSpec((tm, tn), lambda i, j: (i, j)), cast_out),
        compiler_params=_params(("arbitrary", "arbitrary"),
                                _vmem_limit(tm * d * 2 + d * tn * 4 + tm * tn * 2 + cast_bytes, 0,
                                            d * tn * 2 + tm * tn * 8)),
        name="inproj",
    )(a, w, cast_src)


def _inproj_ssm_kernel(a_ref, w_ref, o_ref, p_sc, *, row_blocks):
    chunks = o_ref.shape[-1]
    n_slabs = p_sc.shape[0]
    w = w_ref[...].astype(BF16)
    rows = a_ref.shape[0] // row_blocks
    for r0 in range(0, a_ref.shape[0], rows):
        p = jnp.dot(a_ref[r0:r0 + rows, :], w, preferred_element_type=F32)
        for s in range(n_slabs):
            p_sc[s, r0:r0 + rows] = p[:, s * LANES:(s + 1) * LANES]
    gps = LANES // SSM_GROUP
    for q in range(CHUNK):
        for s in range(n_slabs):
            blk = p_sc[s, pl.ds(q, chunks, stride=CHUNK), :]
            o_ref[s * gps:(s + 1) * gps, q] = (
                blk.T.reshape(gps, SSM_GROUP, chunks).astype(BF16))


def _inproj_ssm(a, w, n, seq_len, tn=256, row_blocks=8):
    t, d = a.shape
    g = n // SSM_GROUP
    chunks = seq_len // CHUNK
    return pl.pallas_call(
        functools.partial(_inproj_ssm_kernel, row_blocks=row_blocks),
        out_shape=jax.ShapeDtypeStruct((g, CHUNK, SSM_GROUP, t // CHUNK), BF16),
        grid=(t // seq_len, n // tn),
        in_specs=[pl.BlockSpec((seq_len, d), lambda bi, j: (bi, 0)),
                  pl.BlockSpec((d, tn), lambda bi, j: (0, j))],
        out_specs=pl.BlockSpec((tn // SSM_GROUP, CHUNK, SSM_GROUP, chunks),
                               lambda bi, j: (j, 0, 0, bi)),
        scratch_shapes=[pltpu.VMEM((tn // LANES, seq_len, LANES), F32)],
        compiler_params=_params(("arbitrary", "arbitrary"),
                                _vmem_limit(seq_len * d * 2 + d * tn * 4 + tn * seq_len * 2,
                                            seq_len * tn * 4, d * tn * 2 + seq_len * tn * 16)),
        name="inproj_ssm",
    )(a, w)


def _ssm_prep_kernel(lre_ref, lim_ref, ls_ref, btr_ref, bti_ref, cr_ref, ci_ref,
                     m_ref, rt_ref, et_ref, w_ref, *, groups):
    p2 = 2 * SSM_STATE
    cq = CHUNK * SSM_GROUP
    lane = lax.broadcasted_iota(jnp.int32, (1, p2), 1)
    is_re = lane < SSM_STATE
    sgn = jnp.where(is_re, 1.0, -1.0).astype(F32)
    lane_blk = lax.broadcasted_iota(jnp.int32, (cq, cq), 1) // SSM_GROUP
    row8 = lax.broadcasted_iota(jnp.int32, (8, p2), 0)
    is_re8 = lax.broadcasted_iota(jnp.int32, (8, p2), 1) < SSM_STATE

    for gi in range(groups):
        lr = jnp.minimum(lre_ref[gi], MIN_NEG_REAL)
        li = lim_ref[gi]
        dt = jnp.exp(ls_ref[gi])
        mag = jnp.exp(lr * dt)
        ang = li * dt
        ar = mag * jnp.cos(ang)
        ai = mag * jnp.sin(ang)
        nr = ar - 1.0
        ni = ai
        den = lr * lr + li * li
        f_re = (nr * lr + ni * li) / den
        f_im = (ni * lr - nr * li) / den
        btr = btr_ref[gi]
        bti = bti_ref[gi]
        bbr = f_re * btr - f_im * bti
        bbi = f_re * bti + f_im * btr
        cr = cr_ref[gi]
        ci = ci_ref[gi]

        pa = [jnp.where(is_re, 1.0, 0.0).astype(F32)]
        pb = [jnp.where(is_re, 0.0, 1.0).astype(F32)]
        for _ in range(CHUNK):
            a_prev, b_prev = pa[-1], pb[-1]
            pa.append(a_prev * ar + b_prev * ai)
            pb.append(b_prev * ar - a_prev * ai)

        gk = [sgn * (cr * pa[k] + ci * pb[k]) for k in range(CHUNK + 1)]
        fs = jnp.concatenate(gk[:CHUNK], axis=0)
        et = jnp.concatenate(gk[1:], axis=0)
        rtt = jnp.concatenate(
            [bbr * pa[CHUNK - 1 - q] + bbi * pb[CHUNK - 1 - q] for q in range(CHUNK)], axis=0)
        bst = jnp.where(is_re, bbr, bbi)
        bst_t = jnp.concatenate([bst] * CHUNK, axis=0)
        kw = lax.dot_general(fs, bst_t, (((1,), (1,)), ((), ())),
                             precision=lax.Precision.HIGHEST, preferred_element_type=F32)
        m = jnp.zeros((cq, cq), F32)
        for q in range(CHUNK):
            if q == 0:
                shifted = kw
            else:
                shifted = jnp.concatenate(
                    [jnp.zeros((q * SSM_GROUP, cq), F32), kw[:cq - q * SSM_GROUP]], axis=0)
            m = jnp.where(lane_blk == q, shifted, m)
        m_ref[gi] = m.astype(BF16)
        rt_ref[gi] = rtt.T.astype(BF16)
        et_ref[gi] = et.astype(BF16)

        wa = jnp.broadcast_to(pa[CHUNK], (8, p2))
        rows = jnp.zeros((8, p2), F32)
        for k in range(N_SCAN_PASSES):
            rows = jnp.where(row8 == k, wa, rows)
            swapped = pltpu.roll(wa, SSM_STATE, 1)
            re2 = jnp.where(is_re8, wa, swapped)
            im2 = jnp.where(is_re8, swapped, wa)
            wb = jnp.where(is_re8, -swapped, swapped)
            wa = wa * re2 + wb * im2
        full = jnp.concatenate([rows, jnp.zeros((p2 - 8, p2), F32)], axis=0)
        w_ref[gi] = full.T


def _ssm_prep(lam_re, lam_im, log_step, b_re, b_im, c_re, c_im, groups=8):
    g, p = lam_re.shape
    p2 = 2 * p
    cq = CHUNK * SSM_GROUP
    dup = lambda v: jnp.concatenate([v, v], axis=-1)
    lre = dup(lam_re)[:, None, :]
    lim = dup(lam_im)[:, None, :]
    ls = jnp.broadcast_to(log_step[:, None, None], (g, 1, p2))
    btr = dup(jnp.swapaxes(b_re, 1, 2))
    bti = dup(jnp.swapaxes(b_im, 1, 2))
    cr = dup(c_re)
    ci = dup(c_im)
    row = pl.BlockSpec((groups, 1, p2), lambda i: (i, 0, 0))
    mat = pl.BlockSpec((groups, SSM_GROUP, p2), lambda i: (i, 0, 0))
    return pl.pallas_call(
        functools.partial(_ssm_prep_kernel, groups=groups),
        out_shape=(jax.ShapeDtypeStruct((g, cq, cq), BF16),
                   jax.ShapeDtypeStruct((g, p2, cq), BF16),
                   jax.ShapeDtypeStruct((g, cq, p2), BF16),
                   jax.ShapeDtypeStruct((g, p2, p2), F32)),
        grid=(g // groups,),
        in_specs=[row, row, row, mat, mat, mat, mat],
        out_specs=(pl.BlockSpec((groups, cq, cq), lambda i: (i, 0, 0)),
                   pl.BlockSpec((groups, p2, cq), lambda i: (i, 0, 0)),
                   pl.BlockSpec((groups, cq, p2), lambda i: (i, 0, 0)),
                   pl.BlockSpec((groups, p2, p2), lambda i: (i, 0, 0))),
        compiler_params=_params(("arbitrary",), _vmem_limit(groups * cq * cq * 4, 0, 8 << 20)),
        name="ssm_prep",
    )(lre, lim, ls, btr, bti, cr, ci)


def _ssm_kernel(*refs, groups, chunks_per_seq, n_cast):
    u_ref, m_ref, rt_ref, et_ref, w_ref, d_ref = refs[:6]
    o_ref = refs[6 + n_cast]
    for src, dst in zip(refs[6:6 + n_cast], refs[7 + n_cast:]):
        dst[...] = src[...].astype(BF16)
    cq = CHUNK * SSM_GROUP
    nc = u_ref.shape[-1]
    pos = lax.broadcasted_iota(jnp.int32, (SSM_STATE, nc), 1) % chunks_per_seq
    for gi in range(groups):
        x3 = u_ref[gi]
        x = x3.reshape(cq, nc)
        y = jnp.dot(m_ref[gi], x, preferred_element_type=F32)
        r = jnp.dot(rt_ref[gi], x, preferred_element_type=F32)
        s_re, s_im = r[:SSM_STATE], r[SSM_STATE:]
        for k in range(N_SCAN_PASSES):
            dist = 1 << k
            wr = w_ref[gi, 0:SSM_STATE, k:k + 1]
            wi = w_ref[gi, SSM_STATE:2 * SSM_STATE, k:k + 1]
            keep = pos >= dist
            sh_re = jnp.where(keep, pltpu.roll(s_re, dist, 1), 0.0)
            sh_im = jnp.where(keep, pltpu.roll(s_im, dist, 1), 0.0)
            s_re, s_im = (s_re + wr * sh_re - wi * sh_im,
                          s_im + wr * sh_im + wi * sh_re)
        first = pos >= 1
        sp = jnp.concatenate([jnp.where(first, pltpu.roll(s_re, 1, 1), 0.0),
                              jnp.where(first, pltpu.roll(s_im, 1, 1), 0.0)], axis=0)
        y = y + jnp.dot(et_ref[gi], sp.astype(BF16), preferred_element_type=F32)
        y3 = y.reshape(CHUNK, SSM_GROUP, nc) + d_ref[gi][None] * x3.astype(F32)
        o_ref[gi] = _gelu_tanh(y3).astype(BF16)


def _ssm(ut, m, rt, et, w, d, chunks_per_seq, cast_srcs, groups=4):
    g, q, j, nc = ut.shape
    cq = q * j
    p2 = rt.shape[1]
    grid = (g // groups,)
    blk = lambda *s: pl.BlockSpec((groups,) + s, lambda i: (i,) + (0,) * len(s))
    casts = [_side_cast_specs(src, src.shape[0] // grid[0], grid) for src in cast_srcs]
    cast_bytes = sum(src.size // grid[0] * 6 for src in cast_srcs)
    return pl.pallas_call(
        functools.partial(_ssm_kernel, groups=groups, chunks_per_seq=chunks_per_seq,
                          n_cast=len(cast_srcs)),
        out_shape=(jax.ShapeDtypeStruct(ut.shape, BF16), *[c[2] for c in casts]),
        grid=grid,
        in_specs=[blk(q, j, nc), blk(cq, cq), blk(p2, cq), blk(cq, p2), blk(p2, p2), blk(j, 1),
                  *[c[0] for c in casts]],
        out_specs=(blk(q, j, nc), *[c[1] for c in casts]),
        compiler_params=_params(("arbitrary",),
                                _vmem_limit(groups * cq * nc * 8 + cast_bytes, 0, 8 << 20)),
        name="ssm",
    )(ut, m, rt, et, w, d, *cast_srcs)


def _glu_kernel(y_ref, w_ref, b_ref, o_ref, y_sc, *, tn):
    n = pl.program_id(1)
    chunks = y_ref.shape[-1]
    n_slabs = y_sc.shape[0]

    @pl.when(n == 0)
    def _():
        for q in range(CHUNK):
            yq = y_ref[:, q]
            yq = yq.reshape(yq.shape[0] * yq.shape[1], chunks).astype(F32).T
            for s in range(n_slabs):
                y_sc[s, pl.ds(q, chunks, stride=CHUNK), :] = yq[:, s * LANES:(s + 1) * LANES]

    y = jnp.concatenate([y_sc[s] for s in range(n_slabs)], axis=1).astype(BF16)
    z = jnp.dot(y, w_ref[...], preferred_element_type=F32) + b_ref[...]
    per_tile = tn // LANES
    yn = jnp.concatenate([y_sc[n * per_tile + s] for s in range(per_tile)], axis=1)
    o_ref[...] = (yn * _sigmoid(z)).astype(BF16)


def _glu(yt, w, b, seq_len, tn=512):
    g, q, j, nc = yt.shape
    d = g * j
    chunks = seq_len // q
    t = nc * q
    return pl.pallas_call(
        functools.partial(_glu_kernel, tn=tn),
        out_shape=jax.ShapeDtypeStruct((t, d), BF16),
        grid=(nc // chunks, d // tn),
        in_specs=[pl.BlockSpec((g, q, j, chunks), lambda bi, n: (0, 0, 0, bi)),
                  pl.BlockSpec((d, tn), lambda bi, n: (0, n)),
                  pl.BlockSpec((1, tn), lambda bi, n: (0, n))],
        out_specs=pl.BlockSpec((seq_len, tn), lambda bi, n: (bi, n)),
        scratch_shapes=[pltpu.VMEM((d // LANES, seq_len, LANES), F32)],
        compiler_params=_params(("arbitrary", "arbitrary"),
                                _vmem_limit((seq_len * d + d * tn + seq_len * tn) * 2,
                                            seq_len * d * 4, seq_len * d * 2 + seq_len * tn * 12)),
        name="glu",
    )(yt, w, b)


def _pool_kernel(u_ref, w_ref, b_ref, s_ref, o_ref, ext_sc, *, tiles_per_seq, halo):
    i = pl.program_id(0)
    tm = u_ref.shape[0]
    gp = w_ref.shape[1]

    @pl.when(i % tiles_per_seq == 0)
    def _():
        ext_sc[0:halo] = jnp.zeros((halo, ext_sc.shape[1]), BF16)

    @pl.when(i % tiles_per_seq != 0)
    def _():
        ext_sc[0:halo] = ext_sc[tm:tm + halo]

    ext_sc[halo:halo + tm] = u_ref[...]

    t_loc = lax.broadcasted_iota(jnp.int32, (tm, tm + halo), 0)
    s_loc = lax.broadcasted_iota(jnp.int32, (tm, tm + halo), 1)
    lag = t_loc + halo - s_loc
    t_seq = (i % tiles_per_seq) * tm + lax.broadcasted_iota(jnp.int32, (tm, 1), 0)
    for gi, win in enumerate(POOL_WINDOWS):
        band = jnp.where((lag >= 0) & (lag < win), 1.0, 0.0).astype(BF16)
        cols = slice(gi * gp, (gi + 1) * gp)
        wsum = jnp.dot(band, ext_sc[:, cols], preferred_element_type=F32)
        cnt = jnp.minimum(t_seq + 1, win).astype(F32)
        z = wsum / cnt - u_ref[:, cols].astype(F32)
        z = jnp.dot(z.astype(BF16), w_ref[gi], preferred_element_type=F32) + b_ref[gi]
        o_ref[:, cols] = (z * s_ref[:, cols]).astype(BF16)


def _pool(prest, w, b, scale, seq_len, d_pool, tm=512, halo=128):
    t = prest.shape[0]
    ng, gp, _ = w.shape
    return pl.pallas_call(
        functools.partial(_pool_kernel, tiles_per_seq=seq_len // tm, halo=halo),
        out_shape=jax.ShapeDtypeStruct((t, d_pool), BF16),
        grid=(t // tm,),
        in_specs=[pl.BlockSpec((tm, d_pool), lambda i: (i, 0)),
                  pl.BlockSpec((ng, gp, gp), lambda i: (0, 0, 0)),
                  pl.BlockSpec((ng, 1, gp), lambda i: (0, 0, 0)),
                  pl.BlockSpec((1, d_pool), lambda i: (0, 0))],
        out_specs=pl.BlockSpec((tm, d_pool), lambda i: (i, 0)),
        scratch_shapes=[pltpu.VMEM((tm + halo, d_pool), BF16)],
        compiler_params=_params(("arbitrary",),
                                _vmem_limit(tm * d_pool * 4 + ng * gp * gp * 2,
                                            (tm + halo) * d_pool * 2, 8 << 20)),
        name="pool",
    )(prest, w, b, scale)


def _mix_kernel(ys_ref, po_ref, gs_ref, gp_ref, x_ref, wbs_ref, wbp_ref, wo_ref, g_ref, cast_ref,
                o_ref, cast_o_ref, x_sc, *, sub):
    n = pl.program_id(1)
    cast_o_ref[...] = cast_ref[...].astype(BF16)
    tn = x_ref.shape[1]
    x_sc[:, pl.ds(pl.multiple_of(n * tn, tn), tn)] = x_ref[...]

    @pl.when(n == 0)
    def _():
        o_ref[...] = jnp.zeros_like(o_ref)

    subs = [slice(s * sub, (s + 1) * sub) for s in range(wo_ref.shape[0] // sub)]
    branch = [(jnp.dot(ys_ref[...], wbs_ref[:, cols], preferred_element_type=F32),
               jnp.dot(po_ref[...], wbp_ref[:, cols], preferred_element_type=F32))
              for cols in subs]
    merged = [(gs_ref[:, cols].astype(F32) * ms + gp_ref[:, cols].astype(F32) * mp).astype(BF16)
              for cols, (ms, mp) in zip(subs, branch)]
    o_ref[...] += jnp.dot(jnp.concatenate(merged, axis=1), wo_ref[...],
                          preferred_element_type=F32)

    @pl.when(n == pl.num_programs(1) - 1)
    def _():
        def rows(sl):
            o_ref[sl, :] = x_sc[sl, :] + _rms(o_ref[sl, :], g_ref[...])
        _for_row_blocks(o_ref.shape[0], rows)


def _mix(ys, po, prest, x2, wbs, wbp, wo, g, gate_col0, cast_src, cast_rows,
         tm=512, tn=512, sub=256):
    t, d = x2.shape
    ds_ = ys.shape[1]
    dp = po.shape[1]
    c0 = gate_col0 // tn
    c1 = (gate_col0 + d) // tn
    resident = pl.Buffered(1)
    grid = (t // tm, d // tn)
    cast_in, cast_out, cast_shape = _side_cast_specs(cast_src, cast_rows, grid)
    cast_bytes = cast_rows * cast_src.shape[1] * 6
    return pl.pallas_call(
        functools.partial(_mix_kernel, sub=sub),
        out_shape=(jax.ShapeDtypeStruct((t, d), F32), cast_shape),
        grid=grid,
        in_specs=[pl.BlockSpec((tm, ds_), lambda i, n: (i, 0)),
                  pl.BlockSpec((tm, dp), lambda i, n: (i, 0)),
                  pl.BlockSpec((tm, tn), lambda i, n: (i, c0 + n)),
                  pl.BlockSpec((tm, tn), lambda i, n: (i, c1 + n)),
                  pl.BlockSpec((tm, tn), lambda i, n: (i, n)),
                  pl.BlockSpec((ds_, tn), lambda i, n: (0, n)),
                  pl.BlockSpec((dp, tn), lambda i, n: (0, n)),
                  pl.BlockSpec((tn, d), lambda i, n: (n, 0)),
                  pl.BlockSpec((1, d), lambda i, n: (0, 0)),
                  cast_in],
        out_specs=(pl.BlockSpec((tm, d), lambda i, n: (i, 0), pipeline_mode=resident), cast_out),
        scratch_shapes=[pltpu.VMEM((tm, d), F32)],
        compiler_params=_params(
            ("arbitrary", "arbitrary"),
            _vmem_limit(tm * (ds_ + dp) * 2 + tm * tn * 8 + (ds_ + dp) * tn * 2 + tn * d * 2
                        + cast_bytes, tm * d * 8, tm * tn * 16)),
        name="mix",
    )(ys, po, prest, prest, x2, wbs, wbp, wo, g, cast_src)


def _ffn_kernel(*refs, n_sub, n_tiles, tiles_per_seq, keep, up_row_blocks):
    h_ref, g3_ref = refs[:2]
    groups = [refs[2 + k * n_sub:2 + (k + 1) * n_sub] for k in range(7)]
    wa_refs, wb_refs, cwa_refs, cwb_refs, cba_refs, cbb_refs, wd_refs = groups
    g4_ref, o_ref, c_sc, carry_sc = refs[2 + 7 * n_sub:]
    i = pl.program_id(0)
    f = pl.program_id(1)
    tm = h_ref.shape[0]

    @pl.when(f == 0)
    def _():
        def rows(sl):
            c_sc[sl, :] = _rms(h_ref[sl, :], g3_ref[...]).astype(BF16)
            o_ref[sl, :] = jnp.zeros((NORM_ROWS, o_ref.shape[1]), F32)
        _for_row_blocks(tm, rows)

    seq_start = i % tiles_per_seq == 0

    rb = tm // up_row_blocks
    row_in_head = lax.broadcasted_iota(jnp.int32, (keep, wd_refs[0].shape[0]), 0)

    def up_blocks(w_ref):
        w = w_ref[...]
        return [jnp.dot(c_sc[r0:r0 + rb, :], w, preferred_element_type=F32)
                for r0 in range(0, tm, rb)]

    def conv(blocks, half, tile, cw_ref, cb_ref):
        prev = jnp.where(seq_start, 0.0, carry_sc[half, tile])
        carry_sc[half, tile] = blocks[-1][rb - keep:]
        cw = cw_ref[...]
        outs = []
        for up in blocks:
            z = cw[0:1] * up
            zp = cw[0:1] * prev
            for k in range(1, CONV_WIDTH):
                r = pltpu.roll(z, 1, 0)
                head = jnp.where(row_in_head == 0, zp[keep - 1:keep], r[0:keep])
                z = cw[k:k + 1] * up + jnp.concatenate([head, r[keep:]], axis=0)
                if k + 1 < CONV_WIDTH:
                    zp = cw[k:k + 1] * prev + pltpu.roll(zp, 1, 0)
            outs.append(z + cb_ref[...])
            prev = up[rb - keep:]
        return outs

    def step(n_act):
        ups = [(up_blocks(wa_refs[s]), up_blocks(wb_refs[s])) for s in range(n_act)]
        fms = []
        for s in range(n_act):
            tile = f * n_sub + s
            ua = conv(ups[s][0], 0, tile, cwa_refs[s], cba_refs[s])
            ub = conv(ups[s][1], 1, tile, cwb_refs[s], cbb_refs[s])
            fms.append(jnp.concatenate(
                [(_gelu_tanh(a) * b).astype(BF16) for a, b in zip(ua, ub)], axis=0))
        fm = jnp.concatenate(fms, axis=1)
        wd = jnp.concatenate([wd_refs[s][...] for s in range(n_act)], axis=0)
        o_ref[...] += jnp.dot(fm, wd, preferred_element_type=F32)

    last = pl.num_programs(1) - 1
    tail = n_tiles - (n_tiles - 1) // n_sub * n_sub
    if tail == n_sub:
        step(n_sub)
    else:
        pl.when(f < last)(lambda: step(n_sub))
        pl.when(f == last)(lambda: step(tail))

    @pl.when(f == pl.num_programs(1) - 1)
    def _():
        def rows(sl):
            o_ref[sl, :] = h_ref[sl, :] + _rms(o_ref[sl, :], g4_ref[...])
        _for_row_blocks(tm, rows)


def _ffn(h1, g3, w_up, conv_w, conv_b, w_down, g4, seq_len, tm=512, sub=256, n_sub=2, keep=8,
         up_row_blocks=2):
    t, d = h1.shape
    d_ff = w_down.shape[0]
    n_tiles = d_ff // sub
    nf = pl.cdiv(n_tiles, n_sub)
    resident = pl.Buffered(1)

    def tile_of(s):
        return lambda f: jnp.minimum(f * n_sub + s, n_tiles - 1)

    def per_sub(shape, index):
        return [pl.BlockSpec(shape, functools.partial(index, tile_of(s))) for s in range(n_sub)]

    col_a = lambda tile, i, f: (0, tile(f))
    col_b = lambda tile, i, f: (0, n_tiles + tile(f))
    row_d = lambda tile, i, f: (tile(f), 0)
    in_specs = ([pl.BlockSpec((tm, d), lambda i, f: (i, 0), pipeline_mode=resident),
                 pl.BlockSpec((1, d), lambda i, f: (0, 0))]
                + per_sub((d, sub), col_a) + per_sub((d, sub), col_b)
                + per_sub((CONV_WIDTH, sub), col_a) + per_sub((CONV_WIDTH, sub), col_b)
                + per_sub((1, sub), col_a) + per_sub((1, sub), col_b)
                + per_sub((sub, d), row_d)
                + [pl.BlockSpec((1, d), lambda i, f: (0, 0))])
    operands = ([h1, g3] + [w_up] * (2 * n_sub) + [conv_w] * (2 * n_sub)
                + [conv_b] * (2 * n_sub) + [w_down] * n_sub + [g4])
    tf = sub * n_sub
    return pl.pallas_call(
        functools.partial(_ffn_kernel, n_sub=n_sub, n_tiles=n_tiles,
                          tiles_per_seq=seq_len // tm, keep=keep, up_row_blocks=up_row_blocks),
        out_shape=jax.ShapeDtypeStruct((t, d), F32),
        grid=(t // tm, nf),
        in_specs=in_specs,
        out_specs=pl.BlockSpec((tm, d), lambda i, f: (i, 0)),
        scratch_shapes=[pltpu.VMEM((tm, d), BF16),
                        pltpu.VMEM((2, nf * n_sub, keep, sub), F32)],
        compiler_params=_params(
            ("arbitrary", "arbitrary"),
            _vmem_limit(d * tf * 4 + tf * d * 2 + tm * d * 4,
                        tm * d * 4 + tm * d * 2 + 2 * nf * tf * keep * 4, tm * tf * 16)),
        name="ffn",
    )(*operands)


def kernel(x, norm_pre_mix, w_in, ssm_lambda_re, ssm_lambda_im, ssm_log_step, ssm_b_re, ssm_b_im,
           ssm_c_re, ssm_c_im, ssm_d, ssm_glu_w, ssm_glu_b, pool_w, pool_b, pool_scale,
           w_branch_ssm, w_branch_pool, w_out, norm_post_mix, norm_pre_ffn, w_up, ffn_conv_w,
           ffn_conv_b, w_down, norm_post_ffn):
    bsz, seq_len, d = x.shape
    depth = w_in.shape[0]
    d_ssm = ssm_d.shape[1]
    d_pool = pool_scale.shape[1]
    n_groups = d_ssm // SSM_GROUP
    assert seq_len // CHUNK == 1 << N_SCAN_PASSES
    t = bsz * seq_len
    h = x.reshape(t, d)
    row = lambda v: v.reshape(1, -1)
    for i in range(depth):
        a = _norm(h, row(norm_pre_mix[i]))
        prest, w_up_b = _inproj(a, w_in[i], d_ssm, d_pool, w_up[i], 32)
        ut = _inproj_ssm(a, w_in[i], d_ssm, seq_len)
        m, rt, et, wsc = _ssm_prep(ssm_lambda_re[i], ssm_lambda_im[i], ssm_log_step[i],
                                   ssm_b_re[i], ssm_b_im[i], ssm_c_re[i], ssm_c_im[i])
        yt, w_out_b, w_bs_b, w_bp_b = _ssm(
            ut, m, rt, et, wsc, ssm_d[i].reshape(n_groups, SSM_GROUP, 1), seq_len // CHUNK,
            [w_out[i], w_branch_ssm[i], w_branch_pool[i]])
        ys = _glu(yt, ssm_glu_w[i].astype(BF16), row(ssm_glu_b[i]), seq_len)
        po = _pool(prest, pool_w[i].astype(BF16), pool_b[i][:, None, :], row(pool_scale[i]),
                   seq_len, d_pool)
        h, w_down_b = _mix(ys, po, prest, h, w_bs_b, w_bp_b, w_out_b,
                           row(norm_post_mix[i]), d_pool, w_down[i], 128)
        h = _ffn(h, row(norm_pre_ffn[i]), w_up_b, ffn_conv_w[i], row(ffn_conv_b[i]),
                 w_down_b, row(norm_post_ffn[i]), seq_len)
    return h.reshape(bsz, seq_len, d)
```

```python
import functools
import math

import jax
import jax.numpy as jnp
from jax import lax
from jax.experimental import pallas as pl
from jax.experimental.pallas import tpu as pltpu

F32 = jnp.float32
BF16 = jnp.bfloat16

EPS = 1e-6
MIN_NEG_REAL = -1e-4
SSM_GROUP = 16
SSM_STATE = 64
POOL_WINDOWS = (2, 4, 8, 16)
CONV_WIDTH = 3
CHUNK = 16
N_SCAN_PASSES = 7
LANES = 128

V7X_VMEM_BYTES = 64 * 1024 * 1024
VMEM_CEILING = V7X_VMEM_BYTES - 6 * 1024 * 1024


def _vmem_limit(pipelined_bytes, scratch_bytes, temp_bytes):
    return min(2 * pipelined_bytes + scratch_bytes + temp_bytes + (4 << 20), VMEM_CEILING)


def _params(semantics, vmem):
    return pltpu.CompilerParams(dimension_semantics=semantics, vmem_limit_bytes=vmem)


def _rms(xf, g):
    ms = jnp.mean(xf * xf, axis=-1, keepdims=True)
    return xf * lax.rsqrt(ms + EPS) * g


NORM_ROWS = 32


def _for_row_blocks(n_rows, body):
    def step(r, carry):
        body(pl.ds(pl.multiple_of(r * NORM_ROWS, NORM_ROWS), NORM_ROWS))
        return carry
    lax.fori_loop(0, n_rows // NORM_ROWS, step, 0, unroll=2)


def _sigmoid(x):
    return 1.0 / (1.0 + jnp.exp(-x))


def _gelu_tanh(x):
    c = math.sqrt(2.0 / math.pi)
    return 0.5 * x * (1.0 + jnp.tanh(c * (x + 0.044715 * (x * x * x))))


def _norm_kernel(x_ref, g_ref, o_ref):
    def rows(sl):
        o_ref[sl, :] = _rms(x_ref[sl, :], g_ref[...]).astype(BF16)
    _for_row_blocks(x_ref.shape[0], rows)


def _norm(x2, g, tm=512):
    t, d = x2.shape
    return pl.pallas_call(
        _norm_kernel,
        out_shape=jax.ShapeDtypeStruct((t, d), BF16),
        grid=(t // tm,),
        in_specs=[pl.BlockSpec((tm, d), lambda i: (i, 0)),
                  pl.BlockSpec((1, d), lambda i: (0, 0))],
        out_specs=pl.BlockSpec((tm, d), lambda i: (i, 0)),
        compiler_params=_params(("arbitrary",), _vmem_limit(tm * d * 6, 0, tm * d * 8)),
        name="norm",
    )(x2, g)


def _side_cast_specs(src, rows, grid):
    n_blocks = src.shape[0] // rows
    assert src.shape[0] % rows == 0 and n_blocks <= math.prod(grid)

    def index(*g):
        step = g[0]
        for k in range(1, len(grid)):
            step = step * grid[k] + g[k]
        return (jnp.minimum(step, n_blocks - 1), 0)

    spec = pl.BlockSpec((rows, src.shape[1]), index)
    return spec, spec, jax.ShapeDtypeStruct(src.shape, BF16)


def _inproj_kernel(a_ref, w_ref, cast_ref, o_ref, cast_o_ref, *, n_plain, row_blocks):
    gate = pl.program_id(1) >= n_plain
    w = w_ref[...].astype(BF16)
    rows = a_ref.shape[0] // row_blocks
    for r0 in range(0, a_ref.shape[0], rows):
        p = jnp.dot(a_ref[r0:r0 + rows, :], w, preferred_element_type=F32)
        o_ref[r0:r0 + rows, :] = jnp.where(gate, _sigmoid(p), p).astype(BF16)
    cast_o_ref[...] = cast_ref[...].astype(BF16)


def _inproj(a, w, col0, n_plain_cols, cast_src, cast_rows, tm=1024, tn=512, row_blocks=4):
    t, d = a.shape
    n = w.shape[1] - col0
    c0 = col0 // tn
    grid = (t // tm, n // tn)
    cast_in, cast_out, cast_shape = _side_cast_specs(cast_src, cast_rows, grid)
    cast_bytes = cast_rows * cast_src.shape[1] * 6
    return pl.pallas_call(
        functools.partial(_inproj_kernel, n_plain=n_plain_cols // tn, row_blocks=row_blocks),
        out_shape=(jax.ShapeDtypeStruct((t, n), BF16), cast_shape),
        grid=grid,
        in_specs=[pl.BlockSpec((tm, d), lambda i, j: (i, 0)),
                  pl.BlockSpec((d, tn), lambda i, j: (0, c0 + j)),
                  cast_in],
        out_specs=(pl.BlockSpec((tm, tn), lambda i, j: (i, j)), cast_out),
        compiler_params=_params(("arbitrary", "arbitrary"),
                                _vmem_limit(tm * d * 2 + d * tn * 4 + tm * tn * 2 + cast_bytes, 0,
                                            d * tn * 2 + tm * tn * 8)),
        name="inproj",
    )(a, w, cast_src)


def _inproj_ssm_kernel(a_ref, w_ref, o_ref, p_sc, *, row_blocks):
    chunks = o_ref.shape[-1]
    n_slabs = p_sc.shape[0]
    w = w_ref[...].astype(BF16)
    rows = a_ref.shape[0] // row_blocks
    for r0 in range(0, a_ref.shape[0], rows):
        p = jnp.dot(a_ref[r0:r0 + rows, :], w, preferred_element_type=F32)
        for s in range(n_slabs):
            p_sc[s, r0:r0 + rows] = p[:, s * LANES:(s + 1) * LANES]
    gps = LANES // SSM_GROUP
    for q in range(CHUNK):
        for s in range(n_slabs):
            blk = p_sc[s, pl.ds(q, chunks, stride=CHUNK), :]
            o_ref[s * gps:(s + 1) * gps, q] = (
                blk.T.reshape(gps, SSM_GROUP, chunks).astype(BF16))


def _inproj_ssm(a, w, n, seq_len, tn=256, row_blocks=8):
    t, d = a.shape
    g = n // SSM_GROUP
    chunks = seq_len // CHUNK
    return pl.pallas_call(
        functools.partial(_inproj_ssm_kernel, row_blocks=row_blocks),
        out_shape=jax.ShapeDtypeStruct((g, CHUNK, SSM_GROUP, t // CHUNK), BF16),
        grid=(t // seq_len, n // tn),
        in_specs=[pl.BlockSpec((seq_len, d), lambda bi, j: (bi, 0)),
                  pl.BlockSpec((d, tn), lambda bi, j: (0, j))],
        out_specs=pl.BlockSpec((tn // SSM_GROUP, CHUNK, SSM_GROUP, chunks),
                               lambda bi, j: (j, 0, 0, bi)),
        scratch_shapes=[pltpu.VMEM((tn // LANES, seq_len, LANES), F32)],
        compiler_params=_params(("arbitrary", "arbitrary"),
                                _vmem_limit(seq_len * d * 2 + d * tn * 4 + tn * seq_len * 2,
                                            seq_len * tn * 4, d * tn * 2 + seq_len * tn * 16)),
        name="inproj_ssm",
    )(a, w)


def _ssm_prep_kernel(lre_ref, lim_ref, ls_ref, btr_ref, bti_ref, cr_ref, ci_ref,
                     m_ref, rt_ref, et_ref, w_ref, *, groups):
    p2 = 2 * SSM_STATE
    cq = CHUNK * SSM_GROUP
    lane = lax.broadcasted_iota(jnp.int32, (1, p2), 1)
    is_re = lane < SSM_STATE
    sgn = jnp.where(is_re, 1.0, -1.0).astype(F32)
    lane_blk = lax.broadcasted_iota(jnp.int32, (cq, cq), 1) // SSM_GROUP
    row8 = lax.broadcasted_iota(jnp.int32, (8, p2), 0)
    is_re8 = lax.broadcasted_iota(jnp.int32, (8, p2), 1) < SSM_STATE

    for gi in range(groups):
        lr = jnp.minimum(lre_ref[gi], MIN_NEG_REAL)
        li = lim_ref[gi]
        dt = jnp.exp(ls_ref[gi])
        mag = jnp.exp(lr * dt)
        ang = li * dt
        ar = mag * jnp.cos(ang)
        ai = mag * jnp.sin(ang)
        nr = ar - 1.0
        ni = ai
        den = lr * lr + li * li
        f_re = (nr * lr + ni * li) / den
        f_im = (ni * lr - nr * li) / den
        btr = btr_ref[gi]
        bti = bti_ref[gi]
        bbr = f_re * btr - f_im * bti
        bbi = f_re * bti + f_im * btr
        cr = cr_ref[gi]
        ci = ci_ref[gi]

        pa = [jnp.where(is_re, 1.0, 0.0).astype(F32)]
        pb = [jnp.where(is_re, 0.0, 1.0).astype(F32)]
        for _ in range(CHUNK):
            a_prev, b_prev = pa[-1], pb[-1]
            pa.append(a_prev * ar + b_prev * ai)
            pb.append(b_prev * ar - a_prev * ai)

        gk = [sgn * (cr * pa[k] + ci * pb[k]) for k in range(CHUNK + 1)]
        fs = jnp.concatenate(gk[:CHUNK], axis=0)
        et = jnp.concatenate(gk[1:], axis=0)
        rtt = jnp.concatenate(
            [bbr * pa[CHUNK - 1 - q] + bbi * pb[CHUNK - 1 - q] for q in range(CHUNK)], axis=0)
        bst = jnp.where(is_re, bbr, bbi)
        bst_t = jnp.concatenate([bst] * CHUNK, axis=0)
        kw = lax.dot_general(fs, bst_t, (((1,), (1,)), ((), ())),
                             precision=lax.Precision.HIGHEST, preferred_element_type=F32)
        m = jnp.zeros((cq, cq), F32)
        for q in range(CHUNK):
            if q == 0:
                shifted = kw
            else:
                shifted = jnp.concatenate(
                    [jnp.zeros((q * SSM_GROUP, cq), F32), kw[:cq - q * SSM_GROUP]], axis=0)
            m = jnp.where(lane_blk == q, shifted, m)
        m_ref[gi] = m.astype(BF16)
        rt_ref[gi] = rtt.T.astype(BF16)
        et_ref[gi] = et.astype(BF16)

        wa = jnp.broadcast_to(pa[CHUNK], (8, p2))
        rows = jnp.zeros((8, p2), F32)
        for k in range(N_SCAN_PASSES):
            rows = jnp.where(row8 == k, wa, rows)
            swapped = pltpu.roll(wa, SSM_STATE, 1)
            re2 = jnp.where(is_re8, wa, swapped)
            im2 = jnp.where(is_re8, swapped, wa)
            wb = jnp.where(is_re8, -swapped, swapped)
            wa = wa * re2 + wb * im2
        full = jnp.concatenate([rows, jnp.zeros((p2 - 8, p2), F32)], axis=0)
        w_ref[gi] = full.T


def _ssm_prep(lam_re, lam_im, log_step, b_re, b_im, c_re, c_im, groups=8):
    g, p = lam_re.shape
    p2 = 2 * p
    cq = CHUNK * SSM_GROUP
    dup = lambda v: jnp.concatenate([v, v], axis=-1)
    lre = dup(lam_re)[:, None, :]
    lim = dup(lam_im)[:, None, :]
    ls = jnp.broadcast_to(log_step[:, None, None], (g, 1, p2))
    btr = dup(jnp.swapaxes(b_re, 1, 2))
    bti = dup(jnp.swapaxes(b_im, 1, 2))
    cr = dup(c_re)
    ci = dup(c_im)
    row = pl.BlockSpec((groups, 1, p2), lambda i: (i, 0, 0))
    mat = pl.BlockSpec((groups, SSM_GROUP, p2), lambda i: (i, 0, 0))
    return pl.pallas_call(
        functools.partial(_ssm_prep_kernel, groups=groups),
        out_shape=(jax.ShapeDtypeStruct((g, cq, cq), BF16),
                   jax.ShapeDtypeStruct((g, p2, cq), BF16),
                   jax.ShapeDtypeStruct((g, cq, p2), BF16),
                   jax.ShapeDtypeStruct((g, p2, p2), F32)),
        grid=(g // groups,),
        in_specs=[row, row, row, mat, mat, mat, mat],
        out_specs=(pl.BlockSpec((groups, cq, cq), lambda i: (i, 0, 0)),
                   pl.BlockSpec((groups, p2, cq), lambda i: (i, 0, 0)),
                   pl.BlockSpec((groups, cq, p2), lambda i: (i, 0, 0)),
                   pl.BlockSpec((groups, p2, p2), lambda i: (i, 0, 0))),
        compiler_params=_params(("arbitrary",), _vmem_limit(groups * cq * cq * 4, 0, 8 << 20)),
        name="ssm_prep",
    )(lre, lim, ls, btr, bti, cr, ci)


def _ssm_kernel(*refs, groups, chunks_per_seq, n_cast):
    u_ref, m_ref, rt_ref, et_ref, w_ref, d_ref = refs[:6]
    o_ref = refs[6 + n_cast]
    for src, dst in zip(refs[6:6 + n_cast], refs[7 + n_cast:]):
        dst[...] = src[...].astype(BF16)
    cq = CHUNK * SSM_GROUP
    nc = u_ref.shape[-1]
    pos = lax.broadcasted_iota(jnp.int32, (SSM_STATE, nc), 1) % chunks_per_seq
    for gi in range(groups):
        x3 = u_ref[gi]
        x = x3.reshape(cq, nc)
        y = jnp.dot(m_ref[gi], x, preferred_element_type=F32)
        r = jnp.dot(rt_ref[gi], x, preferred_element_type=F32)
        s_re, s_im = r[:SSM_STATE], r[SSM_STATE:]
        for k in range(N_SCAN_PASSES):
            dist = 1 << k
            wr = w_ref[gi, 0:SSM_STATE, k:k + 1]
            wi = w_ref[gi, SSM_STATE:2 * SSM_STATE, k:k + 1]
            keep = pos >= dist
            sh_re = jnp.where(keep, pltpu.roll(s_re, dist, 1), 0.0)
            sh_im = jnp.where(keep, pltpu.roll(s_im, dist, 1), 0.0)
            s_re, s_im = (s_re + wr * sh_re - wi * sh_im,
                          s_im + wr * sh_im + wi * sh_re)
        first = pos >= 1
        sp = jnp.concatenate([jnp.where(first, pltpu.roll(s_re, 1, 1), 0.0),
                              jnp.where(first, pltpu.roll(s_im, 1, 1), 0.0)], axis=0)
        y = y + jnp.dot(et_ref[gi], sp.astype(BF16), preferred_element_type=F32)
        y3 = y.reshape(CHUNK, SSM_GROUP, nc) + d_ref[gi][None] * x3.astype(F32)
        o_ref[gi] = _gelu_tanh(y3).astype(BF16)


def _ssm(ut, m, rt, et, w, d, chunks_per_seq, cast_srcs, groups=4):
    g, q, j, nc = ut.shape
    cq = q * j
    p2 = rt.shape[1]
    grid = (g // groups,)
    blk = lambda *s: pl.BlockSpec((groups,) + s, lambda i: (i,) + (0,) * len(s))
    casts = [_side_cast_specs(src, src.shape[0] // grid[0], grid) for src in cast_srcs]
    cast_bytes = sum(src.size // grid[0] * 6 for src in cast_srcs)
    return pl.pallas_call(
        functools.partial(_ssm_kernel, groups=groups, chunks_per_seq=chunks_per_seq,
                          n_cast=len(cast_srcs)),
        out_shape=(jax.ShapeDtypeStruct(ut.shape, BF16), *[c[2] for c in casts]),
        grid=grid,
        in_specs=[blk(q, j, nc), blk(cq, cq), blk(p2, cq), blk(cq, p2), blk(p2, p2), blk(j, 1),
                  *[c[0] for c in casts]],
        out_specs=(blk(q, j, nc), *[c[1] for c in casts]),
        compiler_params=_params(("arbitrary",),
                                _vmem_limit(groups * cq * nc * 8 + cast_bytes, 0, 8 << 20)),
        name="ssm",
    )(ut, m, rt, et, w, d, *cast_srcs)


def _glu_kernel(y_ref, w_ref, b_ref, o_ref, y_sc, *, tn):
    n = pl.program_id(1)
    chunks = y_ref.shape[-1]
    n_slabs = y_sc.shape[0]

    @pl.when(n == 0)
    def _():
        for q in range(CHUNK):
            yq = y_ref[:, q]
            yq = yq.reshape(yq.shape[0] * yq.shape[1], chunks).astype(F32).T
            for s in range(n_slabs):
                y_sc[s, pl.ds(q, chunks, stride=CHUNK), :] = yq[:, s * LANES:(s + 1) * LANES]

    y = jnp.concatenate([y_sc[s] for s in range(n_slabs)], axis=1).astype(BF16)
    z = jnp.dot(y, w_ref[...], preferred_element_type=F32) + b_ref[...]
    per_tile = tn // LANES
    yn = jnp.concatenate([y_sc[n * per_tile + s] for s in range(per_tile)], axis=1)
    o_ref[...] = (yn * _sigmoid(z)).astype(BF16)


def _glu(yt, w, b, seq_len, tn=512):
    g, q, j, nc = yt.shape
    d = g * j
    chunks = seq_len // q
    t = nc * q
    return pl.pallas_call(
        functools.partial(_glu_kernel, tn=tn),
        out_shape=jax.ShapeDtypeStruct((t, d), BF16),
        grid=(nc // chunks, d // tn),
        in_specs=[pl.BlockSpec((g, q, j, chunks), lambda bi, n: (0, 0, 0, bi)),
                  pl.BlockSpec((d, tn), lambda bi, n: (0, n)),
                  pl.BlockSpec((1, tn), lambda bi, n: (0, n))],
        out_specs=pl.BlockSpec((seq_len, tn), lambda bi, n: (bi, n)),
        scratch_shapes=[pltpu.VMEM((d // LANES, seq_len, LANES), F32)],
        compiler_params=_params(("arbitrary", "arbitrary"),
                                _vmem_limit((seq_len * d + d * tn + seq_len * tn) * 2,
                                            seq_len * d * 4, seq_len * d * 2 + seq_len * tn * 12)),
        name="glu",
    )(yt, w, b)


def _pool_kernel(u_ref, w_ref, b_ref, s_ref, o_ref, ext_sc, *, tiles_per_seq, halo):
    i = pl.program_id(0)
    tm = u_ref.shape[0]
    gp = w_ref.shape[1]

    @pl.when(i % tiles_per_seq == 0)
    def _():
        ext_sc[0:halo] = jnp.zeros((halo, ext_sc.shape[1]), BF16)

    @pl.when(i % tiles_per_seq != 0)
    def _():
        ext_sc[0:halo] = ext_sc[tm:tm + halo]

    ext_sc[halo:halo + tm] = u_ref[...]

    t_loc = lax.broadcasted_iota(jnp.int32, (tm, tm + halo), 0)
    s_loc = lax.broadcasted_iota(jnp.int32, (tm, tm + halo), 1)
    lag = t_loc + halo - s_loc
    t_seq = (i % tiles_per_seq) * tm + lax.broadcasted_iota(jnp.int32, (tm, 1), 0)
    for gi, win in enumerate(POOL_WINDOWS):
        band = jnp.where((lag >= 0) & (lag < win), 1.0, 0.0).astype(BF16)
        cols = slice(gi * gp, (gi + 1) * gp)
        wsum = jnp.dot(band, ext_sc[:, cols], preferred_element_type=F32)
        cnt = jnp.minimum(t_seq + 1, win).astype(F32)
        z = wsum / cnt - u_ref[:, cols].astype(F32)
        z = jnp.dot(z.astype(BF16), w_ref[gi], preferred_element_type=F32) + b_ref[gi]
        o_ref[:, cols] = (z * s_ref[:, cols]).astype(BF16)


def _pool(prest, w, b, scale, seq_len, d_pool, tm=512, halo=128):
    t = prest.shape[0]
    ng, gp, _ = w.shape
    return pl.pallas_call(
        functools.partial(_pool_kernel, tiles_per_seq=seq_len // tm, halo=halo),
        out_shape=jax.ShapeDtypeStruct((t, d_pool), BF16),
        grid=(t // tm,),
        in_specs=[pl.BlockSpec((tm, d_pool), lambda i: (i, 0)),
                  pl.BlockSpec((ng, gp, gp), lambda i: (0, 0, 0)),
                  pl.BlockSpec((ng, 1, gp), lambda i: (0, 0, 0)),
                  pl.BlockSpec((1, d_pool), lambda i: (0, 0))],
        out_specs=pl.BlockSpec((tm, d_pool), lambda i: (i, 0)),
        scratch_shapes=[pltpu.VMEM((tm + halo, d_pool), BF16)],
        compiler_params=_params(("arbitrary",),
                                _vmem_limit(tm * d_pool * 4 + ng * gp * gp * 2,
                                            (tm + halo) * d_pool * 2, 8 << 20)),
        name="pool",
    )(prest, w, b, scale)


def _mix_kernel(ys_ref, po_ref, gs_ref, gp_ref, x_ref, wbs_ref, wbp_ref, wo_ref, g_ref, cast_ref,
                o_ref, cast_o_ref, x_sc, *, sub):
    n = pl.program_id(1)
    cast_o_ref[...] = cast_ref[...].astype(BF16)
    tn = x_ref.shape[1]
    x_sc[:, pl.ds(pl.multiple_of(n * tn, tn), tn)] = x_ref[...]

    @pl.when(n == 0)
    def _():
        o_ref[...] = jnp.zeros_like(o_ref)

    subs = [slice(s * sub, (s + 1) * sub) for s in range(wo_ref.shape[0] // sub)]
    branch = [(jnp.dot(ys_ref[...], wbs_ref[:, cols], preferred_element_type=F32),
               jnp.dot(po_ref[...], wbp_ref[:, cols], preferred_element_type=F32))
              for cols in subs]
    merged = [(gs_ref[:, cols].astype(F32) * ms + gp_ref[:, cols].astype(F32) * mp).astype(BF16)
              for cols, (ms, mp) in zip(subs, branch)]
    o_ref[...] += jnp.dot(jnp.concatenate(merged, axis=1), wo_ref[...],
                          preferred_element_type=F32)

    @pl.when(n == pl.num_programs(1) - 1)
    def _():
        def rows(sl):
            o_ref[sl, :] = x_sc[sl, :] + _rms(o_ref[sl, :], g_ref[...])
        _for_row_blocks(o_ref.shape[0], rows)


def _mix(ys, po, prest, x2, wbs, wbp, wo, g, gate_col0, cast_src, cast_rows,
         tm=512, tn=512, sub=256):
    t, d = x2.shape
    ds_ = ys.shape[1]
    dp = po.shape[1]
    c0 = gate_col0 // tn
    c1 = (gate_col0 + d) // tn
    resident = pl.Buffered(1)
    grid = (t // tm, d // tn)
    cast_in, cast_out, cast_shape = _side_cast_specs(cast_src, cast_rows, grid)
    cast_bytes = cast_rows * cast_src.shape[1] * 6
    return pl.pallas_call(
        functools.partial(_mix_kernel, sub=sub),
        out_shape=(jax.ShapeDtypeStruct((t, d), F32), cast_shape),
        grid=grid,
        in_specs=[pl.BlockSpec((tm, ds_), lambda i, n: (i, 0)),
                  pl.BlockSpec((tm, dp), lambda i, n: (i, 0)),
                  pl.BlockSpec((tm, tn), lambda i, n: (i, c0 + n)),
                  pl.BlockSpec((tm, tn), lambda i, n: (i, c1 + n)),
                  pl.BlockSpec((tm, tn), lambda i, n: (i, n)),
                  pl.BlockSpec((ds_, tn), lambda i, n: (0, n)),
                  pl.BlockSpec((dp, tn), lambda i, n: (0, n)),
                  pl.BlockSpec((tn, d), lambda i, n: (n, 0)),
                  pl.BlockSpec((1, d), lambda i, n: (0, 0)),
                  cast_in],
        out_specs=(pl.BlockSpec((tm, d), lambda i, n: (i, 0), pipeline_mode=resident), cast_out),
        scratch_shapes=[pltpu.VMEM((tm, d), F32)],
        compiler_params=_params(
            ("arbitrary", "arbitrary"),
            _vmem_limit(tm * (ds_ + dp) * 2 + tm * tn * 8 + (ds_ + dp) * tn * 2 + tn * d * 2
                        + cast_bytes, tm * d * 8, tm * tn * 16)),
        name="mix",
    )(ys, po, prest, prest, x2, wbs, wbp, wo, g, cast_src)


def _ffn_kernel(*refs, n_sub, n_tiles, n_up_steps, tiles_per_seq, keep, up_row_blocks):
    h_ref, g3_ref = refs[:2]
    groups = [refs[2 + k * n_sub:2 + (k + 1) * n_sub] for k in range(7)]
    wa_refs, wb_refs, cwa_refs, cwb_refs, cba_refs, cbb_refs, wd_refs = groups
    g4_ref, o_ref, c_sc, carry_sc, up_sc = refs[2 + 7 * n_sub:]
    i = pl.program_id(0)
    f = pl.program_id(1)
    tm = h_ref.shape[0]

    @pl.when(f == 0)
    def _():
        def rows(sl):
            c_sc[sl, :] = _rms(h_ref[sl, :], g3_ref[...]).astype(BF16)
            o_ref[sl, :] = jnp.zeros((NORM_ROWS, o_ref.shape[1]), F32)
        _for_row_blocks(tm, rows)

    seq_start = i % tiles_per_seq == 0

    rb = tm // up_row_blocks
    row_in_head = lax.broadcasted_iota(jnp.int32, (keep, wd_refs[0].shape[0]), 0)
    n_slots = 2 * n_sub

    def up_products(n_act):
        base = (f % 2) * n_slots
        for s in range(n_act):
            for half, w_refs in enumerate((wa_refs, wb_refs)):
                w = w_refs[s][...]
                for r0 in range(0, tm, rb):
                    up_sc[base + 2 * s + half, r0:r0 + rb] = jnp.dot(
                        c_sc[r0:r0 + rb, :], w, preferred_element_type=F32)

    def conv(slot, half, tile, cw_ref, cb_ref):
        prev = jnp.where(seq_start, 0.0, carry_sc[half, tile])
        carry_sc[half, tile] = up_sc[slot, tm - keep:tm]
        cw = cw_ref[...]
        outs = []
        for r0 in range(0, tm, rb):
            up = up_sc[slot, r0:r0 + rb]
            z = cw[0:1] * up
            zp = cw[0:1] * prev
            for k in range(1, CONV_WIDTH):
                r = pltpu.roll(z, 1, 0)
                head = jnp.where(row_in_head == 0, zp[keep - 1:keep], r[0:keep])
                z = cw[k:k + 1] * up + jnp.concatenate([head, r[keep:]], axis=0)
                if k + 1 < CONV_WIDTH:
                    zp = cw[k:k + 1] * prev + pltpu.roll(zp, 1, 0)
            outs.append(z + cb_ref[...])
            prev = up[rb - keep:]
        return outs

    def down_products(n_act):
        base = ((f + 1) % 2) * n_slots
        fms = []
        for s in range(n_act):
            tile = (f - 1) * n_sub + s
            ua = conv(base + 2 * s, 0, tile, cwa_refs[s], cba_refs[s])
            ub = conv(base + 2 * s + 1, 1, tile, cwb_refs[s], cbb_refs[s])
            fms.append(jnp.concatenate(
                [(_gelu_tanh(a) * b).astype(BF16) for a, b in zip(ua, ub)], axis=0))
        fm = jnp.concatenate(fms, axis=1)
        wd = jnp.concatenate([wd_refs[s][...] for s in range(n_act)], axis=0)
        o_ref[...] += jnp.dot(fm, wd, preferred_element_type=F32)

    tail = n_tiles - (n_up_steps - 1) * n_sub
    assert n_up_steps >= 2

    def run(n_up, n_down):
        def body():
            if n_up:
                up_products(n_up)
            if n_down:
                down_products(n_down)
        return body

    pl.when(f == 0)(run(n_sub, 0))
    pl.when((f > 0) & (f < n_up_steps - 1))(run(n_sub, n_sub))
    pl.when(f == n_up_steps - 1)(run(tail, n_sub))
    pl.when(f == n_up_steps)(run(0, tail))

    @pl.when(f == pl.num_programs(1) - 1)
    def _():
        def rows(sl):
            o_ref[sl, :] = h_ref[sl, :] + _rms(o_ref[sl, :], g4_ref[...])
        _for_row_blocks(tm, rows)


def _ffn(h1, g3, w_up, conv_w, conv_b, w_down, g4, seq_len, tm=512, sub=256, n_sub=2, keep=8,
         up_row_blocks=2):
    t, d = h1.shape
    d_ff = w_down.shape[0]
    n_tiles = d_ff // sub
    nf = pl.cdiv(n_tiles, n_sub)
    resident = pl.Buffered(1)

    def tile_of(s, lag):
        return lambda f: jnp.minimum(jnp.maximum(f - lag, 0) * n_sub + s, n_tiles - 1)

    def per_sub(shape, index, lag):
        return [pl.BlockSpec(shape, functools.partial(index, tile_of(s, lag)))
                for s in range(n_sub)]

    col_a = lambda tile, i, f: (0, tile(f))
    col_b = lambda tile, i, f: (0, n_tiles + tile(f))
    row_d = lambda tile, i, f: (tile(f), 0)
    in_specs = ([pl.BlockSpec((tm, d), lambda i, f: (i, 0), pipeline_mode=resident),
                 pl.BlockSpec((1, d), lambda i, f: (0, 0))]
                + per_sub((d, sub), col_a, 0) + per_sub((d, sub), col_b, 0)
                + per_sub((CONV_WIDTH, sub), col_a, 1) + per_sub((CONV_WIDTH, sub), col_b, 1)
                + per_sub((1, sub), col_a, 1) + per_sub((1, sub), col_b, 1)
                + per_sub((sub, d), row_d, 1)
                + [pl.BlockSpec((1, d), lambda i, f: (0, 0))])
    operands = ([h1, g3] + [w_up] * (2 * n_sub) + [conv_w] * (2 * n_sub)
                + [conv_b] * (2 * n_sub) + [w_down] * n_sub + [g4])
    tf = sub * n_sub
    return pl.pallas_call(
        functools.partial(_ffn_kernel, n_sub=n_sub, n_tiles=n_tiles, n_up_steps=nf,
                          tiles_per_seq=seq_len // tm, keep=keep, up_row_blocks=up_row_blocks),
        out_shape=jax.ShapeDtypeStruct((t, d), F32),
        grid=(t // tm, nf + 1),
        in_specs=in_specs,
        out_specs=pl.BlockSpec((tm, d), lambda i, f: (i, 0), pipeline_mode=resident),
        scratch_shapes=[pltpu.VMEM((tm, d), BF16),
                        pltpu.VMEM((2, nf * n_sub, keep, sub), F32),
                        pltpu.VMEM((2 * 2 * n_sub, tm, sub), F32)],
        compiler_params=_params(
            ("arbitrary", "arbitrary"),
            _vmem_limit(d * tf * 4 + tf * d * 2,
                        tm * d * 8 + tm * d * 2 + 2 * nf * tf * keep * 4 + 4 * tm * tf * 4,
                        tm * tf * 32)),
        name="ffn",
    )(*operands)


def kernel(x, norm_pre_mix, w_in, ssm_lambda_re, ssm_lambda_im, ssm_log_step, ssm_b_re, ssm_b_im,
           ssm_c_re, ssm_c_im, ssm_d, ssm_glu_w, ssm_glu_b, pool_w, pool_b, pool_scale,
           w_branch_ssm, w_branch_pool, w_out, norm_post_mix, norm_pre_ffn, w_up, ffn_conv_w,
           ffn_conv_b, w_down, norm_post_ffn):
    bsz, seq_len, d = x.shape
    depth = w_in.shape[0]
    d_ssm = ssm_d.shape[1]
    d_pool = pool_scale.shape[1]
    n_groups = d_ssm // SSM_GROUP
    assert seq_len // CHUNK == 1 << N_SCAN_PASSES
    t = bsz * seq_len
    h = x.reshape(t, d)
    row = lambda v: v.reshape(1, -1)
    for i in range(depth):
        a = _norm(h, row(norm_pre_mix[i]))
        prest, w_up_b = _inproj(a, w_in[i], d_ssm, d_pool, w_up[i], 32)
        ut = _inproj_ssm(a, w_in[i], d_ssm, seq_len)
        m, rt, et, wsc = _ssm_prep(ssm_lambda_re[i], ssm_lambda_im[i], ssm_log_step[i],
                                   ssm_b_re[i], ssm_b_im[i], ssm_c_re[i], ssm_c_im[i])
        yt, w_out_b, w_bs_b, w_bp_b = _ssm(
            ut, m, rt, et, wsc, ssm_d[i].reshape(n_groups, SSM_GROUP, 1), seq_len // CHUNK,
            [w_out[i], w_branch_ssm[i], w_branch_pool[i]])
        ys = _glu(yt, ssm_glu_w[i].astype(BF16), row(ssm_glu_b[i]), seq_len)
        po = _pool(prest, pool_w[i].astype(BF16), pool_b[i][:, None, :], row(pool_scale[i]),
                   seq_len, d_pool)
        h, w_down_b = _mix(ys, po, prest, h, w_bs_b, w_bp_b, w_out_b,
                           row(norm_post_mix[i]), d_pool, w_down[i], 128)
        h = _ffn(h, row(norm_pre_ffn[i]), w_up_b, ffn_conv_w[i], row(ffn_conv_b[i]),
                 w_down_b, row(norm_post_ffn[i]), seq_len)
    return h.reshape(bsz, seq_len, d)
```

```python
import functools
import math

import jax
import jax.numpy as jnp
from jax import lax
from jax.experimental import pallas as pl
from jax.experimental.pallas import tpu as pltpu

F32 = jnp.float32
BF16 = jnp.bfloat16

EPS = 1e-6
MIN_NEG_REAL = -1e-4
SSM_GROUP = 16
SSM_STATE = 64
POOL_WINDOWS = (2, 4, 8, 16)
CONV_WIDTH = 3
CHUNK = 16
N_SCAN_PASSES = 7
LANES = 128
FFN_SUB = 256

V7X_VMEM_BYTES = 64 * 1024 * 1024
VMEM_CEILING = V7X_VMEM_BYTES - 6 * 1024 * 1024


def _vmem_limit(pipelined_bytes, scratch_bytes, temp_bytes):
    return min(2 * pipelined_bytes + scratch_bytes + temp_bytes + (4 << 20), VMEM_CEILING)


def _params(semantics, vmem):
    return pltpu.CompilerParams(dimension_semantics=semantics, vmem_limit_bytes=vmem)


def _rms(xf, g):
    ms = jnp.mean(xf * xf, axis=-1, keepdims=True)
    return xf * lax.rsqrt(ms + EPS) * g


NORM_ROWS = 32


def _for_row_blocks(n_rows, body):
    def step(r, carry):
        body(pl.ds(pl.multiple_of(r * NORM_ROWS, NORM_ROWS), NORM_ROWS))
        return carry
    lax.fori_loop(0, n_rows // NORM_ROWS, step, 0, unroll=2)


def _sigmoid(x):
    return 1.0 / (1.0 + jnp.exp(-x))


def _gelu_tanh(x):
    c = math.sqrt(2.0 / math.pi)
    return 0.5 * x * (1.0 + jnp.tanh(c * (x + 0.044715 * (x * x * x))))


def _norm_kernel(x_ref, g_ref, o_ref):
    def rows(sl):
        o_ref[sl, :] = _rms(x_ref[sl, :], g_ref[...]).astype(BF16)
    _for_row_blocks(x_ref.shape[0], rows)


def _norm(x2, g, tm=512):
    t, d = x2.shape
    return pl.pallas_call(
        _norm_kernel,
        out_shape=jax.ShapeDtypeStruct((t, d), BF16),
        grid=(t // tm,),
        in_specs=[pl.BlockSpec((tm, d), lambda i: (i, 0)),
                  pl.BlockSpec((1, d), lambda i: (0, 0))],
        out_specs=pl.BlockSpec((tm, d), lambda i: (i, 0)),
        compiler_params=_params(("arbitrary",), _vmem_limit(tm * d * 6, 0, tm * d * 8)),
        name="norm",
    )(x2, g)


def _side_cast_specs(src, rows, grid, col_tile=None):
    n_blocks = src.shape[0] // rows
    assert src.shape[0] % rows == 0 and n_blocks <= math.prod(grid)

    def block(*g):
        step = g[0]
        for k in range(1, len(grid)):
            step = step * grid[k] + g[k]
        return jnp.minimum(step, n_blocks - 1)

    in_spec = pl.BlockSpec((rows, src.shape[1]), lambda *g: (block(*g), 0))
    if col_tile is None:
        return in_spec, in_spec, jax.ShapeDtypeStruct(src.shape, BF16)
    n_col = src.shape[1] // col_tile
    out_spec = pl.BlockSpec((n_col, rows, col_tile), lambda *g: (0, block(*g), 0))
    return in_spec, out_spec, jax.ShapeDtypeStruct((n_col, src.shape[0], col_tile), BF16)


def _side_cast(src_ref, dst_ref):
    if len(dst_ref.shape) == 2:
        dst_ref[...] = src_ref[...].astype(BF16)
    else:
        col_tile = dst_ref.shape[2]
        for t in range(dst_ref.shape[0]):
            dst_ref[t] = src_ref[:, t * col_tile:(t + 1) * col_tile].astype(BF16)


def _inproj_kernel(a_ref, w_ref, cast_ref, o_ref, cast_o_ref, *, n_plain, row_blocks):
    gate = pl.program_id(1) >= n_plain
    w = w_ref[...].astype(BF16)
    rows = a_ref.shape[0] // row_blocks
    for r0 in range(0, a_ref.shape[0], rows):
        p = jnp.dot(a_ref[r0:r0 + rows, :], w, preferred_element_type=F32)
        o_ref[r0:r0 + rows, :] = jnp.where(gate, _sigmoid(p), p).astype(BF16)
    _side_cast(cast_ref, cast_o_ref)


def _inproj(a, w, col0, n_plain_cols, cast_src, cast_rows, cast_col_tile,
            tm=1024, tn=512, row_blocks=4):
    t, d = a.shape
    n = w.shape[1] - col0
    c0 = col0 // tn
    grid = (t // tm, n // tn)
    cast_in, cast_out, cast_shape = _side_cast_specs(cast_src, cast_rows, grid, cast_col_tile)
    cast_bytes = cast_rows * cast_src.shape[1] * 6
    return pl.pallas_call(
        functools.partial(_inproj_kernel, n_plain=n_plain_cols // tn, row_blocks=row_blocks),
        out_shape=(jax.ShapeDtypeStruct((t, n), BF16), cast_shape),
        grid=grid,
        in_specs=[pl.BlockSpec((tm, d), lambda i, j: (i, 0)),
                  pl.BlockSpec((d, tn), lambda i, j: (0, c0 + j)),
                  cast_in],
        out_specs=(pl.BlockSpec((tm, tn), lambda i, j: (i, j)), cast_out),
        compiler_params=_params(("arbitrary", "arbitrary"),
                                _vmem_limit(tm * d * 2 + d * tn * 4 + tm * tn * 2 + cast_bytes, 0,
                                            d * tn * 2 + tm * tn * 8)),
        name="inproj",
    )(a, w, cast_src)


def _inproj_ssm_kernel(a_ref, w_ref, o_ref, p_sc, *, row_blocks):
    chunks = o_ref.shape[-1]
    n_slabs = p_sc.shape[0]
    w = w_ref[...].astype(BF16)
    rows = a_ref.shape[0] // row_blocks
    for r0 in range(0, a_ref.shape[0], rows):
        p = jnp.dot(a_ref[r0:r0 + rows, :], w, preferred_element_type=F32)
        for s in range(n_slabs):
            p_sc[s, r0:r0 + rows] = p[:, s * LANES:(s + 1) * LANES]
    gps = LANES // SSM_GROUP
    for q in range(CHUNK):
        for s in range(n_slabs):
            blk = p_sc[s, pl.ds(q, chunks, stride=CHUNK), :]
            o_ref[s * gps:(s + 1) * gps, q] = (
                blk.T.reshape(gps, SSM_GROUP, chunks).astype(BF16))


def _inproj_ssm(a, w, n, seq_len, tn=256, row_blocks=8):
    t, d = a.shape
    g = n // SSM_GROUP
    chunks = seq_len // CHUNK
    return pl.pallas_call(
        functools.partial(_inproj_ssm_kernel, row_blocks=row_blocks),
        out_shape=jax.ShapeDtypeStruct((g, CHUNK, SSM_GROUP, t // CHUNK), BF16),
        grid=(t // seq_len, n // tn),
        in_specs=[pl.BlockSpec((seq_len, d), lambda bi, j: (bi, 0)),
                  pl.BlockSpec((d, tn), lambda bi, j: (0, j))],
        out_specs=pl.BlockSpec((tn // SSM_GROUP, CHUNK, SSM_GROUP, chunks),
                               lambda bi, j: (j, 0, 0, bi)),
        scratch_shapes=[pltpu.VMEM((tn // LANES, seq_len, LANES), F32)],
        compiler_params=_params(("arbitrary", "arbitrary"),
                                _vmem_limit(seq_len * d * 2 + d * tn * 4 + tn * seq_len * 2,
                                            seq_len * tn * 4, d * tn * 2 + seq_len * tn * 16)),
        name="inproj_ssm",
    )(a, w)


def _ssm_prep_kernel(lre_ref, lim_ref, ls_ref, btr_ref, bti_ref, cr_ref, ci_ref,
                     m_ref, rt_ref, et_ref, w_ref, *, groups):
    p2 = 2 * SSM_STATE
    cq = CHUNK * SSM_GROUP
    lane = lax.broadcasted_iota(jnp.int32, (1, p2), 1)
    is_re = lane < SSM_STATE
    sgn = jnp.where(is_re, 1.0, -1.0).astype(F32)
    lane_blk = lax.broadcasted_iota(jnp.int32, (cq, cq), 1) // SSM_GROUP
    row8 = lax.broadcasted_iota(jnp.int32, (8, p2), 0)
    is_re8 = lax.broadcasted_iota(jnp.int32, (8, p2), 1) < SSM_STATE

    for gi in range(groups):
        lr = jnp.minimum(lre_ref[gi], MIN_NEG_REAL)
        li = lim_ref[gi]
        dt = jnp.exp(ls_ref[gi])
        mag = jnp.exp(lr * dt)
        ang = li * dt
        ar = mag * jnp.cos(ang)
        ai = mag * jnp.sin(ang)
        nr = ar - 1.0
        ni = ai
        den = lr * lr + li * li
        f_re = (nr * lr + ni * li) / den
        f_im = (ni * lr - nr * li) / den
        btr = btr_ref[gi]
        bti = bti_ref[gi]
        bbr = f_re * btr - f_im * bti
        bbi = f_re * bti + f_im * btr
        cr = cr_ref[gi]
        ci = ci_ref[gi]

        pa = [jnp.where(is_re, 1.0, 0.0).astype(F32)]
        pb = [jnp.where(is_re, 0.0, 1.0).astype(F32)]
        for _ in range(CHUNK):
            a_prev, b_prev = pa[-1], pb[-1]
            pa.append(a_prev * ar + b_prev * ai)
            pb.append(b_prev * ar - a_prev * ai)

        gk = [sgn * (cr * pa[k] + ci * pb[k]) for k in range(CHUNK + 1)]
        fs = jnp.concatenate(gk[:CHUNK], axis=0)
        et = jnp.concatenate(gk[1:], axis=0)
        rtt = jnp.concatenate(
            [bbr * pa[CHUNK - 1 - q] + bbi * pb[CHUNK - 1 - q] for q in range(CHUNK)], axis=0)
        bst = jnp.where(is_re, bbr, bbi)
        bst_t = jnp.concatenate([bst] * CHUNK, axis=0)
        kw = lax.dot_general(fs, bst_t, (((1,), (1,)), ((), ())),
                             precision=lax.Precision.HIGHEST, preferred_element_type=F32)
        m = jnp.zeros((cq, cq), F32)
        for q in range(CHUNK):
            if q == 0:
                shifted = kw
            else:
                shifted = jnp.concatenate(
                    [jnp.zeros((q * SSM_GROUP, cq), F32), kw[:cq - q * SSM_GROUP]], axis=0)
            m = jnp.where(lane_blk == q, shifted, m)
        m_ref[gi] = m.astype(BF16)
        rt_ref[gi] = rtt.T.astype(BF16)
        et_ref[gi] = et.astype(BF16)

        wa = jnp.broadcast_to(pa[CHUNK], (8, p2))
        rows = jnp.zeros((8, p2), F32)
        for k in range(N_SCAN_PASSES):
            rows = jnp.where(row8 == k, wa, rows)
            swapped = pltpu.roll(wa, SSM_STATE, 1)
            re2 = jnp.where(is_re8, wa, swapped)
            im2 = jnp.where(is_re8, swapped, wa)
            wb = jnp.where(is_re8, -swapped, swapped)
            wa = wa * re2 + wb * im2
        full = jnp.concatenate([rows, jnp.zeros((p2 - 8, p2), F32)], axis=0)
        w_ref[gi] = full.T


def _ssm_prep(lam_re, lam_im, log_step, b_re, b_im, c_re, c_im, groups=8):
    g, p = lam_re.shape
    p2 = 2 * p
    cq = CHUNK * SSM_GROUP
    dup = lambda v: jnp.concatenate([v, v], axis=-1)
    lre = dup(lam_re)[:, None, :]
    lim = dup(lam_im)[:, None, :]
    ls = jnp.broadcast_to(log_step[:, None, None], (g, 1, p2))
    btr = dup(jnp.swapaxes(b_re, 1, 2))
    bti = dup(jnp.swapaxes(b_im, 1, 2))
    cr = dup(c_re)
    ci = dup(c_im)
    row = pl.BlockSpec((groups, 1, p2), lambda i: (i, 0, 0))
    mat = pl.BlockSpec((groups, SSM_GROUP, p2), lambda i: (i, 0, 0))
    return pl.pallas_call(
        functools.partial(_ssm_prep_kernel, groups=groups),
        out_shape=(jax.ShapeDtypeStruct((g, cq, cq), BF16),
                   jax.ShapeDtypeStruct((g, p2, cq), BF16),
                   jax.ShapeDtypeStruct((g, cq, p2), BF16),
                   jax.ShapeDtypeStruct((g, p2, p2), F32)),
        grid=(g // groups,),
        in_specs=[row, row, row, mat, mat, mat, mat],
        out_specs=(pl.BlockSpec((groups, cq, cq), lambda i: (i, 0, 0)),
                   pl.BlockSpec((groups, p2, cq), lambda i: (i, 0, 0)),
                   pl.BlockSpec((groups, cq, p2), lambda i: (i, 0, 0)),
                   pl.BlockSpec((groups, p2, p2), lambda i: (i, 0, 0))),
        compiler_params=_params(("arbitrary",), _vmem_limit(groups * cq * cq * 4, 0, 8 << 20)),
        name="ssm_prep",
    )(lre, lim, ls, btr, bti, cr, ci)


def _ssm_kernel(*refs, groups, chunks_per_seq, n_cast):
    u_ref, m_ref, rt_ref, et_ref, w_ref, d_ref = refs[:6]
    o_ref = refs[6 + n_cast]
    for src, dst in zip(refs[6:6 + n_cast], refs[7 + n_cast:]):
        dst[...] = src[...].astype(BF16)
    cq = CHUNK * SSM_GROUP
    nc = u_ref.shape[-1]
    pos = lax.broadcasted_iota(jnp.int32, (SSM_STATE, nc), 1) % chunks_per_seq
    for gi in range(groups):
        x3 = u_ref[gi]
        x = x3.reshape(cq, nc)
        y = jnp.dot(m_ref[gi], x, preferred_element_type=F32)
        r = jnp.dot(rt_ref[gi], x, preferred_element_type=F32)
        s_re, s_im = r[:SSM_STATE], r[SSM_STATE:]
        for k in range(N_SCAN_PASSES):
            dist = 1 << k
            wr = w_ref[gi, 0:SSM_STATE, k:k + 1]
            wi = w_ref[gi, SSM_STATE:2 * SSM_STATE, k:k + 1]
            keep = pos >= dist
            sh_re = jnp.where(keep, pltpu.roll(s_re, dist, 1), 0.0)
            sh_im = jnp.where(keep, pltpu.roll(s_im, dist, 1), 0.0)
            s_re, s_im = (s_re + wr * sh_re - wi * sh_im,
                          s_im + wr * sh_im + wi * sh_re)
        first = pos >= 1
        sp = jnp.concatenate([jnp.where(first, pltpu.roll(s_re, 1, 1), 0.0),
                              jnp.where(first, pltpu.roll(s_im, 1, 1), 0.0)], axis=0)
        y = y + jnp.dot(et_ref[gi], sp.astype(BF16), preferred_element_type=F32)
        y3 = y.reshape(CHUNK, SSM_GROUP, nc) + d_ref[gi][None] * x3.astype(F32)
        o_ref[gi] = _gelu_tanh(y3).astype(BF16)


def _ssm(ut, m, rt, et, w, d, chunks_per_seq, cast_srcs, groups=4):
    g, q, j, nc = ut.shape
    cq = q * j
    p2 = rt.shape[1]
    grid = (g // groups,)
    blk = lambda *s: pl.BlockSpec((groups,) + s, lambda i: (i,) + (0,) * len(s))
    casts = [_side_cast_specs(src, src.shape[0] // grid[0], grid) for src in cast_srcs]
    cast_bytes = sum(src.size // grid[0] * 6 for src in cast_srcs)
    return pl.pallas_call(
        functools.partial(_ssm_kernel, groups=groups, chunks_per_seq=chunks_per_seq,
                          n_cast=len(cast_srcs)),
        out_shape=(jax.ShapeDtypeStruct(ut.shape, BF16), *[c[2] for c in casts]),
        grid=grid,
        in_specs=[blk(q, j, nc), blk(cq, cq), blk(p2, cq), blk(cq, p2), blk(p2, p2), blk(j, 1),
                  *[c[0] for c in casts]],
        out_specs=(blk(q, j, nc), *[c[1] for c in casts]),
        compiler_params=_params(("arbitrary",),
                                _vmem_limit(groups * cq * nc * 8 + cast_bytes, 0, 8 << 20)),
        name="ssm",
    )(ut, m, rt, et, w, d, *cast_srcs)


def _glu_kernel(y_ref, w_ref, b_ref, o_ref, y_sc, *, tn):
    n = pl.program_id(1)
    chunks = y_ref.shape[-1]
    n_slabs = y_sc.shape[0]

    @pl.when(n == 0)
    def _():
        for q in range(CHUNK):
            yq = y_ref[:, q]
            yq = yq.reshape(yq.shape[0] * yq.shape[1], chunks).astype(F32).T
            for s in range(n_slabs):
                y_sc[s, pl.ds(q, chunks, stride=CHUNK), :] = yq[:, s * LANES:(s + 1) * LANES]

    y = jnp.concatenate([y_sc[s] for s in range(n_slabs)], axis=1).astype(BF16)
    z = jnp.dot(y, w_ref[...], preferred_element_type=F32) + b_ref[...]
    per_tile = tn // LANES
    yn = jnp.concatenate([y_sc[n * per_tile + s] for s in range(per_tile)], axis=1)
    o_ref[...] = (yn * _sigmoid(z)).astype(BF16)


def _glu(yt, w, b, seq_len, tn=512):
    g, q, j, nc = yt.shape
    d = g * j
    chunks = seq_len // q
    t = nc * q
    return pl.pallas_call(
        functools.partial(_glu_kernel, tn=tn),
        out_shape=jax.ShapeDtypeStruct((t, d), BF16),
        grid=(nc // chunks, d // tn),
        in_specs=[pl.BlockSpec((g, q, j, chunks), lambda bi, n: (0, 0, 0, bi)),
                  pl.BlockSpec((d, tn), lambda bi, n: (0, n)),
                  pl.BlockSpec((1, tn), lambda bi, n: (0, n))],
        out_specs=pl.BlockSpec((seq_len, tn), lambda bi, n: (bi, n)),
        scratch_shapes=[pltpu.VMEM((d // LANES, seq_len, LANES), F32)],
        compiler_params=_params(("arbitrary", "arbitrary"),
                                _vmem_limit((seq_len * d + d * tn + seq_len * tn) * 2,
                                            seq_len * d * 4, seq_len * d * 2 + seq_len * tn * 12)),
        name="glu",
    )(yt, w, b)


def _pool_kernel(u_ref, w_ref, b_ref, s_ref, o_ref, ext_sc, *, tiles_per_seq, halo):
    i = pl.program_id(0)
    tm = u_ref.shape[0]
    gp = w_ref.shape[1]

    @pl.when(i % tiles_per_seq == 0)
    def _():
        ext_sc[0:halo] = jnp.zeros((halo, ext_sc.shape[1]), BF16)

    @pl.when(i % tiles_per_seq != 0)
    def _():
        ext_sc[0:halo] = ext_sc[tm:tm + halo]

    ext_sc[halo:halo + tm] = u_ref[...]

    t_loc = lax.broadcasted_iota(jnp.int32, (tm, tm + halo), 0)
    s_loc = lax.broadcasted_iota(jnp.int32, (tm, tm + halo), 1)
    lag = t_loc + halo - s_loc
    t_seq = (i % tiles_per_seq) * tm + lax.broadcasted_iota(jnp.int32, (tm, 1), 0)
    for gi, win in enumerate(POOL_WINDOWS):
        band = jnp.where((lag >= 0) & (lag < win), 1.0, 0.0).astype(BF16)
        cols = slice(gi * gp, (gi + 1) * gp)
        wsum = jnp.dot(band, ext_sc[:, cols], preferred_element_type=F32)
        cnt = jnp.minimum(t_seq + 1, win).astype(F32)
        z = wsum / cnt - u_ref[:, cols].astype(F32)
        z = jnp.dot(z.astype(BF16), w_ref[gi], preferred_element_type=F32) + b_ref[gi]
        o_ref[:, cols] = (z * s_ref[:, cols]).astype(BF16)


def _pool(prest, w, b, scale, seq_len, d_pool, tm=512, halo=128):
    t = prest.shape[0]
    ng, gp, _ = w.shape
    return pl.pallas_call(
        functools.partial(_pool_kernel, tiles_per_seq=seq_len // tm, halo=halo),
        out_shape=jax.ShapeDtypeStruct((t, d_pool), BF16),
        grid=(t // tm,),
        in_specs=[pl.BlockSpec((tm, d_pool), lambda i: (i, 0)),
                  pl.BlockSpec((ng, gp, gp), lambda i: (0, 0, 0)),
                  pl.BlockSpec((ng, 1, gp), lambda i: (0, 0, 0)),
                  pl.BlockSpec((1, d_pool), lambda i: (0, 0))],
        out_specs=pl.BlockSpec((tm, d_pool), lambda i: (i, 0)),
        scratch_shapes=[pltpu.VMEM((tm + halo, d_pool), BF16)],
        compiler_params=_params(("arbitrary",),
                                _vmem_limit(tm * d_pool * 4 + ng * gp * gp * 2,
                                            (tm + halo) * d_pool * 2, 8 << 20)),
        name="pool",
    )(prest, w, b, scale)


def _mix_kernel(ys_ref, po_ref, gs_ref, gp_ref, x_ref, wbs_ref, wbp_ref, wo_ref, g_ref, cast_ref,
                o_ref, cast_o_ref, x_sc, *, sub):
    n = pl.program_id(1)
    _side_cast(cast_ref, cast_o_ref)
    tn = x_ref.shape[1]
    x_sc[:, pl.ds(pl.multiple_of(n * tn, tn), tn)] = x_ref[...]

    @pl.when(n == 0)
    def _():
        o_ref[...] = jnp.zeros_like(o_ref)

    subs = [slice(s * sub, (s + 1) * sub) for s in range(wo_ref.shape[0] // sub)]
    branch = [(jnp.dot(ys_ref[...], wbs_ref[:, cols], preferred_element_type=F32),
               jnp.dot(po_ref[...], wbp_ref[:, cols], preferred_element_type=F32))
              for cols in subs]
    merged = [(gs_ref[:, cols].astype(F32) * ms + gp_ref[:, cols].astype(F32) * mp).astype(BF16)
              for cols, (ms, mp) in zip(subs, branch)]
    o_ref[...] += jnp.dot(jnp.concatenate(merged, axis=1), wo_ref[...],
                          preferred_element_type=F32)

    @pl.when(n == pl.num_programs(1) - 1)
    def _():
        def rows(sl):
            o_ref[sl, :] = x_sc[sl, :] + _rms(o_ref[sl, :], g_ref[...])
        _for_row_blocks(o_ref.shape[0], rows)


def _mix(ys, po, prest, x2, wbs, wbp, wo, g, gate_col0, cast_src, cast_rows,
         tm=512, tn=512, sub=256):
    t, d = x2.shape
    ds_ = ys.shape[1]
    dp = po.shape[1]
    c0 = gate_col0 // tn
    c1 = (gate_col0 + d) // tn
    resident = pl.Buffered(1)
    grid = (t // tm, d // tn)
    cast_in, cast_out, cast_shape = _side_cast_specs(cast_src, cast_rows, grid)
    cast_bytes = cast_rows * cast_src.shape[1] * 6
    return pl.pallas_call(
        functools.partial(_mix_kernel, sub=sub),
        out_shape=(jax.ShapeDtypeStruct((t, d), F32), cast_shape),
        grid=grid,
        in_specs=[pl.BlockSpec((tm, ds_), lambda i, n: (i, 0)),
                  pl.BlockSpec((tm, dp), lambda i, n: (i, 0)),
                  pl.BlockSpec((tm, tn), lambda i, n: (i, c0 + n)),
                  pl.BlockSpec((tm, tn), lambda i, n: (i, c1 + n)),
                  pl.BlockSpec((tm, tn), lambda i, n: (i, n)),
                  pl.BlockSpec((ds_, tn), lambda i, n: (0, n)),
                  pl.BlockSpec((dp, tn), lambda i, n: (0, n)),
                  pl.BlockSpec((tn, d), lambda i, n: (n, 0)),
                  pl.BlockSpec((1, d), lambda i, n: (0, 0)),
                  cast_in],
        out_specs=(pl.BlockSpec((tm, d), lambda i, n: (i, 0), pipeline_mode=resident), cast_out),
        scratch_shapes=[pltpu.VMEM((tm, d), F32)],
        compiler_params=_params(
            ("arbitrary", "arbitrary"),
            _vmem_limit(tm * (ds_ + dp) * 2 + tm * tn * 8 + (ds_ + dp) * tn * 2 + tn * d * 2
                        + cast_bytes, tm * d * 8, tm * tn * 16)),
        name="mix",
    )(ys, po, prest, prest, x2, wbs, wbp, wo, g, cast_src)


def _ffn_kernel(*refs, n_sub, n_tiles, n_up_steps, tiles_per_seq, keep, up_row_blocks):
    h_ref, g3_ref = refs[:2]
    groups = [refs[2 + k * n_sub:2 + (k + 1) * n_sub] for k in range(7)]
    wa_refs, wb_refs, cwa_refs, cwb_refs, cba_refs, cbb_refs, wd_refs = groups
    g4_ref, o_ref, c_sc, carry_sc, up_sc = refs[2 + 7 * n_sub:]
    i = pl.program_id(0)
    f = pl.program_id(1)
    tm = h_ref.shape[0]

    @pl.when(f == 0)
    def _():
        def rows(sl):
            c_sc[sl, :] = _rms(h_ref[sl, :], g3_ref[...]).astype(BF16)
            o_ref[sl, :] = jnp.zeros((NORM_ROWS, o_ref.shape[1]), F32)
        _for_row_blocks(tm, rows)

    seq_start = i % tiles_per_seq == 0

    rb = tm // up_row_blocks
    row_in_head = lax.broadcasted_iota(jnp.int32, (keep, wd_refs[0].shape[0]), 0)
    n_slots = 2 * n_sub

    def up_products(n_act):
        base = (f % 2) * n_slots
        for s in range(n_act):
            for half, w_refs in enumerate((wa_refs, wb_refs)):
                w = w_refs[s][...]
                for r0 in range(0, tm, rb):
                    up_sc[base + 2 * s + half, r0:r0 + rb] = jnp.dot(
                        c_sc[r0:r0 + rb, :], w, preferred_element_type=F32)

    def conv(slot, half, tile, cw_ref, cb_ref):
        prev = jnp.where(seq_start, 0.0, carry_sc[half, tile])
        carry_sc[half, tile] = up_sc[slot, tm - keep:tm]
        cw = cw_ref[...]
        outs = []
        for r0 in range(0, tm, rb):
            up = up_sc[slot, r0:r0 + rb]
            z = cw[0:1] * up
            zp = cw[0:1] * prev
            for k in range(1, CONV_WIDTH):
                r = pltpu.roll(z, 1, 0)
                head = jnp.where(row_in_head == 0, zp[keep - 1:keep], r[0:keep])
                z = cw[k:k + 1] * up + jnp.concatenate([head, r[keep:]], axis=0)
                if k + 1 < CONV_WIDTH:
                    zp = cw[k:k + 1] * prev + pltpu.roll(zp, 1, 0)
            outs.append(z + cb_ref[...])
            prev = up[rb - keep:]
        return outs

    def down_products(n_act):
        base = ((f + 1) % 2) * n_slots
        fms = []
        for s in range(n_act):
            tile = (f - 1) * n_sub + s
            ua = conv(base + 2 * s, 0, tile, cwa_refs[s], cba_refs[s])
            ub = conv(base + 2 * s + 1, 1, tile, cwb_refs[s], cbb_refs[s])
            fms.append(jnp.concatenate(
                [(_gelu_tanh(a) * b).astype(BF16) for a, b in zip(ua, ub)], axis=0))
        fm = jnp.concatenate(fms, axis=1)
        wd = jnp.concatenate([wd_refs[s][...] for s in range(n_act)], axis=0)
        o_ref[...] += jnp.dot(fm, wd, preferred_element_type=F32)

    tail = n_tiles - (n_up_steps - 1) * n_sub
    assert n_up_steps >= 2

    def run(n_up, n_down):
        def body():
            if n_up:
                up_products(n_up)
            if n_down:
                down_products(n_down)
        return body

    pl.when(f == 0)(run(n_sub, 0))
    pl.when((f > 0) & (f < n_up_steps - 1))(run(n_sub, n_sub))
    pl.when(f == n_up_steps - 1)(run(tail, n_sub))
    pl.when(f == n_up_steps)(run(0, tail))

    @pl.when(f == pl.num_programs(1) - 1)
    def _():
        def rows(sl):
            o_ref[sl, :] = h_ref[sl, :] + _rms(o_ref[sl, :], g4_ref[...])
        _for_row_blocks(tm, rows)


def _ffn(h1, g3, w_up, conv_w, conv_b, w_down, g4, seq_len, tm=512, n_sub=2, keep=8,
         up_row_blocks=2):
    t, d = h1.shape
    d_ff = w_down.shape[0]
    sub = w_up.shape[2]
    n_tiles = d_ff // sub
    assert w_up.shape == (2 * n_tiles, d, sub)
    nf = pl.cdiv(n_tiles, n_sub)
    resident = pl.Buffered(1)

    def tile_of(s, lag):
        return lambda f: jnp.minimum(jnp.maximum(f - lag, 0) * n_sub + s, n_tiles - 1)

    def per_sub(shape, index, lag):
        return [pl.BlockSpec(shape, functools.partial(index, tile_of(s, lag)))
                for s in range(n_sub)]

    col_a = lambda tile, i, f: (0, tile(f))
    col_b = lambda tile, i, f: (0, n_tiles + tile(f))
    row_d = lambda tile, i, f: (tile(f), 0)
    up_a = lambda tile, i, f: (tile(f), 0, 0)
    up_b = lambda tile, i, f: (n_tiles + tile(f), 0, 0)
    in_specs = ([pl.BlockSpec((tm, d), lambda i, f: (i, 0), pipeline_mode=resident),
                 pl.BlockSpec((1, d), lambda i, f: (0, 0))]
                + per_sub((None, d, sub), up_a, 0) + per_sub((None, d, sub), up_b, 0)
                + per_sub((CONV_WIDTH, sub), col_a, 1) + per_sub((CONV_WIDTH, sub), col_b, 1)
                + per_sub((1, sub), col_a, 1) + per_sub((1, sub), col_b, 1)
                + per_sub((sub, d), row_d, 1)
                + [pl.BlockSpec((1, d), lambda i, f: (0, 0))])
    operands = ([h1, g3] + [w_up] * (2 * n_sub) + [conv_w] * (2 * n_sub)
                + [conv_b] * (2 * n_sub) + [w_down] * n_sub + [g4])
    tf = sub * n_sub
    return pl.pallas_call(
        functools.partial(_ffn_kernel, n_sub=n_sub, n_tiles=n_tiles, n_up_steps=nf,
                          tiles_per_seq=seq_len // tm, keep=keep, up_row_blocks=up_row_blocks),
        out_shape=jax.ShapeDtypeStruct((t, d), F32),
        grid=(t // tm, nf + 1),
        in_specs=in_specs,
        out_specs=pl.BlockSpec((tm, d), lambda i, f: (i, 0), pipeline_mode=resident),
        scratch_shapes=[pltpu.VMEM((tm, d), BF16),
                        pltpu.VMEM((2, nf * n_sub, keep, sub), F32),
                        pltpu.VMEM((2 * 2 * n_sub, tm, sub), F32)],
        compiler_params=_params(
            ("arbitrary", "arbitrary"),
            _vmem_limit(d * tf * 4 + tf * d * 2,
                        tm * d * 8 + tm * d * 2 + 2 * nf * tf * keep * 4 + 4 * tm * tf * 4,
                        tm * tf * 32)),
        name="ffn",
    )(*operands)


def kernel(x, norm_pre_mix, w_in, ssm_lambda_re, ssm_lambda_im, ssm_log_step, ssm_b_re, ssm_b_im,
           ssm_c_re, ssm_c_im, ssm_d, ssm_glu_w, ssm_glu_b, pool_w, pool_b, pool_scale,
           w_branch_ssm, w_branch_pool, w_out, norm_post_mix, norm_pre_ffn, w_up, ffn_conv_w,
           ffn_conv_b, w_down, norm_post_ffn):
    bsz, seq_len, d = x.shape
    depth = w_in.shape[0]
    d_ssm = ssm_d.shape[1]
    d_pool = pool_scale.shape[1]
    n_groups = d_ssm // SSM_GROUP
    assert seq_len // CHUNK == 1 << N_SCAN_PASSES
    t = bsz * seq_len
    h = x.reshape(t, d)
    row = lambda v: v.reshape(1, -1)
    for i in range(depth):
        a = _norm(h, row(norm_pre_mix[i]))
        prest, w_up_b = _inproj(a, w_in[i], d_ssm, d_pool, w_up[i], 32, FFN_SUB)
        ut = _inproj_ssm(a, w_in[i], d_ssm, seq_len)
        m, rt, et, wsc = _ssm_prep(ssm_lambda_re[i], ssm_lambda_im[i], ssm_log_step[i],
                                   ssm_b_re[i], ssm_b_im[i], ssm_c_re[i], ssm_c_im[i])
        yt, w_out_b, w_bs_b, w_bp_b = _ssm(
            ut, m, rt, et, wsc, ssm_d[i].reshape(n_groups, SSM_GROUP, 1), seq_len // CHUNK,
            [w_out[i], w_branch_ssm[i], w_branch_pool[i]])
        ys = _glu(yt, ssm_glu_w[i].astype(BF16), row(ssm_glu_b[i]), seq_len)
        po = _pool(prest, pool_w[i].astype(BF16), pool_b[i][:, None, :], row(pool_scale[i]),
                   seq_len, d_pool)
        h, w_down_b = _mix(ys, po, prest, h, w_bs_b, w_bp_b, w_out_b,
                           row(norm_post_mix[i]), d_pool, w_down[i], 128)
        h = _ffn(h, row(norm_pre_ffn[i]), w_up_b, ffn_conv_w[i], row(ffn_conv_b[i]),
                 w_down_b, row(norm_post_ffn[i]), seq_len)
    return h.reshape(bsz, seq_len, d)
```

```python
import functools
import math

import jax
import jax.numpy as jnp
from jax import lax
from jax.experimental import pallas as pl
from jax.experimental.pallas import tpu as pltpu

F32 = jnp.float32
BF16 = jnp.bfloat16

EPS = 1e-6
MIN_NEG_REAL = -1e-4
SSM_GROUP = 16
SSM_STATE = 64
POOL_WINDOWS = (2, 4, 8, 16)
CONV_WIDTH = 3
CHUNK = 16
N_SCAN_PASSES = 7
LANES = 128

V7X_VMEM_BYTES = 64 * 1024 * 1024
VMEM_RESERVE = 6 * 1024 * 1024
VMEM_CEILING = V7X_VMEM_BYTES - VMEM_RESERVE
VMEM_SLACK = 4 * 1024 * 1024
SMALL_KERNEL_TEMP = 8 * 1024 * 1024


def _vmem_limit(pipelined_bytes, scratch_bytes, temp_bytes):
    return min(2 * pipelined_bytes + scratch_bytes + temp_bytes + VMEM_SLACK, VMEM_CEILING)


def _params(semantics, vmem):
    return pltpu.CompilerParams(dimension_semantics=semantics, vmem_limit_bytes=vmem)


def _rms(xf, g):
    ms = jnp.mean(xf * xf, axis=-1, keepdims=True)
    return xf * lax.rsqrt(ms + EPS) * g


NORM_ROWS = 32


def _for_row_blocks(n_rows, body):
    def step(r, carry):
        body(pl.ds(pl.multiple_of(r * NORM_ROWS, NORM_ROWS), NORM_ROWS))
        return carry
    lax.fori_loop(0, n_rows // NORM_ROWS, step, 0, unroll=2)


def _sigmoid(x):
    return 1.0 / (1.0 + jnp.exp(-x))


def _gelu_tanh(x):
    c = math.sqrt(2.0 / math.pi)
    return 0.5 * x * (1.0 + jnp.tanh(c * (x + 0.044715 * (x * x * x))))


def _norm_kernel(x_ref, g_ref, o_ref):
    def rows(sl):
        o_ref[sl, :] = _rms(x_ref[sl, :], g_ref[...]).astype(BF16)
    _for_row_blocks(x_ref.shape[0], rows)


def _norm(x2, g, tm=512):
    t, d = x2.shape
    return pl.pallas_call(
        _norm_kernel,
        out_shape=jax.ShapeDtypeStruct((t, d), BF16),
        grid=(t // tm,),
        in_specs=[pl.BlockSpec((tm, d), lambda i: (i, 0)),
                  pl.BlockSpec((1, d), lambda i: (0, 0))],
        out_specs=pl.BlockSpec((tm, d), lambda i: (i, 0)),
        compiler_params=_params(("arbitrary",), _vmem_limit(tm * d * 6, 0, tm * d * 8)),
        name="norm",
    )(x2, g)


def _side_cast_specs(src, rows, grid):
    n_blocks = src.shape[0] // rows
    assert src.shape[0] % rows == 0 and n_blocks <= math.prod(grid)

    def index(*g):
        step = g[0]
        for k in range(1, len(grid)):
            step = step * grid[k] + g[k]
        return (jnp.minimum(step, n_blocks - 1), 0)

    spec = pl.BlockSpec((rows, src.shape[1]), index)
    return spec, spec, jax.ShapeDtypeStruct(src.shape, BF16)


def _inproj_kernel(a_ref, w_ref, cast_ref, o_ref, cast_o_ref, *, n_plain, row_blocks):
    gate = pl.program_id(1) >= n_plain
    w = w_ref[...].astype(BF16)
    rows = a_ref.shape[0] // row_blocks
    for r0 in range(0, a_ref.shape[0], rows):
        p = jnp.dot(a_ref[r0:r0 + rows, :], w, preferred_element_type=F32)
        o_ref[r0:r0 + rows, :] = jnp.where(gate, _sigmoid(p), p).astype(BF16)
    cast_o_ref[...] = cast_ref[...].astype(BF16)


def _inproj(a, w, col0, n_plain_cols, cast_src, cast_rows, tm=1024, tn=512, row_blocks=4):
    t, d = a.shape
    n = w.shape[1] - col0
    c0 = col0 // tn
    grid = (t // tm, n // tn)
    cast_in, cast_out, cast_shape = _side_cast_specs(cast_src, cast_rows, grid)
    cast_bytes = cast_rows * cast_src.shape[1] * 6
    return pl.pallas_call(
        functools.partial(_inproj_kernel, n_plain=n_plain_cols // tn, row_blocks=row_blocks),
        out_shape=(jax.ShapeDtypeStruct((t, n), BF16), cast_shape),
        grid=grid,
        in_specs=[pl.BlockSpec((tm, d), lambda i, j: (i, 0)),
                  pl.BlockSpec((d, tn), lambda i, j: (0, c0 + j)),
                  cast_in],
        out_specs=(pl.BlockSpec((tm, tn), lambda i, j: (i, j)), cast_out),
        compiler_params=_params(("arbitrary", "arbitrary"),
                                _vmem_limit(tm * d * 2 + d * tn * 4 + tm * tn * 2 + cast_bytes, 0,
                                            d * tn * 2 + tm * tn * 8)),
        name="inproj",
    )(a, w, cast_src)


def _inproj_ssm_kernel(a_ref, w_ref, o_ref, p_sc, *, row_blocks):
    chunks = o_ref.shape[-1]
    n_slabs = p_sc.shape[0]
    w = w_ref[...].astype(BF16)
    rows = a_ref.shape[0] // row_blocks
    for r0 in range(0, a_ref.shape[0], rows):
        p = jnp.dot(a_ref[r0:r0 + rows, :], w, preferred_element_type=F32)
        for s in range(n_slabs):
            p_sc[s, r0:r0 + rows] = p[:, s * LANES:(s + 1) * LANES]
    gps = LANES // SSM_GROUP
    for q in range(CHUNK):
        for s in range(n_slabs):
            blk = p_sc[s, pl.ds(q, chunks, stride=CHUNK), :]
            o_ref[s * gps:(s + 1) * gps, q] = (
                blk.T.reshape(gps, SSM_GROUP, chunks).astype(BF16))


def _inproj_ssm(a, w, n, seq_len, tn=256, row_blocks=8):
    t, d = a.shape
    g = n // SSM_GROUP
    chunks = seq_len // CHUNK
    return pl.pallas_call(
        functools.partial(_inproj_ssm_kernel, row_blocks=row_blocks),
        out_shape=jax.ShapeDtypeStruct((g, CHUNK, SSM_GROUP, t // CHUNK), BF16),
        grid=(t // seq_len, n // tn),
        in_specs=[pl.BlockSpec((seq_len, d), lambda bi, j: (bi, 0)),
                  pl.BlockSpec((d, tn), lambda bi, j: (0, j))],
        out_specs=pl.BlockSpec((tn // SSM_GROUP, CHUNK, SSM_GROUP, chunks),
                               lambda bi, j: (j, 0, 0, bi)),
        scratch_shapes=[pltpu.VMEM((tn // LANES, seq_len, LANES), F32)],
        compiler_params=_params(("arbitrary", "arbitrary"),
                                _vmem_limit(seq_len * d * 2 + d * tn * 4 + tn * seq_len * 2,
                                            seq_len * tn * 4, d * tn * 2 + seq_len * tn * 16)),
        name="inproj_ssm",
    )(a, w)


def _ssm_prep_kernel(lre_ref, lim_ref, ls_ref, btr_ref, bti_ref, cr_ref, ci_ref,
                     m_ref, rt_ref, et_ref, w_ref, *, groups):
    p2 = 2 * SSM_STATE
    cq = CHUNK * SSM_GROUP
    lane = lax.broadcasted_iota(jnp.int32, (1, p2), 1)
    is_re = lane < SSM_STATE
    sgn = jnp.where(is_re, 1.0, -1.0).astype(F32)
    lane_blk = lax.broadcasted_iota(jnp.int32, (cq, cq), 1) // SSM_GROUP
    row8 = lax.broadcasted_iota(jnp.int32, (8, p2), 0)
    is_re8 = lax.broadcasted_iota(jnp.int32, (8, p2), 1) < SSM_STATE

    for gi in range(groups):
        lr = jnp.minimum(lre_ref[gi], MIN_NEG_REAL)
        li = lim_ref[gi]
        dt = jnp.exp(ls_ref[gi])
        mag = jnp.exp(lr * dt)
        ang = li * dt
        ar = mag * jnp.cos(ang)
        ai = mag * jnp.sin(ang)
        nr = ar - 1.0
        ni = ai
        den = lr * lr + li * li
        f_re = (nr * lr + ni * li) / den
        f_im = (ni * lr - nr * li) / den
        btr = btr_ref[gi]
        bti = bti_ref[gi]
        bbr = f_re * btr - f_im * bti
        bbi = f_re * bti + f_im * btr
        cr = cr_ref[gi]
        ci = ci_ref[gi]

        pa = [jnp.where(is_re, 1.0, 0.0).astype(F32)]
        pb = [jnp.where(is_re, 0.0, 1.0).astype(F32)]
        for _ in range(CHUNK):
            a_prev, b_prev = pa[-1], pb[-1]
            pa.append(a_prev * ar + b_prev * ai)
            pb.append(b_prev * ar - a_prev * ai)

        gk = [sgn * (cr * pa[k] + ci * pb[k]) for k in range(CHUNK + 1)]
        fs = jnp.concatenate(gk[:CHUNK], axis=0)
        et = jnp.concatenate(gk[1:], axis=0)
        rtt = jnp.concatenate(
            [bbr * pa[CHUNK - 1 - q] + bbi * pb[CHUNK - 1 - q] for q in range(CHUNK)], axis=0)
        bst = jnp.where(is_re, bbr, bbi)
        bst_t = jnp.concatenate([bst] * CHUNK, axis=0)
        kw = lax.dot_general(fs, bst_t, (((1,), (1,)), ((), ())),
                             precision=lax.Precision.HIGHEST, preferred_element_type=F32)
        m = jnp.zeros((cq, cq), F32)
        for q in range(CHUNK):
            if q == 0:
                shifted = kw
            else:
                shifted = jnp.concatenate(
                    [jnp.zeros((q * SSM_GROUP, cq), F32), kw[:cq - q * SSM_GROUP]], axis=0)
            m = jnp.where(lane_blk == q, shifted, m)
        m_ref[gi] = m.astype(BF16)
        rt_ref[gi] = rtt.T.astype(BF16)
        et_ref[gi] = et.astype(BF16)

        wa = jnp.broadcast_to(pa[CHUNK], (8, p2))
        rows = jnp.zeros((8, p2), F32)
        for k in range(N_SCAN_PASSES):
            rows = jnp.where(row8 == k, wa, rows)
            swapped = pltpu.roll(wa, SSM_STATE, 1)
            re2 = jnp.where(is_re8, wa, swapped)
            im2 = jnp.where(is_re8, swapped, wa)
            wb = jnp.where(is_re8, -swapped, swapped)
            wa = wa * re2 + wb * im2
        full = jnp.concatenate([rows, jnp.zeros((p2 - 8, p2), F32)], axis=0)
        w_ref[gi] = full.T


def _ssm_prep(lam_re, lam_im, log_step, b_re, b_im, c_re, c_im, groups=8):
    g, p = lam_re.shape
    p2 = 2 * p
    cq = CHUNK * SSM_GROUP
    dup = lambda v: jnp.concatenate([v, v], axis=-1)
    lre = dup(lam_re)[:, None, :]
    lim = dup(lam_im)[:, None, :]
    ls = jnp.broadcast_to(log_step[:, None, None], (g, 1, p2))
    btr = dup(jnp.swapaxes(b_re, 1, 2))
    bti = dup(jnp.swapaxes(b_im, 1, 2))
    cr = dup(c_re)
    ci = dup(c_im)
    row = pl.BlockSpec((groups, 1, p2), lambda i: (i, 0, 0))
    mat = pl.BlockSpec((groups, SSM_GROUP, p2), lambda i: (i, 0, 0))
    return pl.pallas_call(
        functools.partial(_ssm_prep_kernel, groups=groups),
        out_shape=(jax.ShapeDtypeStruct((g, cq, cq), BF16),
                   jax.ShapeDtypeStruct((g, p2, cq), BF16),
                   jax.ShapeDtypeStruct((g, cq, p2), BF16),
                   jax.ShapeDtypeStruct((g, p2, p2), F32)),
        grid=(g // groups,),
        in_specs=[row, row, row, mat, mat, mat, mat],
        out_specs=(pl.BlockSpec((groups, cq, cq), lambda i: (i, 0, 0)),
                   pl.BlockSpec((groups, p2, cq), lambda i: (i, 0, 0)),
                   pl.BlockSpec((groups, cq, p2), lambda i: (i, 0, 0)),
                   pl.BlockSpec((groups, p2, p2), lambda i: (i, 0, 0))),
        compiler_params=_params(("arbitrary",),
                                _vmem_limit(groups * cq * cq * 4, 0, SMALL_KERNEL_TEMP)),
        name="ssm_prep",
    )(lre, lim, ls, btr, bti, cr, ci)


def _ssm_kernel(*refs, groups, chunks_per_seq, n_cast):
    u_ref, m_ref, rt_ref, et_ref, w_ref, d_ref = refs[:6]
    o_ref = refs[6 + n_cast]
    for src, dst in zip(refs[6:6 + n_cast], refs[7 + n_cast:]):
        dst[...] = src[...].astype(BF16)
    cq = CHUNK * SSM_GROUP
    nc = u_ref.shape[-1]
    pos = lax.broadcasted_iota(jnp.int32, (SSM_STATE, nc), 1) % chunks_per_seq
    for gi in range(groups):
        x3 = u_ref[gi]
        x = x3.reshape(cq, nc)
        y = jnp.dot(m_ref[gi], x, preferred_element_type=F32)
        r = jnp.dot(rt_ref[gi], x, preferred_element_type=F32)
        s_re, s_im = r[:SSM_STATE], r[SSM_STATE:]
        for k in range(N_SCAN_PASSES):
            dist = 1 << k
            wr = w_ref[gi, 0:SSM_STATE, k:k + 1]
            wi = w_ref[gi, SSM_STATE:2 * SSM_STATE, k:k + 1]
            keep = pos >= dist
            sh_re = jnp.where(keep, pltpu.roll(s_re, dist, 1), 0.0)
            sh_im = jnp.where(keep, pltpu.roll(s_im, dist, 1), 0.0)
            s_re, s_im = (s_re + wr * sh_re - wi * sh_im,
                          s_im + wr * sh_im + wi * sh_re)
        first = pos >= 1
        sp = jnp.concatenate([jnp.where(first, pltpu.roll(s_re, 1, 1), 0.0),
                              jnp.where(first, pltpu.roll(s_im, 1, 1), 0.0)], axis=0)
        y = y + jnp.dot(et_ref[gi], sp.astype(BF16), preferred_element_type=F32)
        y3 = y.reshape(CHUNK, SSM_GROUP, nc) + d_ref[gi][None] * x3.astype(F32)
        o_ref[gi] = _gelu_tanh(y3).astype(BF16)


def _ssm(ut, m, rt, et, w, d, chunks_per_seq, cast_srcs, groups=4):
    g, q, j, nc = ut.shape
    cq = q * j
    p2 = rt.shape[1]
    grid = (g // groups,)
    blk = lambda *s: pl.BlockSpec((groups,) + s, lambda i: (i,) + (0,) * len(s))
    casts = [_side_cast_specs(src, src.shape[0] // grid[0], grid) for src in cast_srcs]
    cast_bytes = sum(src.size // grid[0] * 6 for src in cast_srcs)
    return pl.pallas_call(
        functools.partial(_ssm_kernel, groups=groups, chunks_per_seq=chunks_per_seq,
                          n_cast=len(cast_srcs)),
        out_shape=(jax.ShapeDtypeStruct(ut.shape, BF16), *[c[2] for c in casts]),
        grid=grid,
        in_specs=[blk(q, j, nc), blk(cq, cq), blk(p2, cq), blk(cq, p2), blk(p2, p2), blk(j, 1),
                  *[c[0] for c in casts]],
        out_specs=(blk(q, j, nc), *[c[1] for c in casts]),
        compiler_params=_params(("arbitrary",),
                                _vmem_limit(groups * cq * nc * 8 + cast_bytes, 0,
                                            SMALL_KERNEL_TEMP)),
        name="ssm",
    )(ut, m, rt, et, w, d, *cast_srcs)


def _glu_kernel(y_ref, w_ref, b_ref, o_ref, y_sc, *, tn):
    n = pl.program_id(1)
    chunks = y_ref.shape[-1]
    n_slabs = y_sc.shape[0]

    @pl.when(n == 0)
    def _():
        for q in range(CHUNK):
            yq = y_ref[:, q]
            yq = yq.reshape(yq.shape[0] * yq.shape[1], chunks).astype(F32).T
            for s in range(n_slabs):
                y_sc[s, pl.ds(q, chunks, stride=CHUNK), :] = yq[:, s * LANES:(s + 1) * LANES]

    y = jnp.concatenate([y_sc[s] for s in range(n_slabs)], axis=1).astype(BF16)
    z = jnp.dot(y, w_ref[...], preferred_element_type=F32) + b_ref[...]
    per_tile = tn // LANES
    yn = jnp.concatenate([y_sc[n * per_tile + s] for s in range(per_tile)], axis=1)
    o_ref[...] = (yn * _sigmoid(z)).astype(BF16)


def _glu(yt, w, b, seq_len, tn=512):
    g, q, j, nc = yt.shape
    d = g * j
    chunks = seq_len // q
    t = nc * q
    return pl.pallas_call(
        functools.partial(_glu_kernel, tn=tn),
        out_shape=jax.ShapeDtypeStruct((t, d), BF16),
        grid=(nc // chunks, d // tn),
        in_specs=[pl.BlockSpec((g, q, j, chunks), lambda bi, n: (0, 0, 0, bi)),
                  pl.BlockSpec((d, tn), lambda bi, n: (0, n)),
                  pl.BlockSpec((1, tn), lambda bi, n: (0, n))],
        out_specs=pl.BlockSpec((seq_len, tn), lambda bi, n: (bi, n)),
        scratch_shapes=[pltpu.VMEM((d // LANES, seq_len, LANES), F32)],
        compiler_params=_params(("arbitrary", "arbitrary"),
                                _vmem_limit((seq_len * d + d * tn + seq_len * tn) * 2,
                                            seq_len * d * 4, seq_len * d * 2 + seq_len * tn * 12)),
        name="glu",
    )(yt, w, b)


def _pool_kernel(u_ref, w_ref, b_ref, s_ref, o_ref, ext_sc, *, tiles_per_seq, halo):
    i = pl.program_id(0)
    tm = u_ref.shape[0]
    gp = w_ref.shape[1]

    @pl.when(i % tiles_per_seq == 0)
    def _():
        ext_sc[0:halo] = jnp.zeros((halo, ext_sc.shape[1]), BF16)

    @pl.when(i % tiles_per_seq != 0)
    def _():
        ext_sc[0:halo] = ext_sc[tm:tm + halo]

    ext_sc[halo:halo + tm] = u_ref[...]

    t_loc = lax.broadcasted_iota(jnp.int32, (tm, tm + halo), 0)
    s_loc = lax.broadcasted_iota(jnp.int32, (tm, tm + halo), 1)
    lag = t_loc + halo - s_loc
    t_seq = (i % tiles_per_seq) * tm + lax.broadcasted_iota(jnp.int32, (tm, 1), 0)
    for gi, win in enumerate(POOL_WINDOWS):
        band = jnp.where((lag >= 0) & (lag < win), 1.0, 0.0).astype(BF16)
        cols = slice(gi * gp, (gi + 1) * gp)
        wsum = jnp.dot(band, ext_sc[:, cols], preferred_element_type=F32)
        cnt = jnp.minimum(t_seq + 1, win).astype(F32)
        z = wsum / cnt - u_ref[:, cols].astype(F32)
        z = jnp.dot(z.astype(BF16), w_ref[gi], preferred_element_type=F32) + b_ref[gi]
        o_ref[:, cols] = (z * s_ref[:, cols]).astype(BF16)


def _pool(prest, w, b, scale, seq_len, d_pool, tm=512, halo=128):
    t = prest.shape[0]
    ng, gp, _ = w.shape
    return pl.pallas_call(
        functools.partial(_pool_kernel, tiles_per_seq=seq_len // tm, halo=halo),
        out_shape=jax.ShapeDtypeStruct((t, d_pool), BF16),
        grid=(t // tm,),
        in_specs=[pl.BlockSpec((tm, d_pool), lambda i: (i, 0)),
                  pl.BlockSpec((ng, gp, gp), lambda i: (0, 0, 0)),
                  pl.BlockSpec((ng, 1, gp), lambda i: (0, 0, 0)),
                  pl.BlockSpec((1, d_pool), lambda i: (0, 0))],
        out_specs=pl.BlockSpec((tm, d_pool), lambda i: (i, 0)),
        scratch_shapes=[pltpu.VMEM((tm + halo, d_pool), BF16)],
        compiler_params=_params(("arbitrary",),
                                _vmem_limit(tm * d_pool * 4 + ng * gp * gp * 2,
                                            (tm + halo) * d_pool * 2, SMALL_KERNEL_TEMP)),
        name="pool",
    )(prest, w, b, scale)


def _mix_kernel(ys_ref, po_ref, gs_ref, gp_ref, x_ref, wbs_ref, wbp_ref, wo_ref, g_ref, cast_ref,
                o_ref, cast_o_ref, x_sc, *, sub):
    n = pl.program_id(1)
    cast_o_ref[...] = cast_ref[...].astype(BF16)
    tn = x_ref.shape[1]
    x_sc[:, pl.ds(pl.multiple_of(n * tn, tn), tn)] = x_ref[...]

    @pl.when(n == 0)
    def _():
        o_ref[...] = jnp.zeros_like(o_ref)

    subs = [slice(s * sub, (s + 1) * sub) for s in range(wo_ref.shape[0] // sub)]
    branch = [(jnp.dot(ys_ref[...], wbs_ref[:, cols], preferred_element_type=F32),
               jnp.dot(po_ref[...], wbp_ref[:, cols], preferred_element_type=F32))
              for cols in subs]
    merged = [(gs_ref[:, cols].astype(F32) * ms + gp_ref[:, cols].astype(F32) * mp).astype(BF16)
              for cols, (ms, mp) in zip(subs, branch)]
    o_ref[...] += jnp.dot(jnp.concatenate(merged, axis=1), wo_ref[...],
                          preferred_element_type=F32)

    @pl.when(n == pl.num_programs(1) - 1)
    def _():
        def rows(sl):
            o_ref[sl, :] = x_sc[sl, :] + _rms(o_ref[sl, :], g_ref[...])
        _for_row_blocks(o_ref.shape[0], rows)


def _mix(ys, po, prest, x2, wbs, wbp, wo, g, gate_col0, cast_src, cast_rows,
         tm=512, tn=512, sub=256):
    t, d = x2.shape
    ds_ = ys.shape[1]
    dp = po.shape[1]
    c0 = gate_col0 // tn
    c1 = (gate_col0 + d) // tn
    resident = pl.Buffered(1)
    grid = (t // tm, d // tn)
    cast_in, cast_out, cast_shape = _side_cast_specs(cast_src, cast_rows, grid)
    cast_bytes = cast_rows * cast_src.shape[1] * 6
    return pl.pallas_call(
        functools.partial(_mix_kernel, sub=sub),
        out_shape=(jax.ShapeDtypeStruct((t, d), F32), cast_shape),
        grid=grid,
        in_specs=[pl.BlockSpec((tm, ds_), lambda i, n: (i, 0)),
                  pl.BlockSpec((tm, dp), lambda i, n: (i, 0)),
                  pl.BlockSpec((tm, tn), lambda i, n: (i, c0 + n)),
                  pl.BlockSpec((tm, tn), lambda i, n: (i, c1 + n)),
                  pl.BlockSpec((tm, tn), lambda i, n: (i, n)),
                  pl.BlockSpec((ds_, tn), lambda i, n: (0, n)),
                  pl.BlockSpec((dp, tn), lambda i, n: (0, n)),
                  pl.BlockSpec((tn, d), lambda i, n: (n, 0)),
                  pl.BlockSpec((1, d), lambda i, n: (0, 0)),
                  cast_in],
        out_specs=(pl.BlockSpec((tm, d), lambda i, n: (i, 0), pipeline_mode=resident), cast_out),
        scratch_shapes=[pltpu.VMEM((tm, d), F32)],
        compiler_params=_params(
            ("arbitrary", "arbitrary"),
            _vmem_limit(tm * (ds_ + dp) * 2 + tm * tn * 8 + (ds_ + dp) * tn * 2 + tn * d * 2
                        + cast_bytes, tm * d * 8, tm * tn * 16)),
        name="mix",
    )(ys, po, prest, prest, x2, wbs, wbp, wo, g, cast_src)


def _ffn_kernel(*refs, n_sub, n_tiles, tiles_per_seq, keep, up_row_blocks):
    h_ref, g3_ref = refs[:2]
    groups = [refs[2 + k * n_sub:2 + (k + 1) * n_sub] for k in range(7)]
    wa_refs, wb_refs, cwa_refs, cwb_refs, cba_refs, cbb_refs, wd_refs = groups
    g4_ref, o_ref, c_sc, carry_sc = refs[2 + 7 * n_sub:]
    i = pl.program_id(0)
    f = pl.program_id(1)
    tm = h_ref.shape[0]

    @pl.when(f == 0)
    def _():
        def rows(sl):
            c_sc[sl, :] = _rms(h_ref[sl, :], g3_ref[...]).astype(BF16)
            o_ref[sl, :] = jnp.zeros((NORM_ROWS, o_ref.shape[1]), F32)
        _for_row_blocks(tm, rows)

    seq_start = i % tiles_per_seq == 0

    rb = tm // up_row_blocks
    row_in_head = lax.broadcasted_iota(jnp.int32, (keep, wd_refs[0].shape[0]), 0)

    def up_blocks(w_ref):
        w = w_ref[...]
        return [jnp.dot(c_sc[r0:r0 + rb, :], w, preferred_element_type=F32)
                for r0 in range(0, tm, rb)]

    def conv(blocks, half, tile, cw_ref, cb_ref):
        prev = jnp.where(seq_start, 0.0, carry_sc[half, tile])
        carry_sc[half, tile] = blocks[-1][rb - keep:]
        cw = cw_ref[...]
        outs = []
        for up in blocks:
            z = cw[0:1] * up
            zp = cw[0:1] * prev
            for k in range(1, CONV_WIDTH):
                r = pltpu.roll(z, 1, 0)
                head = jnp.where(row_in_head == 0, zp[keep - 1:keep], r[0:keep])
                z = cw[k:k + 1] * up + jnp.concatenate([head, r[keep:]], axis=0)
                if k + 1 < CONV_WIDTH:
                    zp = cw[k:k + 1] * prev + pltpu.roll(zp, 1, 0)
            outs.append(z + cb_ref[...])
            prev = up[rb - keep:]
        return outs

    def step(n_act):
        ups_a = [up_blocks(wa_refs[s]) for s in range(n_act)]
        ups_b = [up_blocks(wb_refs[s]) for s in range(n_act)]
        ups = list(zip(ups_a, ups_b))
        fms = []
        for s in range(n_act):
            tile = f * n_sub + s
            ua = conv(ups[s][0], 0, tile, cwa_refs[s], cba_refs[s])
            ub = conv(ups[s][1], 1, tile, cwb_refs[s], cbb_refs[s])
            fms.append(jnp.concatenate(
                [(_gelu_tanh(a) * b).astype(BF16) for a, b in zip(ua, ub)], axis=0))
        fm = jnp.concatenate(fms, axis=1)
        wd = jnp.concatenate([wd_refs[s][...] for s in range(n_act)], axis=0)
        o_ref[...] += jnp.dot(fm, wd, preferred_element_type=F32)

    last = pl.num_programs(1) - 1
    tail = n_tiles - (n_tiles - 1) // n_sub * n_sub
    if tail == n_sub:
        step(n_sub)
    else:
        pl.when(f < last)(lambda: step(n_sub))
        pl.when(f == last)(lambda: step(tail))

    @pl.when(f == pl.num_programs(1) - 1)
    def _():
        def rows(sl):
            o_ref[sl, :] = h_ref[sl, :] + _rms(o_ref[sl, :], g4_ref[...])
        _for_row_blocks(tm, rows)


def _ffn(h1, g3, w_up, conv_w, conv_b, w_down, g4, seq_len, tm=512, sub=256, n_sub=2, keep=8,
         up_row_blocks=2):
    t, d = h1.shape
    d_ff = w_down.shape[0]
    n_tiles = d_ff // sub
    nf = pl.cdiv(n_tiles, n_sub)
    resident = pl.Buffered(1)

    def tile_of(s):
        return lambda f: jnp.minimum(f * n_sub + s, n_tiles - 1)

    def per_sub(shape, index):
        return [pl.BlockSpec(shape, functools.partial(index, tile_of(s))) for s in range(n_sub)]

    col_a = lambda tile, i, f: (0, tile(f))
    col_b = lambda tile, i, f: (0, n_tiles + tile(f))
    row_d = lambda tile, i, f: (tile(f), 0)
    in_specs = ([pl.BlockSpec((tm, d), lambda i, f: (i, 0), pipeline_mode=resident),
                 pl.BlockSpec((1, d), lambda i, f: (0, 0))]
                + per_sub((d, sub), col_a) + per_sub((d, sub), col_b)
                + per_sub((CONV_WIDTH, sub), col_a) + per_sub((CONV_WIDTH, sub), col_b)
                + per_sub((1, sub), col_a) + per_sub((1, sub), col_b)
                + per_sub((sub, d), row_d)
                + [pl.BlockSpec((1, d), lambda i, f: (0, 0))])
    operands = ([h1, g3] + [w_up] * (2 * n_sub) + [conv_w] * (2 * n_sub)
                + [conv_b] * (2 * n_sub) + [w_down] * n_sub + [g4])
    tf = sub * n_sub
    return pl.pallas_call(
        functools.partial(_ffn_kernel, n_sub=n_sub, n_tiles=n_tiles,
                          tiles_per_seq=seq_len // tm, keep=keep, up_row_blocks=up_row_blocks),
        out_shape=jax.ShapeDtypeStruct((t, d), F32),
        grid=(t // tm, nf),
        in_specs=in_specs,
        out_specs=pl.BlockSpec((tm, d), lambda i, f: (i, 0), pipeline_mode=resident),
        scratch_shapes=[pltpu.VMEM((tm, d), BF16),
                        pltpu.VMEM((2, nf * n_sub, keep, sub), F32)],
        compiler_params=_params(
            ("arbitrary", "arbitrary"),
            _vmem_limit(d * tf * 4 + tf * d * 2,
                        tm * d * 8 + tm * d * 2 + 2 * nf * tf * keep * 4, tm * tf * 32)),
        name="ffn",
    )(*operands)


def kernel(x, norm_pre_mix, w_in, ssm_lambda_re, ssm_lambda_im, ssm_log_step, ssm_b_re, ssm_b_im,
           ssm_c_re, ssm_c_im, ssm_d, ssm_glu_w, ssm_glu_b, pool_w, pool_b, pool_scale,
           w_branch_ssm, w_branch_pool, w_out, norm_post_mix, norm_pre_ffn, w_up, ffn_conv_w,
           ffn_conv_b, w_down, norm_post_ffn):
    bsz, seq_len, d = x.shape
    depth = w_in.shape[0]
    d_ssm = ssm_d.shape[1]
    d_pool = pool_scale.shape[1]
    n_groups = d_ssm // SSM_GROUP
    assert seq_len // CHUNK == 1 << N_SCAN_PASSES
    t = bsz * seq_len
    h = x.reshape(t, d)
    row = lambda v: v.reshape(1, -1)
    for i in range(depth):
        a = _norm(h, row(norm_pre_mix[i]))
        prest, w_up_b = _inproj(a, w_in[i], d_ssm, d_pool, w_up[i], 32)
        ut = _inproj_ssm(a, w_in[i], d_ssm, seq_len)
        m, rt, et, wsc = _ssm_prep(ssm_lambda_re[i], ssm_lambda_im[i], ssm_log_step[i],
                                   ssm_b_re[i], ssm_b_im[i], ssm_c_re[i], ssm_c_im[i])
        yt, w_out_b, w_bs_b, w_bp_b = _ssm(
            ut, m, rt, et, wsc, ssm_d[i].reshape(n_groups, SSM_GROUP, 1), seq_len // CHUNK,
            [w_out[i], w_branch_ssm[i], w_branch_pool[i]])
        ys = _glu(yt, ssm_glu_w[i].astype(BF16), row(ssm_glu_b[i]), seq_len)
        po = _pool(prest, pool_w[i].astype(BF16), pool_b[i][:, None, :], row(pool_scale[i]),
                   seq_len, d_pool)
        h, w_down_b = _mix(ys, po, prest, h, w_bs_b, w_bp_b, w_out_b,
                           row(norm_post_mix[i]), d_pool, w_down[i], 128)
        h = _ffn(h, row(norm_pre_ffn[i]), w_up_b, ffn_conv_w[i], row(ffn_conv_b[i]),
                 w_down_b, row(norm_post_ffn[i]), seq_len)
    return h.reshape(bsz, seq_len, d)
```

```python
import functools
import math

import jax
import jax.numpy as jnp
from jax import lax
from jax.experimental import pallas as pl
from jax.experimental.pallas import tpu as pltpu

F32 = jnp.float32
BF16 = jnp.bfloat16

EPS = 1e-6
MIN_NEG_REAL = -1e-4
SSM_GROUP = 16
SSM_STATE = 64
POOL_WINDOWS = (2, 4, 8, 16)
CONV_WIDTH = 3
CHUNK = 16
N_SCAN_PASSES = 7
LANES = 128
FFN_SUB = 256
FFN_TILES_PER_STEP = 2

V7X_VMEM_BYTES = 64 * 1024 * 1024
VMEM_RESERVE = 6 * 1024 * 1024
VMEM_CEILING = V7X_VMEM_BYTES - VMEM_RESERVE
VMEM_SLACK = 4 * 1024 * 1024
SMALL_KERNEL_TEMP = 8 * 1024 * 1024


def _vmem_limit(pipelined_bytes, scratch_bytes, temp_bytes):
    return min(2 * pipelined_bytes + scratch_bytes + temp_bytes + VMEM_SLACK, VMEM_CEILING)


def _params(semantics, vmem):
    return pltpu.CompilerParams(dimension_semantics=semantics, vmem_limit_bytes=vmem)


def _rms(xf, g):
    ms = jnp.mean(xf * xf, axis=-1, keepdims=True)
    return xf * lax.rsqrt(ms + EPS) * g


NORM_ROWS = 32


def _for_row_blocks(n_rows, body):
    def step(r, carry):
        body(pl.ds(pl.multiple_of(r * NORM_ROWS, NORM_ROWS), NORM_ROWS))
        return carry
    lax.fori_loop(0, n_rows // NORM_ROWS, step, 0, unroll=2)


def _sigmoid(x):
    return 1.0 / (1.0 + jnp.exp(-x))


def _gelu_tanh(x):
    c = math.sqrt(2.0 / math.pi)
    return 0.5 * x * (1.0 + jnp.tanh(c * (x + 0.044715 * (x * x * x))))


def _norm_kernel(x_ref, g_ref, o_ref):
    def rows(sl):
        o_ref[sl, :] = _rms(x_ref[sl, :], g_ref[...]).astype(BF16)
    _for_row_blocks(x_ref.shape[0], rows)


def _norm(x2, g, tm=512):
    t, d = x2.shape
    return pl.pallas_call(
        _norm_kernel,
        out_shape=jax.ShapeDtypeStruct((t, d), BF16),
        grid=(t // tm,),
        in_specs=[pl.BlockSpec((tm, d), lambda i: (i, 0)),
                  pl.BlockSpec((1, d), lambda i: (0, 0))],
        out_specs=pl.BlockSpec((tm, d), lambda i: (i, 0)),
        compiler_params=_params(("arbitrary",), _vmem_limit(tm * d * 6, 0, tm * d * 8)),
        name="norm",
    )(x2, g)


def _side_cast_specs(src, rows, grid, col_tile=None, col_groups=1, group_slots=None):
    n_blocks = src.shape[0] // rows
    assert src.shape[0] % rows == 0 and n_blocks <= math.prod(grid)

    def block(*g):
        step = g[0]
        for k in range(1, len(grid)):
            step = step * grid[k] + g[k]
        return jnp.minimum(step, n_blocks - 1)

    in_spec = pl.BlockSpec((rows, src.shape[1]), lambda *g: (block(*g), 0))
    if col_tile is None:
        return in_spec, in_spec, jax.ShapeDtypeStruct(src.shape, BF16)
    n_slots = col_groups * group_slots
    out_spec = pl.BlockSpec((n_slots, rows, col_tile), lambda *g: (0, block(*g), 0))
    return in_spec, out_spec, jax.ShapeDtypeStruct((n_slots, src.shape[0], col_tile), BF16)


def _side_cast(src_ref, dst_ref, col_groups=1):
    if len(dst_ref.shape) == 2:
        dst_ref[...] = src_ref[...].astype(BF16)
        return
    n_slots, rows, col_tile = dst_ref.shape
    group_slots = n_slots // col_groups
    group_tiles = src_ref.shape[1] // col_groups // col_tile
    for g in range(col_groups):
        for t in range(group_slots):
            if t < group_tiles:
                c0 = (g * group_tiles + t) * col_tile
                dst_ref[g * group_slots + t] = src_ref[:, c0:c0 + col_tile].astype(BF16)
            else:
                dst_ref[g * group_slots + t] = jnp.zeros((rows, col_tile), BF16)


def _inproj_kernel(a_ref, w_ref, cast_ref, o_ref, cast_o_ref, *, n_plain, row_blocks,
                   cast_groups):
    gate = pl.program_id(1) >= n_plain
    w = w_ref[...].astype(BF16)
    rows = a_ref.shape[0] // row_blocks
    for r0 in range(0, a_ref.shape[0], rows):
        p = jnp.dot(a_ref[r0:r0 + rows, :], w, preferred_element_type=F32)
        o_ref[r0:r0 + rows, :] = jnp.where(gate, _sigmoid(p), p).astype(BF16)
    _side_cast(cast_ref, cast_o_ref, cast_groups)


def _inproj(a, w, col0, n_plain_cols, cast_src, cast_rows, cast_col_tile, cast_groups,
            cast_group_slots, tm=1024, tn=512, row_blocks=4):
    t, d = a.shape
    n = w.shape[1] - col0
    c0 = col0 // tn
    grid = (t // tm, n // tn)
    cast_in, cast_out, cast_shape = _side_cast_specs(cast_src, cast_rows, grid, cast_col_tile,
                                                     cast_groups, cast_group_slots)
    cast_bytes = cast_rows * cast_src.shape[1] * 6
    return pl.pallas_call(
        functools.partial(_inproj_kernel, n_plain=n_plain_cols // tn, row_blocks=row_blocks,
                          cast_groups=cast_groups),
        out_shape=(jax.ShapeDtypeStruct((t, n), BF16), cast_shape),
        grid=grid,
        in_specs=[pl.BlockSpec((tm, d), lambda i, j: (i, 0)),
                  pl.BlockSpec((d, tn), lambda i, j: (0, c0 + j)),
                  cast_in],
        out_specs=(pl.BlockSpec((tm, tn), lambda i, j: (i, j)), cast_out),
        compiler_params=_params(("arbitrary", "arbitrary"),
                                _vmem_limit(tm * d * 2 + d * tn * 4 + tm * tn * 2 + cast_bytes, 0,
                                            d * tn * 2 + tm * tn * 8)),
        name="inproj",
    )(a, w, cast_src)


def _inproj_ssm_kernel(a_ref, w_ref, o_ref, p_sc, *, row_blocks):
    chunks = o_ref.shape[-1]
    n_slabs = p_sc.shape[0]
    w = w_ref[...].astype(BF16)
    rows = a_ref.shape[0] // row_blocks
    for r0 in range(0, a_ref.shape[0], rows):
        p = jnp.dot(a_ref[r0:r0 + rows, :], w, preferred_element_type=F32)
        for s in range(n_slabs):
            p_sc[s, r0:r0 + rows] = p[:, s * LANES:(s + 1) * LANES]
    gps = LANES // SSM_GROUP
    for q in range(CHUNK):
        for s in range(n_slabs):
            blk = p_sc[s, pl.ds(q, chunks, stride=CHUNK), :]
            o_ref[s * gps:(s + 1) * gps, q] = (
                blk.T.reshape(gps, SSM_GROUP, chunks).astype(BF16))


def _inproj_ssm(a, w, n, seq_len, tn=256, row_blocks=8):
    t, d = a.shape
    g = n // SSM_GROUP
    chunks = seq_len // CHUNK
    return pl.pallas_call(
        functools.partial(_inproj_ssm_kernel, row_blocks=row_blocks),
        out_shape=jax.ShapeDtypeStruct((g, CHUNK, SSM_GROUP, t // CHUNK), BF16),
        grid=(t // seq_len, n // tn),
        in_specs=[pl.BlockSpec((seq_len, d), lambda bi, j: (bi, 0)),
                  pl.BlockSpec((d, tn), lambda bi, j: (0, j))],
        out_specs=pl.BlockSpec((tn // SSM_GROUP, CHUNK, SSM_GROUP, chunks),
                               lambda bi, j: (j, 0, 0, bi)),
        scratch_shapes=[pltpu.VMEM((tn // LANES, seq_len, LANES), F32)],
        compiler_params=_params(("arbitrary", "arbitrary"),
                                _vmem_limit(seq_len * d * 2 + d * tn * 4 + tn * seq_len * 2,
                                            seq_len * tn * 4, d * tn * 2 + seq_len * tn * 16)),
        name="inproj_ssm",
    )(a, w)


def _ssm_prep_kernel(lre_ref, lim_ref, ls_ref, btr_ref, bti_ref, cr_ref, ci_ref,
                     m_ref, rt_ref, et_ref, w_ref, *, groups):
    p2 = 2 * SSM_STATE
    cq = CHUNK * SSM_GROUP
    lane = lax.broadcasted_iota(jnp.int32, (1, p2), 1)
    is_re = lane < SSM_STATE
    sgn = jnp.where(is_re, 1.0, -1.0).astype(F32)
    lane_blk = lax.broadcasted_iota(jnp.int32, (cq, cq), 1) // SSM_GROUP
    row8 = lax.broadcasted_iota(jnp.int32, (8, p2), 0)
    is_re8 = lax.broadcasted_iota(jnp.int32, (8, p2), 1) < SSM_STATE

    for gi in range(groups):
        lr = jnp.minimum(lre_ref[gi], MIN_NEG_REAL)
        li = lim_ref[gi]
        dt = jnp.exp(ls_ref[gi])
        mag = jnp.exp(lr * dt)
        ang = li * dt
        ar = mag * jnp.cos(ang)
        ai = mag * jnp.sin(ang)
        nr = ar - 1.0
        ni = ai
        den = lr * lr + li * li
        f_re = (nr * lr + ni * li) / den
        f_im = (ni * lr - nr * li) / den
        btr = btr_ref[gi]
        bti = bti_ref[gi]
        bbr = f_re * btr - f_im * bti
        bbi = f_re * bti + f_im * btr
        cr = cr_ref[gi]
        ci = ci_ref[gi]

        pa = [jnp.where(is_re, 1.0, 0.0).astype(F32)]
        pb = [jnp.where(is_re, 0.0, 1.0).astype(F32)]
        for _ in range(CHUNK):
            a_prev, b_prev = pa[-1], pb[-1]
            pa.append(a_prev * ar + b_prev * ai)
            pb.append(b_prev * ar - a_prev * ai)

        gk = [sgn * (cr * pa[k] + ci * pb[k]) for k in range(CHUNK + 1)]
        fs = jnp.concatenate(gk[:CHUNK], axis=0)
        et = jnp.concatenate(gk[1:], axis=0)
        rtt = jnp.concatenate(
            [bbr * pa[CHUNK - 1 - q] + bbi * pb[CHUNK - 1 - q] for q in range(CHUNK)], axis=0)
        bst = jnp.where(is_re, bbr, bbi)
        bst_t = jnp.concatenate([bst] * CHUNK, axis=0)
        kw = lax.dot_general(fs, bst_t, (((1,), (1,)), ((), ())),
                             precision=lax.Precision.HIGHEST, preferred_element_type=F32)
        m = jnp.zeros((cq, cq), F32)
        for q in range(CHUNK):
            if q == 0:
                shifted = kw
            else:
                shifted = jnp.concatenate(
                    [jnp.zeros((q * SSM_GROUP, cq), F32), kw[:cq - q * SSM_GROUP]], axis=0)
            m = jnp.where(lane_blk == q, shifted, m)
        m_ref[gi] = m.astype(BF16)
        rt_ref[gi] = rtt.T.astype(BF16)
        et_ref[gi] = et.astype(BF16)

        wa = jnp.broadcast_to(pa[CHUNK], (8, p2))
        rows = jnp.zeros((8, p2), F32)
        for k in range(N_SCAN_PASSES):
            rows = jnp.where(row8 == k, wa, rows)
            swapped = pltpu.roll(wa, SSM_STATE, 1)
            re2 = jnp.where(is_re8, wa, swapped)
            im2 = jnp.where(is_re8, swapped, wa)
            wb = jnp.where(is_re8, -swapped, swapped)
            wa = wa * re2 + wb * im2
        full = jnp.concatenate([rows, jnp.zeros((p2 - 8, p2), F32)], axis=0)
        w_ref[gi] = full.T


def _ssm_prep(lam_re, lam_im, log_step, b_re, b_im, c_re, c_im, groups=8):
    g, p = lam_re.shape
    p2 = 2 * p
    cq = CHUNK * SSM_GROUP
    dup = lambda v: jnp.concatenate([v, v], axis=-1)
    lre = dup(lam_re)[:, None, :]
    lim = dup(lam_im)[:, None, :]
    ls = jnp.broadcast_to(log_step[:, None, None], (g, 1, p2))
    btr = dup(jnp.swapaxes(b_re, 1, 2))
    bti = dup(jnp.swapaxes(b_im, 1, 2))
    cr = dup(c_re)
    ci = dup(c_im)
    row = pl.BlockSpec((groups, 1, p2), lambda i: (i, 0, 0))
    mat = pl.BlockSpec((groups, SSM_GROUP, p2), lambda i: (i, 0, 0))
    return pl.pallas_call(
        functools.partial(_ssm_prep_kernel, groups=groups),
        out_shape=(jax.ShapeDtypeStruct((g, cq, cq), BF16),
                   jax.ShapeDtypeStruct((g, p2, cq), BF16),
                   jax.ShapeDtypeStruct((g, cq, p2), BF16),
                   jax.ShapeDtypeStruct((g, p2, p2), F32)),
        grid=(g // groups,),
        in_specs=[row, row, row, mat, mat, mat, mat],
        out_specs=(pl.BlockSpec((groups, cq, cq), lambda i: (i, 0, 0)),
                   pl.BlockSpec((groups, p2, cq), lambda i: (i, 0, 0)),
                   pl.BlockSpec((groups, cq, p2), lambda i: (i, 0, 0)),
                   pl.BlockSpec((groups, p2, p2), lambda i: (i, 0, 0))),
        compiler_params=_params(("arbitrary",),
                                _vmem_limit(groups * cq * cq * 4, 0, SMALL_KERNEL_TEMP)),
        name="ssm_prep",
    )(lre, lim, ls, btr, bti, cr, ci)


def _ssm_kernel(*refs, groups, chunks_per_seq, n_cast):
    u_ref, m_ref, rt_ref, et_ref, w_ref, d_ref = refs[:6]
    o_ref = refs[6 + n_cast]
    for src, dst in zip(refs[6:6 + n_cast], refs[7 + n_cast:]):
        dst[...] = src[...].astype(BF16)
    cq = CHUNK * SSM_GROUP
    nc = u_ref.shape[-1]
    pos = lax.broadcasted_iota(jnp.int32, (SSM_STATE, nc), 1) % chunks_per_seq
    for gi in range(groups):
        x3 = u_ref[gi]
        x = x3.reshape(cq, nc)
        y = jnp.dot(m_ref[gi], x, preferred_element_type=F32)
        r = jnp.dot(rt_ref[gi], x, preferred_element_type=F32)
        s_re, s_im = r[:SSM_STATE], r[SSM_STATE:]
        for k in range(N_SCAN_PASSES):
            dist = 1 << k
            wr = w_ref[gi, 0:SSM_STATE, k:k + 1]
            wi = w_ref[gi, SSM_STATE:2 * SSM_STATE, k:k + 1]
            keep = pos >= dist
            sh_re = jnp.where(keep, pltpu.roll(s_re, dist, 1), 0.0)
            sh_im = jnp.where(keep, pltpu.roll(s_im, dist, 1), 0.0)
            s_re, s_im = (s_re + wr * sh_re - wi * sh_im,
                          s_im + wr * sh_im + wi * sh_re)
        first = pos >= 1
        sp = jnp.concatenate([jnp.where(first, pltpu.roll(s_re, 1, 1), 0.0),
                              jnp.where(first, pltpu.roll(s_im, 1, 1), 0.0)], axis=0)
        y = y + jnp.dot(et_ref[gi], sp.astype(BF16), preferred_element_type=F32)
        y3 = y.reshape(CHUNK, SSM_GROUP, nc) + d_ref[gi][None] * x3.astype(F32)
        o_ref[gi] = _gelu_tanh(y3).astype(BF16)


def _ssm(ut, m, rt, et, w, d, chunks_per_seq, cast_srcs, groups=4):
    g, q, j, nc = ut.shape
    cq = q * j
    p2 = rt.shape[1]
    grid = (g // groups,)
    blk = lambda *s: pl.BlockSpec((groups,) + s, lambda i: (i,) + (0,) * len(s))
    casts = [_side_cast_specs(src, src.shape[0] // grid[0], grid) for src in cast_srcs]
    cast_bytes = sum(src.size // grid[0] * 6 for src in cast_srcs)
    return pl.pallas_call(
        functools.partial(_ssm_kernel, groups=groups, chunks_per_seq=chunks_per_seq,
                          n_cast=len(cast_srcs)),
        out_shape=(jax.ShapeDtypeStruct(ut.shape, BF16), *[c[2] for c in casts]),
        grid=grid,
        in_specs=[blk(q, j, nc), blk(cq, cq), blk(p2, cq), blk(cq, p2), blk(p2, p2), blk(j, 1),
                  *[c[0] for c in casts]],
        out_specs=(blk(q, j, nc), *[c[1] for c in casts]),
        compiler_params=_params(("arbitrary",),
                                _vmem_limit(groups * cq * nc * 8 + cast_bytes, 0,
                                            SMALL_KERNEL_TEMP)),
        name="ssm",
    )(ut, m, rt, et, w, d, *cast_srcs)


def _glu_kernel(y_ref, w_ref, b_ref, o_ref, y_sc, *, tn):
    n = pl.program_id(1)
    chunks = y_ref.shape[-1]
    n_slabs = y_sc.shape[0]

    @pl.when(n == 0)
    def _():
        for q in range(CHUNK):
            yq = y_ref[:, q]
            yq = yq.reshape(yq.shape[0] * yq.shape[1], chunks).astype(F32).T
            for s in range(n_slabs):
                y_sc[s, pl.ds(q, chunks, stride=CHUNK), :] = yq[:, s * LANES:(s + 1) * LANES]

    y = jnp.concatenate([y_sc[s] for s in range(n_slabs)], axis=1).astype(BF16)
    z = jnp.dot(y, w_ref[...], preferred_element_type=F32) + b_ref[...]
    per_tile = tn // LANES
    yn = jnp.concatenate([y_sc[n * per_tile + s] for s in range(per_tile)], axis=1)
    o_ref[...] = (yn * _sigmoid(z)).astype(BF16)


def _glu(yt, w, b, seq_len, tn=512):
    g, q, j, nc = yt.shape
    d = g * j
    chunks = seq_len // q
    t = nc * q
    return pl.pallas_call(
        functools.partial(_glu_kernel, tn=tn),
        out_shape=jax.ShapeDtypeStruct((t, d), BF16),
        grid=(nc // chunks, d // tn),
        in_specs=[pl.BlockSpec((g, q, j, chunks), lambda bi, n: (0, 0, 0, bi)),
                  pl.BlockSpec((d, tn), lambda bi, n: (0, n)),
                  pl.BlockSpec((1, tn), lambda bi, n: (0, n))],
        out_specs=pl.BlockSpec((seq_len, tn), lambda bi, n: (bi, n)),
        scratch_shapes=[pltpu.VMEM((d // LANES, seq_len, LANES), F32)],
        compiler_params=_params(("arbitrary", "arbitrary"),
                                _vmem_limit((seq_len * d + d * tn + seq_len * tn) * 2,
                                            seq_len * d * 4, seq_len * d * 2 + seq_len * tn * 12)),
        name="glu",
    )(yt, w, b)


def _pool_kernel(u_ref, w_ref, b_ref, s_ref, o_ref, ext_sc, *, tiles_per_seq, halo):
    i = pl.program_id(0)
    tm = u_ref.shape[0]
    gp = w_ref.shape[1]

    @pl.when(i % tiles_per_seq == 0)
    def _():
        ext_sc[0:halo] = jnp.zeros((halo, ext_sc.shape[1]), BF16)

    @pl.when(i % tiles_per_seq != 0)
    def _():
        ext_sc[0:halo] = ext_sc[tm:tm + halo]

    ext_sc[halo:halo + tm] = u_ref[...]

    t_loc = lax.broadcasted_iota(jnp.int32, (tm, tm + halo), 0)
    s_loc = lax.broadcasted_iota(jnp.int32, (tm, tm + halo), 1)
    lag = t_loc + halo - s_loc
    t_seq = (i % tiles_per_seq) * tm + lax.broadcasted_iota(jnp.int32, (tm, 1), 0)
    for gi, win in enumerate(POOL_WINDOWS):
        band = jnp.where((lag >= 0) & (lag < win), 1.0, 0.0).astype(BF16)
        cols = slice(gi * gp, (gi + 1) * gp)
        wsum = jnp.dot(band, ext_sc[:, cols], preferred_element_type=F32)
        cnt = jnp.minimum(t_seq + 1, win).astype(F32)
        z = wsum / cnt - u_ref[:, cols].astype(F32)
        z = jnp.dot(z.astype(BF16), w_ref[gi], preferred_element_type=F32) + b_ref[gi]
        o_ref[:, cols] = (z * s_ref[:, cols]).astype(BF16)


def _pool(prest, w, b, scale, seq_len, d_pool, tm=512, halo=128):
    t = prest.shape[0]
    ng, gp, _ = w.shape
    return pl.pallas_call(
        functools.partial(_pool_kernel, tiles_per_seq=seq_len // tm, halo=halo),
        out_shape=jax.ShapeDtypeStruct((t, d_pool), BF16),
        grid=(t // tm,),
        in_specs=[pl.BlockSpec((tm, d_pool), lambda i: (i, 0)),
                  pl.BlockSpec((ng, gp, gp), lambda i: (0, 0, 0)),
                  pl.BlockSpec((ng, 1, gp), lambda i: (0, 0, 0)),
                  pl.BlockSpec((1, d_pool), lambda i: (0, 0))],
        out_specs=pl.BlockSpec((tm, d_pool), lambda i: (i, 0)),
        scratch_shapes=[pltpu.VMEM((tm + halo, d_pool), BF16)],
        compiler_params=_params(("arbitrary",),
                                _vmem_limit(tm * d_pool * 4 + ng * gp * gp * 2,
                                            (tm + halo) * d_pool * 2, SMALL_KERNEL_TEMP)),
        name="pool",
    )(prest, w, b, scale)


def _mix_kernel(ys_ref, po_ref, gs_ref, gp_ref, x_ref, wbs_ref, wbp_ref, wo_ref, g_ref, cast_ref,
                o_ref, cast_o_ref, x_sc, *, sub):
    n = pl.program_id(1)
    cast_o_ref[...] = cast_ref[...].astype(BF16)
    tn = x_ref.shape[1]
    x_sc[:, pl.ds(pl.multiple_of(n * tn, tn), tn)] = x_ref[...]

    @pl.when(n == 0)
    def _():
        o_ref[...] = jnp.zeros_like(o_ref)

    subs = [slice(s * sub, (s + 1) * sub) for s in range(wo_ref.shape[0] // sub)]
    branch = [(jnp.dot(ys_ref[...], wbs_ref[:, cols], preferred_element_type=F32),
               jnp.dot(po_ref[...], wbp_ref[:, cols], preferred_element_type=F32))
              for cols in subs]
    merged = [(gs_ref[:, cols].astype(F32) * ms + gp_ref[:, cols].astype(F32) * mp).astype(BF16)
              for cols, (ms, mp) in zip(subs, branch)]
    o_ref[...] += jnp.dot(jnp.concatenate(merged, axis=1), wo_ref[...],
                          preferred_element_type=F32)

    @pl.when(n == pl.num_programs(1) - 1)
    def _():
        def rows(sl):
            o_ref[sl, :] = x_sc[sl, :] + _rms(o_ref[sl, :], g_ref[...])
        _for_row_blocks(o_ref.shape[0], rows)


def _mix(ys, po, prest, x2, wbs, wbp, wo, g, gate_col0, cast_src, cast_rows,
         tm=512, tn=512, sub=256):
    t, d = x2.shape
    ds_ = ys.shape[1]
    dp = po.shape[1]
    c0 = gate_col0 // tn
    c1 = (gate_col0 + d) // tn
    resident = pl.Buffered(1)
    grid = (t // tm, d // tn)
    cast_in, cast_out, cast_shape = _side_cast_specs(cast_src, cast_rows, grid)
    cast_bytes = cast_rows * cast_src.shape[1] * 6
    return pl.pallas_call(
        functools.partial(_mix_kernel, sub=sub),
        out_shape=(jax.ShapeDtypeStruct((t, d), F32), cast_shape),
        grid=grid,
        in_specs=[pl.BlockSpec((tm, ds_), lambda i, n: (i, 0)),
                  pl.BlockSpec((tm, dp), lambda i, n: (i, 0)),
                  pl.BlockSpec((tm, tn), lambda i, n: (i, c0 + n)),
                  pl.BlockSpec((tm, tn), lambda i, n: (i, c1 + n)),
                  pl.BlockSpec((tm, tn), lambda i, n: (i, n)),
                  pl.BlockSpec((ds_, tn), lambda i, n: (0, n)),
                  pl.BlockSpec((dp, tn), lambda i, n: (0, n)),
                  pl.BlockSpec((tn, d), lambda i, n: (n, 0)),
                  pl.BlockSpec((1, d), lambda i, n: (0, 0)),
                  cast_in],
        out_specs=(pl.BlockSpec((tm, d), lambda i, n: (i, 0), pipeline_mode=resident), cast_out),
        scratch_shapes=[pltpu.VMEM((tm, d), F32)],
        compiler_params=_params(
            ("arbitrary", "arbitrary"),
            _vmem_limit(tm * (ds_ + dp) * 2 + tm * tn * 8 + (ds_ + dp) * tn * 2 + tn * d * 2
                        + cast_bytes, tm * d * 8, tm * tn * 16)),
        name="mix",
    )(ys, po, prest, prest, x2, wbs, wbp, wo, g, cast_src)


def _ffn_kernel(h_ref, g3_ref, wa_ref, wb_ref, cw_ref, cb_ref, wd_ref, g4_ref, o_ref,
                c_sc, carry_sc, up_sc, *, n_tiles, n_up_steps, tiles_per_seq, keep, conv_row_blocks):
    n_sub, _, sub = wa_ref.shape
    d_ff = n_tiles * sub
    i = pl.program_id(0)
    f = pl.program_id(1)
    tm = h_ref.shape[0]

    @pl.when(f == 0)
    def _():
        def rows(sl):
            c_sc[sl, :] = _rms(h_ref[sl, :], g3_ref[...]).astype(BF16)
            o_ref[sl, :] = jnp.zeros((NORM_ROWS, o_ref.shape[1]), F32)
        _for_row_blocks(tm, rows)

    seq_start = i % tiles_per_seq == 0

    rb = tm // conv_row_blocks
    row_in_head = lax.broadcasted_iota(jnp.int32, (keep, sub), 0)
    n_slots = 2 * n_sub

    def up_products(n_act):
        base = (f % 2) * n_slots
        c = c_sc[...]
        for half, w_ref in enumerate((wa_ref, wb_ref)):
            w = jnp.concatenate([w_ref[s] for s in range(n_act)], axis=1)
            up = jnp.dot(c, w, preferred_element_type=F32)
            for s in range(n_act):
                up_sc[base + 2 * s + half] = up[:, s * sub:(s + 1) * sub]

    def conv(slot, half, tile):
        prev = jnp.where(seq_start, 0.0, carry_sc[half, tile])
        carry_sc[half, tile] = up_sc[slot, tm - keep:tm]
        cols = pl.ds(pl.multiple_of(half * d_ff + tile * sub, sub), sub)
        cw = cw_ref[:, cols]
        cb = cb_ref[:, cols]
        outs = []
        for r0 in range(0, tm, rb):
            up = up_sc[slot, r0:r0 + rb]
            z = cw[0:1] * up
            zp = cw[0:1] * prev
            for k in range(1, CONV_WIDTH):
                r = pltpu.roll(z, 1, 0)
                head = jnp.where(row_in_head == 0, zp[keep - 1:keep], r[0:keep])
                z = cw[k:k + 1] * up + jnp.concatenate([head, r[keep:]], axis=0)
                if k + 1 < CONV_WIDTH:
                    zp = cw[k:k + 1] * prev + pltpu.roll(zp, 1, 0)
            outs.append(z + cb)
            prev = up[rb - keep:]
        return outs

    def down_products(n_act):
        base = ((f + 1) % 2) * n_slots
        fms = []
        for s in range(n_act):
            tile = (f - 1) * n_sub + s
            ua = conv(base + 2 * s, 0, tile)
            ub = conv(base + 2 * s + 1, 1, tile)
            fms.append(jnp.concatenate(
                [(_gelu_tanh(a) * b).astype(BF16) for a, b in zip(ua, ub)], axis=0))
        fm = jnp.concatenate(fms, axis=1)
        o_ref[...] += jnp.dot(fm, wd_ref[0:n_act * sub, :], preferred_element_type=F32)

    tail = n_tiles - (n_up_steps - 1) * n_sub
    assert n_up_steps >= 2

    def run(n_up, n_down):
        def body():
            if n_up:
                up_products(n_up)
            if n_down:
                down_products(n_down)
        return body

    pl.when(f == 0)(run(n_sub, 0))
    pl.when((f > 0) & (f < n_up_steps - 1))(run(n_sub, n_sub))
    pl.when(f == n_up_steps - 1)(run(tail, n_sub))
    pl.when(f == n_up_steps)(run(0, tail))

    @pl.when(f == pl.num_programs(1) - 1)
    def _():
        def rows(sl):
            o_ref[sl, :] = h_ref[sl, :] + _rms(o_ref[sl, :], g4_ref[...])
        _for_row_blocks(tm, rows)


def _ffn(h1, g3, w_up, conv_w, conv_b, w_down, g4, seq_len, tm=512, n_sub=2, keep=8,
         conv_row_blocks=2):
    t, d = h1.shape
    d_ff = w_down.shape[0]
    sub = w_up.shape[2]
    n_tiles = d_ff // sub
    nf = pl.cdiv(n_tiles, n_sub)
    assert w_up.shape == (2 * nf * n_sub, d, sub)
    resident = pl.Buffered(1)

    up_step = lambda f: jnp.minimum(f, nf - 1)
    down_step = lambda f: jnp.clip(f - 1, 0, nf - 1)
    whole = lambda i, f: (0, 0)
    tf = sub * n_sub
    in_specs = [pl.BlockSpec((tm, d), lambda i, f: (i, 0), pipeline_mode=resident),
                pl.BlockSpec((1, d), whole),
                pl.BlockSpec((n_sub, d, sub), lambda i, f: (up_step(f), 0, 0)),
                pl.BlockSpec((n_sub, d, sub), lambda i, f: (nf + up_step(f), 0, 0)),
                pl.BlockSpec(conv_w.shape, whole),
                pl.BlockSpec(conv_b.shape, whole),
                pl.BlockSpec((tf, d), lambda i, f: (down_step(f), 0)),
                pl.BlockSpec((1, d), whole)]
    return pl.pallas_call(
        functools.partial(_ffn_kernel, n_tiles=n_tiles, n_up_steps=nf,
                          tiles_per_seq=seq_len // tm, keep=keep,
                          conv_row_blocks=conv_row_blocks),
        out_shape=jax.ShapeDtypeStruct((t, d), F32),
        grid=(t // tm, nf + 1),
        in_specs=in_specs,
        out_specs=pl.BlockSpec((tm, d), lambda i, f: (i, 0), pipeline_mode=resident),
        scratch_shapes=[pltpu.VMEM((tm, d), BF16),
                        pltpu.VMEM((2, nf * n_sub, keep, sub), F32),
                        pltpu.VMEM((2 * 2 * n_sub, tm, sub), F32)],
        compiler_params=_params(
            ("arbitrary", "arbitrary"),
            _vmem_limit(d * tf * 4 + tf * d * 2 + conv_w.size * 8 + conv_b.size * 32,
                        tm * d * 8 + tm * d * 2 + 2 * nf * tf * keep * 4 + 4 * tm * tf * 4,
                        tm * tf * 32)),
        name="ffn",
    )(h1, g3, w_up, w_up, conv_w, conv_b, w_down, g4)


def kernel(x, norm_pre_mix, w_in, ssm_lambda_re, ssm_lambda_im, ssm_log_step, ssm_b_re, ssm_b_im,
           ssm_c_re, ssm_c_im, ssm_d, ssm_glu_w, ssm_glu_b, pool_w, pool_b, pool_scale,
           w_branch_ssm, w_branch_pool, w_out, norm_post_mix, norm_pre_ffn, w_up, ffn_conv_w,
           ffn_conv_b, w_down, norm_post_ffn):
    bsz, seq_len, d = x.shape
    depth = w_in.shape[0]
    d_ssm = ssm_d.shape[1]
    d_pool = pool_scale.shape[1]
    n_groups = d_ssm // SSM_GROUP
    assert seq_len // CHUNK == 1 << N_SCAN_PASSES
    t = bsz * seq_len
    h = x.reshape(t, d)
    row = lambda v: v.reshape(1, -1)
    for i in range(depth):
        a = _norm(h, row(norm_pre_mix[i]))
        ffn_slots = pl.cdiv(w_down.shape[1] // FFN_SUB, FFN_TILES_PER_STEP) * FFN_TILES_PER_STEP
        prest, w_up_b = _inproj(a, w_in[i], d_ssm, d_pool, w_up[i], 32, FFN_SUB, 2, ffn_slots)
        ut = _inproj_ssm(a, w_in[i], d_ssm, seq_len)
        m, rt, et, wsc = _ssm_prep(ssm_lambda_re[i], ssm_lambda_im[i], ssm_log_step[i],
                                   ssm_b_re[i], ssm_b_im[i], ssm_c_re[i], ssm_c_im[i])
        yt, w_out_b, w_bs_b, w_bp_b = _ssm(
            ut, m, rt, et, wsc, ssm_d[i].reshape(n_groups, SSM_GROUP, 1), seq_len // CHUNK,
            [w_out[i], w_branch_ssm[i], w_branch_pool[i]])
        ys = _glu(yt, ssm_glu_w[i].astype(BF16), row(ssm_glu_b[i]), seq_len)
        po = _pool(prest, pool_w[i].astype(BF16), pool_b[i][:, None, :], row(pool_scale[i]),
                   seq_len, d_pool)
        h, w_down_b = _mix(ys, po, prest, h, w_bs_b, w_bp_b, w_out_b,
                           row(norm_post_mix[i]), d_pool, w_down[i], 128)
        h = _ffn(h, row(norm_pre_ffn[i]), w_up_b, ffn_conv_w[i], row(ffn_conv_b[i]),
                 w_down_b, row(norm_post_ffn[i]), seq_len, n_sub=FFN_TILES_PER_STEP)
    return h.reshape(bsz, seq_len, d)
```

```python
import functools
import math

import jax
import jax.numpy as jnp
from jax import lax
from jax.experimental import pallas as pl
from jax.experimental.pallas import tpu as pltpu

F32 = jnp.float32
BF16 = jnp.bfloat16

EPS = 1e-6
MIN_NEG_REAL = -1e-4
SSM_GROUP = 16
SSM_STATE = 64
POOL_WINDOWS = (2, 4, 8, 16)
CONV_WIDTH = 3
CHUNK = 16
N_SCAN_PASSES = 7
LANES = 128
FFN_SUB = 256
FFN_TILES_PER_STEP = 2

V7X_VMEM_BYTES = 64 * 1024 * 1024
VMEM_RESERVE = 6 * 1024 * 1024
VMEM_CEILING = V7X_VMEM_BYTES - VMEM_RESERVE
VMEM_SLACK = 4 * 1024 * 1024
SMALL_KERNEL_TEMP = 8 * 1024 * 1024


def _vmem_limit(pipelined_bytes, scratch_bytes, temp_bytes):
    return min(2 * pipelined_bytes + scratch_bytes + temp_bytes + VMEM_SLACK, VMEM_CEILING)


def _params(semantics, vmem):
    return pltpu.CompilerParams(dimension_semantics=semantics, vmem_limit_bytes=vmem)


def _rms(xf, g):
    ms = jnp.mean(xf * xf, axis=-1, keepdims=True)
    return xf * lax.rsqrt(ms + EPS) * g


NORM_ROWS = 32


def _for_row_blocks(n_rows, body):
    def step(r, carry):
        body(pl.ds(pl.multiple_of(r * NORM_ROWS, NORM_ROWS), NORM_ROWS))
        return carry
    lax.fori_loop(0, n_rows // NORM_ROWS, step, 0, unroll=2)


def _sigmoid(x):
    return 1.0 / (1.0 + jnp.exp(-x))


def _gelu_tanh(x):
    c = math.sqrt(2.0 / math.pi)
    return 0.5 * x * (1.0 + jnp.tanh(c * (x + 0.044715 * (x * x * x))))


def _norm_kernel(x_ref, g_ref, o_ref):
    def rows(sl):
        o_ref[sl, :] = _rms(x_ref[sl, :], g_ref[...]).astype(BF16)
    _for_row_blocks(x_ref.shape[0], rows)


def _norm(x2, g, tm=512):
    t, d = x2.shape
    return pl.pallas_call(
        _norm_kernel,
        out_shape=jax.ShapeDtypeStruct((t, d), BF16),
        grid=(t // tm,),
        in_specs=[pl.BlockSpec((tm, d), lambda i: (i, 0)),
                  pl.BlockSpec((1, d), lambda i: (0, 0))],
        out_specs=pl.BlockSpec((tm, d), lambda i: (i, 0)),
        compiler_params=_params(("arbitrary",), _vmem_limit(tm * d * 6, 0, tm * d * 8)),
        name="norm",
    )(x2, g)


def _side_cast_specs(src, rows, grid, col_tile=None, col_groups=1, group_slots=None):
    n_blocks = src.shape[0] // rows
    assert src.shape[0] % rows == 0 and n_blocks <= math.prod(grid)

    def block(*g):
        step = g[0]
        for k in range(1, len(grid)):
            step = step * grid[k] + g[k]
        return jnp.minimum(step, n_blocks - 1)

    in_spec = pl.BlockSpec((rows, src.shape[1]), lambda *g: (block(*g), 0))
    if col_tile is None:
        return in_spec, in_spec, jax.ShapeDtypeStruct(src.shape, BF16)
    n_slots = col_groups * group_slots
    out_spec = pl.BlockSpec((n_slots, rows, col_tile), lambda *g: (0, block(*g), 0))
    return in_spec, out_spec, jax.ShapeDtypeStruct((n_slots, src.shape[0], col_tile), BF16)


def _side_cast(src_ref, dst_ref, col_groups=1):
    if len(dst_ref.shape) == 2:
        dst_ref[...] = src_ref[...].astype(BF16)
        return
    n_slots, rows, col_tile = dst_ref.shape
    group_slots = n_slots // col_groups
    group_tiles = src_ref.shape[1] // col_groups // col_tile
    for g in range(col_groups):
        for t in range(group_slots):
            if t < group_tiles:
                c0 = (g * group_tiles + t) * col_tile
                dst_ref[g * group_slots + t] = src_ref[:, c0:c0 + col_tile].astype(BF16)
            else:
                dst_ref[g * group_slots + t] = jnp.zeros((rows, col_tile), BF16)


def _inproj_kernel(a_ref, w_ref, cast_ref, o_ref, cast_o_ref, *, n_plain, row_blocks,
                   cast_groups):
    gate = pl.program_id(1) >= n_plain
    w = w_ref[...].astype(BF16)
    rows = a_ref.shape[0] // row_blocks
    for r0 in range(0, a_ref.shape[0], rows):
        p = jnp.dot(a_ref[r0:r0 + rows, :], w, preferred_element_type=F32)
        o_ref[r0:r0 + rows, :] = jnp.where(gate, _sigmoid(p), p).astype(BF16)
    _side_cast(cast_ref, cast_o_ref, cast_groups)


def _inproj(a, w, col0, n_plain_cols, cast_src, cast_rows, cast_col_tile, cast_groups,
            cast_group_slots, tm=1024, tn=512, row_blocks=4):
    t, d = a.shape
    n = w.shape[1] - col0
    c0 = col0 // tn
    grid = (t // tm, n // tn)
    cast_in, cast_out, cast_shape = _side_cast_specs(cast_src, cast_rows, grid, cast_col_tile,
                                                     cast_groups, cast_group_slots)
    cast_bytes = cast_rows * cast_src.shape[1] * 6
    return pl.pallas_call(
        functools.partial(_inproj_kernel, n_plain=n_plain_cols // tn, row_blocks=row_blocks,
                          cast_groups=cast_groups),
        out_shape=(jax.ShapeDtypeStruct((t, n), BF16), cast_shape),
        grid=grid,
        in_specs=[pl.BlockSpec((tm, d), lambda i, j: (i, 0)),
                  pl.BlockSpec((d, tn), lambda i, j: (0, c0 + j)),
                  cast_in],
        out_specs=(pl.BlockSpec((tm, tn), lambda i, j: (i, j)), cast_out),
        compiler_params=_params(("arbitrary", "arbitrary"),
                                _vmem_limit(tm * d * 2 + d * tn * 4 + tm * tn * 2 + cast_bytes, 0,
                                            d * tn * 2 + tm * tn * 8)),
        name="inproj",
    )(a, w, cast_src)


def _inproj_ssm_kernel(a_ref, w_ref, o_ref, p_sc, *, row_blocks):
    chunks = o_ref.shape[-1]
    n_slabs = p_sc.shape[0]
    w = w_ref[...].astype(BF16)
    rows = a_ref.shape[0] // row_blocks
    for r0 in range(0, a_ref.shape[0], rows):
        p = jnp.dot(a_ref[r0:r0 + rows, :], w, preferred_element_type=F32)
        for s in range(n_slabs):
            p_sc[s, r0:r0 + rows] = p[:, s * LANES:(s + 1) * LANES]
    gps = LANES // SSM_GROUP
    for q in range(CHUNK):
        for s in range(n_slabs):
            blk = p_sc[s, pl.ds(q, chunks, stride=CHUNK), :]
            o_ref[s * gps:(s + 1) * gps, q] = (
                blk.T.reshape(gps, SSM_GROUP, chunks).astype(BF16))


def _inproj_ssm(a, w, n, seq_len, tn=256, row_blocks=8):
    t, d = a.shape
    g = n // SSM_GROUP
    chunks = seq_len // CHUNK
    return pl.pallas_call(
        functools.partial(_inproj_ssm_kernel, row_blocks=row_blocks),
        out_shape=jax.ShapeDtypeStruct((g, CHUNK, SSM_GROUP, t // CHUNK), BF16),
        grid=(t // seq_len, n // tn),
        in_specs=[pl.BlockSpec((seq_len, d), lambda bi, j: (bi, 0)),
                  pl.BlockSpec((d, tn), lambda bi, j: (0, j))],
        out_specs=pl.BlockSpec((tn // SSM_GROUP, CHUNK, SSM_GROUP, chunks),
                               lambda bi, j: (j, 0, 0, bi)),
        scratch_shapes=[pltpu.VMEM((tn // LANES, seq_len, LANES), F32)],
        compiler_params=_params(("arbitrary", "arbitrary"),
                                _vmem_limit(seq_len * d * 2 + d * tn * 4 + tn * seq_len * 2,
                                            seq_len * tn * 4, d * tn * 2 + seq_len * tn * 16)),
        name="inproj_ssm",
    )(a, w)


def _ssm_prep_kernel(lre_ref, lim_ref, ls_ref, btr_ref, bti_ref, cr_ref, ci_ref, cast_ref,
                     m_ref, rt_ref, et_ref, w_ref, cast_o_ref, *, groups):
    cast_o_ref[...] = cast_ref[...].astype(BF16)
    p2 = 2 * SSM_STATE
    cq = CHUNK * SSM_GROUP
    lane = lax.broadcasted_iota(jnp.int32, (1, p2), 1)
    is_re = lane < SSM_STATE
    sgn = jnp.where(is_re, 1.0, -1.0).astype(F32)
    lane_blk = lax.broadcasted_iota(jnp.int32, (cq, cq), 1) // SSM_GROUP
    row8 = lax.broadcasted_iota(jnp.int32, (8, p2), 0)
    is_re8 = lax.broadcasted_iota(jnp.int32, (8, p2), 1) < SSM_STATE

    for gi in range(groups):
        lr = jnp.minimum(lre_ref[gi], MIN_NEG_REAL)
        li = lim_ref[gi]
        dt = jnp.exp(ls_ref[gi])
        mag = jnp.exp(lr * dt)
        ang = li * dt
        ar = mag * jnp.cos(ang)
        ai = mag * jnp.sin(ang)
        nr = ar - 1.0
        ni = ai
        den = lr * lr + li * li
        f_re = (nr * lr + ni * li) / den
        f_im = (ni * lr - nr * li) / den
        btr = btr_ref[gi]
        bti = bti_ref[gi]
        bbr = f_re * btr - f_im * bti
        bbi = f_re * bti + f_im * btr
        cr = cr_ref[gi]
        ci = ci_ref[gi]

        pa = [jnp.where(is_re, 1.0, 0.0).astype(F32)]
        pb = [jnp.where(is_re, 0.0, 1.0).astype(F32)]
        for _ in range(CHUNK):
            a_prev, b_prev = pa[-1], pb[-1]
            pa.append(a_prev * ar + b_prev * ai)
            pb.append(b_prev * ar - a_prev * ai)

        gk = [sgn * (cr * pa[k] + ci * pb[k]) for k in range(CHUNK + 1)]
        fs = jnp.concatenate(gk[:CHUNK], axis=0)
        et = jnp.concatenate(gk[1:], axis=0)
        rtt = jnp.concatenate(
            [bbr * pa[CHUNK - 1 - q] + bbi * pb[CHUNK - 1 - q] for q in range(CHUNK)], axis=0)
        bst = jnp.where(is_re, bbr, bbi)
        bst_t = jnp.concatenate([bst] * CHUNK, axis=0)
        kw = lax.dot_general(fs, bst_t, (((1,), (1,)), ((), ())),
                             precision=lax.Precision.HIGHEST, preferred_element_type=F32)
        m = jnp.zeros((cq, cq), F32)
        for q in range(CHUNK):
            if q == 0:
                shifted = kw
            else:
                shifted = jnp.concatenate(
                    [jnp.zeros((q * SSM_GROUP, cq), F32), kw[:cq - q * SSM_GROUP]], axis=0)
            m = jnp.where(lane_blk == q, shifted, m)
        m_ref[gi] = m.astype(BF16)
        rt_ref[gi] = rtt.T.astype(BF16)
        et_ref[gi] = et.astype(BF16)

        wa = jnp.broadcast_to(pa[CHUNK], (8, p2))
        rows = jnp.zeros((8, p2), F32)
        for k in range(N_SCAN_PASSES):
            rows = jnp.where(row8 == k, wa, rows)
            swapped = pltpu.roll(wa, SSM_STATE, 1)
            re2 = jnp.where(is_re8, wa, swapped)
            im2 = jnp.where(is_re8, swapped, wa)
            wb = jnp.where(is_re8, -swapped, swapped)
            wa = wa * re2 + wb * im2
        full = jnp.concatenate([rows, jnp.zeros((p2 - 8, p2), F32)], axis=0)
        w_ref[gi] = full.T


def _ssm_prep(lam_re, lam_im, log_step, b_re, b_im, c_re, c_im, cast_src, groups=8):
    g, p = lam_re.shape
    p2 = 2 * p
    cq = CHUNK * SSM_GROUP
    dup = lambda v: jnp.concatenate([v, v], axis=-1)
    lre = dup(lam_re)[:, None, :]
    lim = dup(lam_im)[:, None, :]
    ls = jnp.broadcast_to(log_step[:, None, None], (g, 1, p2))
    btr = dup(jnp.swapaxes(b_re, 1, 2))
    bti = dup(jnp.swapaxes(b_im, 1, 2))
    cr = dup(c_re)
    ci = dup(c_im)
    row = pl.BlockSpec((groups, 1, p2), lambda i: (i, 0, 0))
    mat = pl.BlockSpec((groups, SSM_GROUP, p2), lambda i: (i, 0, 0))
    grid = (g // groups,)
    cast_in, cast_out, cast_shape = _side_cast_specs(cast_src, cast_src.shape[0] // grid[0], grid)
    return pl.pallas_call(
        functools.partial(_ssm_prep_kernel, groups=groups),
        out_shape=(jax.ShapeDtypeStruct((g, cq, cq), BF16),
                   jax.ShapeDtypeStruct((g, p2, cq), BF16),
                   jax.ShapeDtypeStruct((g, cq, p2), BF16),
                   jax.ShapeDtypeStruct((g, p2, p2), F32),
                   cast_shape),
        grid=grid,
        in_specs=[row, row, row, mat, mat, mat, mat, cast_in],
        out_specs=(pl.BlockSpec((groups, cq, cq), lambda i: (i, 0, 0)),
                   pl.BlockSpec((groups, p2, cq), lambda i: (i, 0, 0)),
                   pl.BlockSpec((groups, cq, p2), lambda i: (i, 0, 0)),
                   pl.BlockSpec((groups, p2, p2), lambda i: (i, 0, 0)),
                   cast_out),
        compiler_params=_params(("arbitrary",),
                                _vmem_limit(groups * cq * cq * 4 + cast_src.size // grid[0] * 6, 0,
                                            SMALL_KERNEL_TEMP)),
        name="ssm_prep",
    )(lre, lim, ls, btr, bti, cr, ci, cast_src)


def _ssm_kernel(*refs, groups, chunks_per_seq, n_cast):
    u_ref, m_ref, rt_ref, et_ref, w_ref, d_ref = refs[:6]
    o_ref = refs[6 + n_cast]
    for src, dst in zip(refs[6:6 + n_cast], refs[7 + n_cast:]):
        dst[...] = src[...].astype(BF16)
    cq = CHUNK * SSM_GROUP
    nc = u_ref.shape[-1]
    pos = lax.broadcasted_iota(jnp.int32, (SSM_STATE, nc), 1) % chunks_per_seq
    for gi in range(groups):
        x3 = u_ref[gi]
        x = x3.reshape(cq, nc)
        y = jnp.dot(m_ref[gi], x, preferred_element_type=F32)
        r = jnp.dot(rt_ref[gi], x, preferred_element_type=F32)
        s_re, s_im = r[:SSM_STATE], r[SSM_STATE:]
        for k in range(N_SCAN_PASSES):
            dist = 1 << k
            wr = w_ref[gi, 0:SSM_STATE, k:k + 1]
            wi = w_ref[gi, SSM_STATE:2 * SSM_STATE, k:k + 1]
            keep = pos >= dist
            sh_re = jnp.where(keep, pltpu.roll(s_re, dist, 1), 0.0)
            sh_im = jnp.where(keep, pltpu.roll(s_im, dist, 1), 0.0)
            s_re, s_im = (s_re + wr * sh_re - wi * sh_im,
                          s_im + wr * sh_im + wi * sh_re)
        first = pos >= 1
        sp = jnp.concatenate([jnp.where(first, pltpu.roll(s_re, 1, 1), 0.0),
                              jnp.where(first, pltpu.roll(s_im, 1, 1), 0.0)], axis=0)
        y = y + jnp.dot(et_ref[gi], sp.astype(BF16), preferred_element_type=F32)
        y3 = y.reshape(CHUNK, SSM_GROUP, nc) + d_ref[gi][None] * x3.astype(F32)
        o_ref[gi] = _gelu_tanh(y3).astype(BF16)


def _ssm(ut, m, rt, et, w, d, chunks_per_seq, cast_srcs, groups=4):
    g, q, j, nc = ut.shape
    cq = q * j
    p2 = rt.shape[1]
    grid = (g // groups,)
    blk = lambda *s: pl.BlockSpec((groups,) + s, lambda i: (i,) + (0,) * len(s))
    casts = [_side_cast_specs(src, src.shape[0] // grid[0], grid) for src in cast_srcs]
    cast_bytes = sum(src.size // grid[0] * 6 for src in cast_srcs)
    return pl.pallas_call(
        functools.partial(_ssm_kernel, groups=groups, chunks_per_seq=chunks_per_seq,
                          n_cast=len(cast_srcs)),
        out_shape=(jax.ShapeDtypeStruct(ut.shape, BF16), *[c[2] for c in casts]),
        grid=grid,
        in_specs=[blk(q, j, nc), blk(cq, cq), blk(p2, cq), blk(cq, p2), blk(p2, p2), blk(j, 1),
                  *[c[0] for c in casts]],
        out_specs=(blk(q, j, nc), *[c[1] for c in casts]),
        compiler_params=_params(("arbitrary",),
                                _vmem_limit(groups * cq * nc * 8 + cast_bytes, 0,
                                            SMALL_KERNEL_TEMP)),
        name="ssm",
    )(ut, m, rt, et, w, d, *cast_srcs)


def _glu_kernel(y_ref, w_ref, b_ref, o_ref, y_sc, *, tn):
    n = pl.program_id(1)
    chunks = y_ref.shape[-1]
    n_slabs = y_sc.shape[0]

    @pl.when(n == 0)
    def _():
        for q in range(CHUNK):
            yq = y_ref[:, q]
            yq = yq.reshape(yq.shape[0] * yq.shape[1], chunks).astype(F32).T
            for s in range(n_slabs):
                y_sc[s, pl.ds(q, chunks, stride=CHUNK), :] = yq[:, s * LANES:(s + 1) * LANES]

    y = jnp.concatenate([y_sc[s] for s in range(n_slabs)], axis=1).astype(BF16)
    z = jnp.dot(y, w_ref[...], preferred_element_type=F32) + b_ref[...]
    per_tile = tn // LANES
    yn = jnp.concatenate([y_sc[n * per_tile + s] for s in range(per_tile)], axis=1)
    o_ref[...] = (yn * _sigmoid(z)).astype(BF16)


def _glu(yt, w, b, seq_len, tn=512):
    g, q, j, nc = yt.shape
    d = g * j
    chunks = seq_len // q
    t = nc * q
    return pl.pallas_call(
        functools.partial(_glu_kernel, tn=tn),
        out_shape=jax.ShapeDtypeStruct((t, d), BF16),
        grid=(nc // chunks, d // tn),
        in_specs=[pl.BlockSpec((g, q, j, chunks), lambda bi, n: (0, 0, 0, bi)),
                  pl.BlockSpec((d, tn), lambda bi, n: (0, n)),
                  pl.BlockSpec((1, tn), lambda bi, n: (0, n))],
        out_specs=pl.BlockSpec((seq_len, tn), lambda bi, n: (bi, n)),
        scratch_shapes=[pltpu.VMEM((d // LANES, seq_len, LANES), F32)],
        compiler_params=_params(("arbitrary", "arbitrary"),
                                _vmem_limit((seq_len * d + d * tn + seq_len * tn) * 2,
                                            seq_len * d * 4, seq_len * d * 2 + seq_len * tn * 12)),
        name="glu",
    )(yt, w, b)


def _pool_kernel(u_ref, w_ref, b_ref, s_ref, o_ref, ext_sc, *, tiles_per_seq, halo):
    i = pl.program_id(0)
    tm = u_ref.shape[0]
    gp = w_ref.shape[1]

    @pl.when(i % tiles_per_seq == 0)
    def _():
        ext_sc[0:halo] = jnp.zeros((halo, ext_sc.shape[1]), BF16)

    @pl.when(i % tiles_per_seq != 0)
    def _():
        ext_sc[0:halo] = ext_sc[tm:tm + halo]

    ext_sc[halo:halo + tm] = u_ref[...]

    t_loc = lax.broadcasted_iota(jnp.int32, (tm, tm + halo), 0)
    s_loc = lax.broadcasted_iota(jnp.int32, (tm, tm + halo), 1)
    lag = t_loc + halo - s_loc
    t_seq = (i % tiles_per_seq) * tm + lax.broadcasted_iota(jnp.int32, (tm, 1), 0)
    for gi, win in enumerate(POOL_WINDOWS):
        band = jnp.where((lag >= 0) & (lag < win), 1.0, 0.0).astype(BF16)
        cols = slice(gi * gp, (gi + 1) * gp)
        wsum = jnp.dot(band, ext_sc[:, cols], preferred_element_type=F32)
        cnt = jnp.minimum(t_seq + 1, win).astype(F32)
        z = wsum / cnt - u_ref[:, cols].astype(F32)
        z = jnp.dot(z.astype(BF16), w_ref[gi], preferred_element_type=F32) + b_ref[gi]
        o_ref[:, cols] = (z * s_ref[:, cols]).astype(BF16)


def _pool(prest, w, b, scale, seq_len, d_pool, tm=512, halo=128):
    t = prest.shape[0]
    ng, gp, _ = w.shape
    return pl.pallas_call(
        functools.partial(_pool_kernel, tiles_per_seq=seq_len // tm, halo=halo),
        out_shape=jax.ShapeDtypeStruct((t, d_pool), BF16),
        grid=(t // tm,),
        in_specs=[pl.BlockSpec((tm, d_pool), lambda i: (i, 0)),
                  pl.BlockSpec((ng, gp, gp), lambda i: (0, 0, 0)),
                  pl.BlockSpec((ng, 1, gp), lambda i: (0, 0, 0)),
                  pl.BlockSpec((1, d_pool), lambda i: (0, 0))],
        out_specs=pl.BlockSpec((tm, d_pool), lambda i: (i, 0)),
        scratch_shapes=[pltpu.VMEM((tm + halo, d_pool), BF16)],
        compiler_params=_params(("arbitrary",),
                                _vmem_limit(tm * d_pool * 4 + ng * gp * gp * 2,
                                            (tm + halo) * d_pool * 2, SMALL_KERNEL_TEMP)),
        name="pool",
    )(prest, w, b, scale)


def _mix_kernel(ys_ref, po_ref, gs_ref, gp_ref, x_ref, wbs_ref, wbp_ref, wo_ref, g_ref,
                o_ref, x_sc, *, sub):
    n = pl.program_id(1)
    tn = x_ref.shape[1]
    x_sc[:, pl.ds(pl.multiple_of(n * tn, tn), tn)] = x_ref[...]

    @pl.when(n == 0)
    def _():
        o_ref[...] = jnp.zeros_like(o_ref)

    subs = [slice(s * sub, (s + 1) * sub) for s in range(wo_ref.shape[0] // sub)]
    branch = [(jnp.dot(ys_ref[...], wbs_ref[:, cols], preferred_element_type=F32),
               jnp.dot(po_ref[...], wbp_ref[:, cols], preferred_element_type=F32))
              for cols in subs]
    merged = [(gs_ref[:, cols].astype(F32) * ms + gp_ref[:, cols].astype(F32) * mp).astype(BF16)
              for cols, (ms, mp) in zip(subs, branch)]
    o_ref[...] += jnp.dot(jnp.concatenate(merged, axis=1), wo_ref[...],
                          preferred_element_type=F32)

    @pl.when(n == pl.num_programs(1) - 1)
    def _():
        def rows(sl):
            o_ref[sl, :] = x_sc[sl, :] + _rms(o_ref[sl, :], g_ref[...])
        _for_row_blocks(o_ref.shape[0], rows)


def _mix(ys, po, prest, x2, wbs, wbp, wo, g, gate_col0, tm=512, tn=512, sub=256):
    t, d = x2.shape
    ds_ = ys.shape[1]
    dp = po.shape[1]
    c0 = gate_col0 // tn
    c1 = (gate_col0 + d) // tn
    resident = pl.Buffered(1)
    return pl.pallas_call(
        functools.partial(_mix_kernel, sub=sub),
        out_shape=jax.ShapeDtypeStruct((t, d), F32),
        grid=(t // tm, d // tn),
        in_specs=[pl.BlockSpec((tm, ds_), lambda i, n: (i, 0)),
                  pl.BlockSpec((tm, dp), lambda i, n: (i, 0)),
                  pl.BlockSpec((tm, tn), lambda i, n: (i, c0 + n)),
                  pl.BlockSpec((tm, tn), lambda i, n: (i, c1 + n)),
                  pl.BlockSpec((tm, tn), lambda i, n: (i, n)),
                  pl.BlockSpec((ds_, tn), lambda i, n: (0, n)),
                  pl.BlockSpec((dp, tn), lambda i, n: (0, n)),
                  pl.BlockSpec((tn, d), lambda i, n: (n, 0)),
                  pl.BlockSpec((1, d), lambda i, n: (0, 0))],
        out_specs=pl.BlockSpec((tm, d), lambda i, n: (i, 0), pipeline_mode=resident),
        scratch_shapes=[pltpu.VMEM((tm, d), F32)],
        compiler_params=_params(
            ("arbitrary", "arbitrary"),
            _vmem_limit(tm * (ds_ + dp) * 2 + tm * tn * 8 + (ds_ + dp) * tn * 2 + tn * d * 2,
                        tm * d * 8, tm * tn * 16)),
        name="mix",
    )(ys, po, prest, prest, x2, wbs, wbp, wo, g)


def _ffn_kernel(h_ref, g3_ref, wa_ref, wb_ref, cw_ref, cb_ref, wd_ref, g4_ref, o_ref,
                c_sc, carry_sc, up_sc, *, n_tiles, n_up_steps, tiles_per_seq, keep, conv_row_blocks):
    n_sub, _, sub = wa_ref.shape
    d_ff = n_tiles * sub
    i = pl.program_id(0)
    f = pl.program_id(1)
    tm = h_ref.shape[0]

    @pl.when(f == 0)
    def _():
        def rows(sl):
            c_sc[sl, :] = _rms(h_ref[sl, :], g3_ref[...]).astype(BF16)
            o_ref[sl, :] = jnp.zeros((NORM_ROWS, o_ref.shape[1]), F32)
        _for_row_blocks(tm, rows)

    seq_start = i % tiles_per_seq == 0

    rb = tm // conv_row_blocks
    row_in_head = lax.broadcasted_iota(jnp.int32, (keep, sub), 0)
    n_slots = 2 * n_sub

    def up_products(n_act):
        base = (f % 2) * n_slots
        c = c_sc[...]
        for half, w_ref in enumerate((wa_ref, wb_ref)):
            w = jnp.concatenate([w_ref[s] for s in range(n_act)], axis=1)
            up = jnp.dot(c, w, preferred_element_type=F32)
            for s in range(n_act):
                up_sc[base + 2 * s + half] = up[:, s * sub:(s + 1) * sub]

    def conv(slot, half, tile):
        prev = jnp.where(seq_start, 0.0, carry_sc[half, tile])
        carry_sc[half, tile] = up_sc[slot, tm - keep:tm]
        cols = pl.ds(pl.multiple_of(half * d_ff + tile * sub, sub), sub)
        cw = cw_ref[:, cols]
        cb = cb_ref[:, cols]
        outs = []
        for r0 in range(0, tm, rb):
            up = up_sc[slot, r0:r0 + rb]
            z = cw[0:1] * up
            zp = cw[0:1] * prev
            for k in range(1, CONV_WIDTH):
                r = pltpu.roll(z, 1, 0)
                head = jnp.where(row_in_head == 0, zp[keep - 1:keep], r[0:keep])
                z = cw[k:k + 1] * up + jnp.concatenate([head, r[keep:]], axis=0)
                if k + 1 < CONV_WIDTH:
                    zp = cw[k:k + 1] * prev + pltpu.roll(zp, 1, 0)
            outs.append(z + cb)
            prev = up[rb - keep:]
        return outs

    def down_products(n_act):
        base = ((f + 1) % 2) * n_slots
        fms = []
        for s in range(n_act):
            tile = (f - 1) * n_sub + s
            ua = conv(base + 2 * s, 0, tile)
            ub = conv(base + 2 * s + 1, 1, tile)
            fms.append(jnp.concatenate(
                [(_gelu_tanh(a) * b).astype(BF16) for a, b in zip(ua, ub)], axis=0))
        fm = jnp.concatenate(fms, axis=1)
        o_ref[...] += jnp.dot(fm, wd_ref[0:n_act * sub, :], preferred_element_type=F32)

    tail = n_tiles - (n_up_steps - 1) * n_sub
    assert n_up_steps >= 2

    def run(n_up, n_down):
        def body():
            if n_up:
                up_products(n_up)
            if n_down:
                down_products(n_down)
        return body

    pl.when(f == 0)(run(n_sub, 0))
    pl.when((f > 0) & (f < n_up_steps - 1))(run(n_sub, n_sub))
    pl.when(f == n_up_steps - 1)(run(tail, n_sub))
    pl.when(f == n_up_steps)(run(0, tail))

    @pl.when(f == pl.num_programs(1) - 1)
    def _():
        def rows(sl):
            o_ref[sl, :] = h_ref[sl, :] + _rms(o_ref[sl, :], g4_ref[...])
        _for_row_blocks(tm, rows)


def _ffn(h1, g3, w_up, conv_w, conv_b, w_down, g4, seq_len, tm=512, n_sub=2, keep=8,
         conv_row_blocks=2):
    t, d = h1.shape
    d_ff = w_down.shape[0]
    sub = w_up.shape[2]
    n_tiles = d_ff // sub
    nf = pl.cdiv(n_tiles, n_sub)
    assert w_up.shape == (2 * nf * n_sub, d, sub)
    resident = pl.Buffered(1)

    up_step = lambda f: jnp.minimum(f, nf - 1)
    down_step = lambda f: jnp.clip(f - 1, 0, nf - 1)
    whole = lambda i, f: (0, 0)
    tf = sub * n_sub
    in_specs = [pl.BlockSpec((tm, d), lambda i, f: (i, 0), pipeline_mode=resident),
                pl.BlockSpec((1, d), whole),
                pl.BlockSpec((n_sub, d, sub), lambda i, f: (up_step(f), 0, 0)),
                pl.BlockSpec((n_sub, d, sub), lambda i, f: (nf + up_step(f), 0, 0)),
                pl.BlockSpec(conv_w.shape, whole),
                pl.BlockSpec(conv_b.shape, whole),
                pl.BlockSpec((tf, d), lambda i, f: (down_step(f), 0)),
                pl.BlockSpec((1, d), whole)]
    return pl.pallas_call(
        functools.partial(_ffn_kernel, n_tiles=n_tiles, n_up_steps=nf,
                          tiles_per_seq=seq_len // tm, keep=keep,
                          conv_row_blocks=conv_row_blocks),
        out_shape=jax.ShapeDtypeStruct((t, d), F32),
        grid=(t // tm, nf + 1),
        in_specs=in_specs,
        out_specs=pl.BlockSpec((tm, d), lambda i, f: (i, 0), pipeline_mode=resident),
        scratch_shapes=[pltpu.VMEM((tm, d), BF16),
                        pltpu.VMEM((2, nf * n_sub, keep, sub), F32),
                        pltpu.VMEM((2 * 2 * n_sub, tm, sub), F32)],
        compiler_params=_params(
            ("arbitrary", "arbitrary"),
            _vmem_limit(d * tf * 4 + tf * d * 2 + conv_w.size * 8 + conv_b.size * 32,
                        tm * d * 8 + tm * d * 2 + 2 * nf * tf * keep * 4 + 4 * tm * tf * 4,
                        tm * tf * 32)),
        name="ffn",
    )(h1, g3, w_up, w_up, conv_w, conv_b, w_down, g4)


def kernel(x, norm_pre_mix, w_in, ssm_lambda_re, ssm_lambda_im, ssm_log_step, ssm_b_re, ssm_b_im,
           ssm_c_re, ssm_c_im, ssm_d, ssm_glu_w, ssm_glu_b, pool_w, pool_b, pool_scale,
           w_branch_ssm, w_branch_pool, w_out, norm_post_mix, norm_pre_ffn, w_up, ffn_conv_w,
           ffn_conv_b, w_down, norm_post_ffn):
    bsz, seq_len, d = x.shape
    depth = w_in.shape[0]
    d_ssm = ssm_d.shape[1]
    d_pool = pool_scale.shape[1]
    n_groups = d_ssm // SSM_GROUP
    assert seq_len // CHUNK == 1 << N_SCAN_PASSES
    t = bsz * seq_len
    h = x.reshape(t, d)
    row = lambda v: v.reshape(1, -1)
    for i in range(depth):
        a = _norm(h, row(norm_pre_mix[i]))
        ffn_slots = pl.cdiv(w_down.shape[1] // FFN_SUB, FFN_TILES_PER_STEP) * FFN_TILES_PER_STEP
        prest, w_up_b = _inproj(a, w_in[i], d_ssm, d_pool, w_up[i], 32, FFN_SUB, 2, ffn_slots)
        ut = _inproj_ssm(a, w_in[i], d_ssm, seq_len)
        m, rt, et, wsc, w_down_b = _ssm_prep(
            ssm_lambda_re[i], ssm_lambda_im[i], ssm_log_step[i], ssm_b_re[i], ssm_b_im[i],
            ssm_c_re[i], ssm_c_im[i], w_down[i])
        yt, w_out_b, w_bs_b, w_bp_b = _ssm(
            ut, m, rt, et, wsc, ssm_d[i].reshape(n_groups, SSM_GROUP, 1), seq_len // CHUNK,
            [w_out[i], w_branch_ssm[i], w_branch_pool[i]])
        ys = _glu(yt, ssm_glu_w[i].astype(BF16), row(ssm_glu_b[i]), seq_len)
        po = _pool(prest, pool_w[i].astype(BF16), pool_b[i][:, None, :], row(pool_scale[i]),
                   seq_len, d_pool)
        h = _mix(ys, po, prest, h, w_bs_b, w_bp_b, w_out_b, row(norm_post_mix[i]), d_pool)
        h = _ffn(h, row(norm_pre_ffn[i]), w_up_b, ffn_conv_w[i], row(ffn_conv_b[i]),
                 w_down_b, row(norm_post_ffn[i]), seq_len, n_sub=FFN_TILES_PER_STEP)
    return h.reshape(bsz, seq_len, d)
```

```python
import functools
import math

import jax
import jax.numpy as jnp
from jax import lax
from jax.experimental import pallas as pl
from jax.experimental.pallas import tpu as pltpu

F32 = jnp.float32
BF16 = jnp.bfloat16

EPS = 1e-6
MIN_NEG_REAL = -1e-4
SSM_GROUP = 16
SSM_STATE = 64
POOL_WINDOWS = (2, 4, 8, 16)
CONV_WIDTH = 3
CHUNK = 16
N_SCAN_PASSES = 7
LANES = 128

V7X_VMEM_BYTES = 64 * 1024 * 1024
VMEM_RESERVE = 6 * 1024 * 1024
VMEM_CEILING = V7X_VMEM_BYTES - VMEM_RESERVE
VMEM_SLACK = 4 * 1024 * 1024
SMALL_KERNEL_TEMP = 8 * 1024 * 1024


def _vmem_limit(pipelined_bytes, scratch_bytes, temp_bytes):
    return min(2 * pipelined_bytes + scratch_bytes + temp_bytes + VMEM_SLACK, VMEM_CEILING)


def _params(semantics, vmem):
    return pltpu.CompilerParams(dimension_semantics=semantics, vmem_limit_bytes=vmem)


def _rms(xf, g):
    ms = jnp.mean(xf * xf, axis=-1, keepdims=True)
    return xf * lax.rsqrt(ms + EPS) * g


NORM_ROWS = 32


def _for_row_blocks(n_rows, body):
    def step(r, carry):
        body(pl.ds(pl.multiple_of(r * NORM_ROWS, NORM_ROWS), NORM_ROWS))
        return carry
    lax.fori_loop(0, n_rows // NORM_ROWS, step, 0, unroll=2)


def _sigmoid(x):
    return 1.0 / (1.0 + jnp.exp(-x))


def _gelu_tanh(x):
    c = math.sqrt(2.0 / math.pi)
    return 0.5 * x * (1.0 + jnp.tanh(c * (x + 0.044715 * (x * x * x))))


def _norm_kernel(x_ref, g_ref, o_ref):
    def rows(sl):
        o_ref[sl, :] = _rms(x_ref[sl, :], g_ref[...]).astype(BF16)
    _for_row_blocks(x_ref.shape[0], rows)


def _norm(x2, g, tm=512):
    t, d = x2.shape
    return pl.pallas_call(
        _norm_kernel,
        out_shape=jax.ShapeDtypeStruct((t, d), BF16),
        grid=(t // tm,),
        in_specs=[pl.BlockSpec((tm, d), lambda i: (i, 0)),
                  pl.BlockSpec((1, d), lambda i: (0, 0))],
        out_specs=pl.BlockSpec((tm, d), lambda i: (i, 0)),
        compiler_params=_params(("arbitrary",), _vmem_limit(tm * d * 6, 0, tm * d * 8)),
        name="norm",
    )(x2, g)


def _side_cast_specs(src, rows, grid):
    n_blocks = src.shape[0] // rows
    assert src.shape[0] % rows == 0 and n_blocks <= math.prod(grid)

    def index(*g):
        step = g[0]
        for k in range(1, len(grid)):
            step = step * grid[k] + g[k]
        return (jnp.minimum(step, n_blocks - 1), 0)

    spec = pl.BlockSpec((rows, src.shape[1]), index)
    return spec, spec, jax.ShapeDtypeStruct(src.shape, BF16)


def _inproj_kernel(a_ref, w_ref, cast_ref, o_ref, cast_o_ref, *, n_plain, row_blocks):
    gate = pl.program_id(1) >= n_plain
    w = w_ref[...].astype(BF16)
    rows = a_ref.shape[0] // row_blocks
    for r0 in range(0, a_ref.shape[0], rows):
        p = jnp.dot(a_ref[r0:r0 + rows, :], w, preferred_element_type=F32)
        o_ref[r0:r0 + rows, :] = jnp.where(gate, _sigmoid(p), p).astype(BF16)
    cast_o_ref[...] = cast_ref[...].astype(BF16)


def _inproj(a, w, col0, n_plain_cols, cast_src, cast_rows, tm=1024, tn=512, row_blocks=2):
    t, d = a.shape
    n = w.shape[1] - col0
    c0 = col0 // tn
    grid = (t // tm, n // tn)
    cast_in, cast_out, cast_shape = _side_cast_specs(cast_src, cast_rows, grid)
    cast_bytes = cast_rows * cast_src.shape[1] * 6
    return pl.pallas_call(
        functools.partial(_inproj_kernel, n_plain=n_plain_cols // tn, row_blocks=row_blocks),
        out_shape=(jax.ShapeDtypeStruct((t, n), BF16), cast_shape),
        grid=grid,
        in_specs=[pl.BlockSpec((tm, d), lambda i, j: (i, 0)),
                  pl.BlockSpec((d, tn), lambda i, j: (0, c0 + j)),
                  cast_in],
        out_specs=(pl.BlockSpec((tm, tn), lambda i, j: (i, j)), cast_out),
        compiler_params=_params(("arbitrary", "arbitrary"),
                                _vmem_limit(tm * d * 2 + d * tn * 4 + tm * tn * 2 + cast_bytes, 0,
                                            d * tn * 2 + tm * tn * 8)),
        name="inproj",
    )(a, w, cast_src)


def _inproj_ssm_kernel(a_ref, w_ref, o_ref, p_sc, *, row_blocks):
    chunks = o_ref.shape[-1]
    n_slabs = p_sc.shape[0]
    w = w_ref[...].astype(BF16)
    rows = a_ref.shape[0] // row_blocks
    for r0 in range(0, a_ref.shape[0], rows):
        p = jnp.dot(a_ref[r0:r0 + rows, :], w, preferred_element_type=F32)
        for s in range(n_slabs):
            p_sc[s, r0:r0 + rows] = p[:, s * LANES:(s + 1) * LANES]
    gps = LANES // SSM_GROUP
    for q in range(CHUNK):
        for s in range(n_slabs):
            blk = p_sc[s, pl.ds(q, chunks, stride=CHUNK), :]
            o_ref[s * gps:(s + 1) * gps, q] = (
                blk.T.reshape(gps, SSM_GROUP, chunks).astype(BF16))


def _inproj_ssm(a, w, n, seq_len, tn=256, row_blocks=4):
    t, d = a.shape
    g = n // SSM_GROUP
    chunks = seq_len // CHUNK
    return pl.pallas_call(
        functools.partial(_inproj_ssm_kernel, row_blocks=row_blocks),
        out_shape=jax.ShapeDtypeStruct((g, CHUNK, SSM_GROUP, t // CHUNK), BF16),
        grid=(t // seq_len, n // tn),
        in_specs=[pl.BlockSpec((seq_len, d), lambda bi, j: (bi, 0)),
                  pl.BlockSpec((d, tn), lambda bi, j: (0, j))],
        out_specs=pl.BlockSpec((tn // SSM_GROUP, CHUNK, SSM_GROUP, chunks),
                               lambda bi, j: (j, 0, 0, bi)),
        scratch_shapes=[pltpu.VMEM((tn // LANES, seq_len, LANES), F32)],
        compiler_params=_params(("arbitrary", "arbitrary"),
                                _vmem_limit(seq_len * d * 2 + d * tn * 4 + tn * seq_len * 2,
                                            seq_len * tn * 4, d * tn * 2 + seq_len * tn * 16)),
        name="inproj_ssm",
    )(a, w)


def _ssm_prep_kernel(lre_ref, lim_ref, ls_ref, btr_ref, bti_ref, cr_ref, ci_ref,
                     m_ref, rt_ref, et_ref, w_ref, *, groups):
    p2 = 2 * SSM_STATE
    cq = CHUNK * SSM_GROUP
    lane = lax.broadcasted_iota(jnp.int32, (1, p2), 1)
    is_re = lane < SSM_STATE
    sgn = jnp.where(is_re, 1.0, -1.0).astype(F32)
    lane_blk = lax.broadcasted_iota(jnp.int32, (cq, cq), 1) // SSM_GROUP
    row8 = lax.broadcasted_iota(jnp.int32, (8, p2), 0)
    is_re8 = lax.broadcasted_iota(jnp.int32, (8, p2), 1) < SSM_STATE

    for gi in range(groups):
        lr = jnp.minimum(lre_ref[gi], MIN_NEG_REAL)
        li = lim_ref[gi]
        dt = jnp.exp(ls_ref[gi])
        mag = jnp.exp(lr * dt)
        ang = li * dt
        ar = mag * jnp.cos(ang)
        ai = mag * jnp.sin(ang)
        nr = ar - 1.0
        ni = ai
        den = lr * lr + li * li
        f_re = (nr * lr + ni * li) / den
        f_im = (ni * lr - nr * li) / den
        btr = btr_ref[gi]
        bti = bti_ref[gi]
        bbr = f_re * btr - f_im * bti
        bbi = f_re * bti + f_im * btr
        cr = cr_ref[gi]
        ci = ci_ref[gi]

        pa = [jnp.where(is_re, 1.0, 0.0).astype(F32)]
        pb = [jnp.where(is_re, 0.0, 1.0).astype(F32)]
        for _ in range(CHUNK):
            a_prev, b_prev = pa[-1], pb[-1]
            pa.append(a_prev * ar + b_prev * ai)
            pb.append(b_prev * ar - a_prev * ai)

        gk = [sgn * (cr * pa[k] + ci * pb[k]) for k in range(CHUNK + 1)]
        fs = jnp.concatenate(gk[:CHUNK], axis=0)
        et = jnp.concatenate(gk[1:], axis=0)
        rtt = jnp.concatenate(
            [bbr * pa[CHUNK - 1 - q] + bbi * pb[CHUNK - 1 - q] for q in range(CHUNK)], axis=0)
        bst = jnp.where(is_re, bbr, bbi)
        bst_t = jnp.concatenate([bst] * CHUNK, axis=0)
        kw = lax.dot_general(fs, bst_t, (((1,), (1,)), ((), ())),
                             precision=lax.Precision.HIGHEST, preferred_element_type=F32)
        m = jnp.zeros((cq, cq), F32)
        for q in range(CHUNK):
            if q == 0:
                shifted = kw
            else:
                shifted = jnp.concatenate(
                    [jnp.zeros((q * SSM_GROUP, cq), F32), kw[:cq - q * SSM_GROUP]], axis=0)
            m = jnp.where(lane_blk == q, shifted, m)
        m_ref[gi] = m.astype(BF16)
        rt_ref[gi] = rtt.T.astype(BF16)
        et_ref[gi] = et.astype(BF16)

        wa = jnp.broadcast_to(pa[CHUNK], (8, p2))
        rows = jnp.zeros((8, p2), F32)
        for k in range(N_SCAN_PASSES):
            rows = jnp.where(row8 == k, wa, rows)
            swapped = pltpu.roll(wa, SSM_STATE, 1)
            re2 = jnp.where(is_re8, wa, swapped)
            im2 = jnp.where(is_re8, swapped, wa)
            wb = jnp.where(is_re8, -swapped, swapped)
            wa = wa * re2 + wb * im2
        full = jnp.concatenate([rows, jnp.zeros((p2 - 8, p2), F32)], axis=0)
        w_ref[gi] = full.T


def _ssm_prep(lam_re, lam_im, log_step, b_re, b_im, c_re, c_im, groups=8):
    g, p = lam_re.shape
    p2 = 2 * p
    cq = CHUNK * SSM_GROUP
    dup = lambda v: jnp.concatenate([v, v], axis=-1)
    lre = dup(lam_re)[:, None, :]
    lim = dup(lam_im)[:, None, :]
    ls = jnp.broadcast_to(log_step[:, None, None], (g, 1, p2))
    btr = dup(jnp.swapaxes(b_re, 1, 2))
    bti = dup(jnp.swapaxes(b_im, 1, 2))
    cr = dup(c_re)
    ci = dup(c_im)
    row = pl.BlockSpec((groups, 1, p2), lambda i: (i, 0, 0))
    mat = pl.BlockSpec((groups, SSM_GROUP, p2), lambda i: (i, 0, 0))
    return pl.pallas_call(
        functools.partial(_ssm_prep_kernel, groups=groups),
        out_shape=(jax.ShapeDtypeStruct((g, cq, cq), BF16),
                   jax.ShapeDtypeStruct((g, p2, cq), BF16),
                   jax.ShapeDtypeStruct((g, cq, p2), BF16),
                   jax.ShapeDtypeStruct((g, p2, p2), F32)),
        grid=(g // groups,),
        in_specs=[row, row, row, mat, mat, mat, mat],
        out_specs=(pl.BlockSpec((groups, cq, cq), lambda i: (i, 0, 0)),
                   pl.BlockSpec((groups, p2, cq), lambda i: (i, 0, 0)),
                   pl.BlockSpec((groups, cq, p2), lambda i: (i, 0, 0)),
                   pl.BlockSpec((groups, p2, p2), lambda i: (i, 0, 0))),
        compiler_params=_params(("arbitrary",),
                                _vmem_limit(groups * cq * cq * 4, 0, SMALL_KERNEL_TEMP)),
        name="ssm_prep",
    )(lre, lim, ls, btr, bti, cr, ci)


def _ssm_kernel(*refs, groups, chunks_per_seq, n_cast):
    u_ref, m_ref, rt_ref, et_ref, w_ref, d_ref = refs[:6]
    o_ref = refs[6 + n_cast]
    for src, dst in zip(refs[6:6 + n_cast], refs[7 + n_cast:]):
        dst[...] = src[...].astype(BF16)
    cq = CHUNK * SSM_GROUP
    nc = u_ref.shape[-1]
    pos = lax.broadcasted_iota(jnp.int32, (SSM_STATE, nc), 1) % chunks_per_seq
    for gi in range(groups):
        x3 = u_ref[gi]
        x = x3.reshape(cq, nc)
        y = jnp.dot(m_ref[gi], x, preferred_element_type=F32)
        r = jnp.dot(rt_ref[gi], x, preferred_element_type=F32)
        s_re, s_im = r[:SSM_STATE], r[SSM_STATE:]
        for k in range(N_SCAN_PASSES):
            dist = 1 << k
            wr = w_ref[gi, 0:SSM_STATE, k:k + 1]
            wi = w_ref[gi, SSM_STATE:2 * SSM_STATE, k:k + 1]
            keep = pos >= dist
            sh_re = jnp.where(keep, pltpu.roll(s_re, dist, 1), 0.0)
            sh_im = jnp.where(keep, pltpu.roll(s_im, dist, 1), 0.0)
            s_re, s_im = (s_re + wr * sh_re - wi * sh_im,
                          s_im + wr * sh_im + wi * sh_re)
        first = pos >= 1
        sp = jnp.concatenate([jnp.where(first, pltpu.roll(s_re, 1, 1), 0.0),
                              jnp.where(first, pltpu.roll(s_im, 1, 1), 0.0)], axis=0)
        y = y + jnp.dot(et_ref[gi], sp.astype(BF16), preferred_element_type=F32)
        y3 = y.reshape(CHUNK, SSM_GROUP, nc) + d_ref[gi][None] * x3.astype(F32)
        o_ref[gi] = _gelu_tanh(y3).astype(BF16)


def _ssm(ut, m, rt, et, w, d, chunks_per_seq, cast_srcs, groups=4):
    g, q, j, nc = ut.shape
    cq = q * j
    p2 = rt.shape[1]
    grid = (g // groups,)
    blk = lambda *s: pl.BlockSpec((groups,) + s, lambda i: (i,) + (0,) * len(s))
    casts = [_side_cast_specs(src, src.shape[0] // grid[0], grid) for src in cast_srcs]
    cast_bytes = sum(src.size // grid[0] * 6 for src in cast_srcs)
    return pl.pallas_call(
        functools.partial(_ssm_kernel, groups=groups, chunks_per_seq=chunks_per_seq,
                          n_cast=len(cast_srcs)),
        out_shape=(jax.ShapeDtypeStruct(ut.shape, BF16), *[c[2] for c in casts]),
        grid=grid,
        in_specs=[blk(q, j, nc), blk(cq, cq), blk(p2, cq), blk(cq, p2), blk(p2, p2), blk(j, 1),
                  *[c[0] for c in casts]],
        out_specs=(blk(q, j, nc), *[c[1] for c in casts]),
        compiler_params=_params(("arbitrary",),
                                _vmem_limit(groups * cq * nc * 8 + cast_bytes, 0,
                                            SMALL_KERNEL_TEMP)),
        name="ssm",
    )(ut, m, rt, et, w, d, *cast_srcs)


def _glu_kernel(y_ref, w_ref, b_ref, o_ref, y_sc, *, tn):
    n = pl.program_id(1)
    chunks = y_ref.shape[-1]
    n_slabs = y_sc.shape[0]

    @pl.when(n == 0)
    def _():
        for q in range(CHUNK):
            yq = y_ref[:, q]
            yq = yq.reshape(yq.shape[0] * yq.shape[1], chunks).astype(F32).T
            for s in range(n_slabs):
                y_sc[s, pl.ds(q, chunks, stride=CHUNK), :] = yq[:, s * LANES:(s + 1) * LANES]

    y = jnp.concatenate([y_sc[s] for s in range(n_slabs)], axis=1).astype(BF16)
    z = jnp.dot(y, w_ref[...], preferred_element_type=F32) + b_ref[...]
    per_tile = tn // LANES
    yn = jnp.concatenate([y_sc[n * per_tile + s] for s in range(per_tile)], axis=1)
    o_ref[...] = (yn * _sigmoid(z)).astype(BF16)


def _glu(yt, w, b, seq_len, tn=512):
    g, q, j, nc = yt.shape
    d = g * j
    chunks = seq_len // q
    t = nc * q
    return pl.pallas_call(
        functools.partial(_glu_kernel, tn=tn),
        out_shape=jax.ShapeDtypeStruct((t, d), BF16),
        grid=(nc // chunks, d // tn),
        in_specs=[pl.BlockSpec((g, q, j, chunks), lambda bi, n: (0, 0, 0, bi)),
                  pl.BlockSpec((d, tn), lambda bi, n: (0, n)),
                  pl.BlockSpec((1, tn), lambda bi, n: (0, n))],
        out_specs=pl.BlockSpec((seq_len, tn), lambda bi, n: (bi, n)),
        scratch_shapes=[pltpu.VMEM((d // LANES, seq_len, LANES), F32)],
        compiler_params=_params(("arbitrary", "arbitrary"),
                                _vmem_limit((seq_len * d + d * tn + seq_len * tn) * 2,
                                            seq_len * d * 4, seq_len * d * 2 + seq_len * tn * 12)),
        name="glu",
    )(yt, w, b)


def _pool_kernel(u_ref, w_ref, b_ref, s_ref, o_ref, ext_sc, *, tiles_per_seq, halo):
    i = pl.program_id(0)
    tm = u_ref.shape[0]
    gp = w_ref.shape[1]

    @pl.when(i % tiles_per_seq == 0)
    def _():
        ext_sc[0:halo] = jnp.zeros((halo, ext_sc.shape[1]), BF16)

    @pl.when(i % tiles_per_seq != 0)
    def _():
        ext_sc[0:halo] = ext_sc[tm:tm + halo]

    ext_sc[halo:halo + tm] = u_ref[...]

    t_loc = lax.broadcasted_iota(jnp.int32, (tm, tm + halo), 0)
    s_loc = lax.broadcasted_iota(jnp.int32, (tm, tm + halo), 1)
    lag = t_loc + halo - s_loc
    t_seq = (i % tiles_per_seq) * tm + lax.broadcasted_iota(jnp.int32, (tm, 1), 0)
    for gi, win in enumerate(POOL_WINDOWS):
        band = jnp.where((lag >= 0) & (lag < win), 1.0, 0.0).astype(BF16)
        cols = slice(gi * gp, (gi + 1) * gp)
        wsum = jnp.dot(band, ext_sc[:, cols], preferred_element_type=F32)
        cnt = jnp.minimum(t_seq + 1, win).astype(F32)
        z = wsum / cnt - u_ref[:, cols].astype(F32)
        z = jnp.dot(z.astype(BF16), w_ref[gi], preferred_element_type=F32) + b_ref[gi]
        o_ref[:, cols] = (z * s_ref[:, cols]).astype(BF16)


def _pool(prest, w, b, scale, seq_len, d_pool, tm=512, halo=128):
    t = prest.shape[0]
    ng, gp, _ = w.shape
    return pl.pallas_call(
        functools.partial(_pool_kernel, tiles_per_seq=seq_len // tm, halo=halo),
        out_shape=jax.ShapeDtypeStruct((t, d_pool), BF16),
        grid=(t // tm,),
        in_specs=[pl.BlockSpec((tm, d_pool), lambda i: (i, 0)),
                  pl.BlockSpec((ng, gp, gp), lambda i: (0, 0, 0)),
                  pl.BlockSpec((ng, 1, gp), lambda i: (0, 0, 0)),
                  pl.BlockSpec((1, d_pool), lambda i: (0, 0))],
        out_specs=pl.BlockSpec((tm, d_pool), lambda i: (i, 0)),
        scratch_shapes=[pltpu.VMEM((tm + halo, d_pool), BF16)],
        compiler_params=_params(("arbitrary",),
                                _vmem_limit(tm * d_pool * 4 + ng * gp * gp * 2,
                                            (tm + halo) * d_pool * 2, SMALL_KERNEL_TEMP)),
        name="pool",
    )(prest, w, b, scale)


def _mix_kernel(ys_ref, po_ref, gs_ref, gp_ref, x_ref, wbs_ref, wbp_ref, wo_ref, g_ref, cast_ref,
                o_ref, cast_o_ref, x_sc, *, sub):
    n = pl.program_id(1)
    cast_o_ref[...] = cast_ref[...].astype(BF16)
    tn = x_ref.shape[1]
    x_sc[:, pl.ds(pl.multiple_of(n * tn, tn), tn)] = x_ref[...]

    @pl.when(n == 0)
    def _():
        o_ref[...] = jnp.zeros_like(o_ref)

    subs = [slice(s * sub, (s + 1) * sub) for s in range(wo_ref.shape[0] // sub)]
    branch = [(jnp.dot(ys_ref[...], wbs_ref[:, cols], preferred_element_type=F32),
               jnp.dot(po_ref[...], wbp_ref[:, cols], preferred_element_type=F32))
              for cols in subs]
    merged = [(gs_ref[:, cols].astype(F32) * ms + gp_ref[:, cols].astype(F32) * mp).astype(BF16)
              for cols, (ms, mp) in zip(subs, branch)]
    o_ref[...] += jnp.dot(jnp.concatenate(merged, axis=1), wo_ref[...],
                          preferred_element_type=F32)

    @pl.when(n == pl.num_programs(1) - 1)
    def _():
        def rows(sl):
            o_ref[sl, :] = x_sc[sl, :] + _rms(o_ref[sl, :], g_ref[...])
        _for_row_blocks(o_ref.shape[0], rows)


def _mix(ys, po, prest, x2, wbs, wbp, wo, g, gate_col0, cast_src, cast_rows,
         tm=512, tn=512, sub=256):
    t, d = x2.shape
    ds_ = ys.shape[1]
    dp = po.shape[1]
    c0 = gate_col0 // tn
    c1 = (gate_col0 + d) // tn
    resident = pl.Buffered(1)
    grid = (t // tm, d // tn)
    cast_in, cast_out, cast_shape = _side_cast_specs(cast_src, cast_rows, grid)
    cast_bytes = cast_rows * cast_src.shape[1] * 6
    return pl.pallas_call(
        functools.partial(_mix_kernel, sub=sub),
        out_shape=(jax.ShapeDtypeStruct((t, d), F32), cast_shape),
        grid=grid,
        in_specs=[pl.BlockSpec((tm, ds_), lambda i, n: (i, 0)),
                  pl.BlockSpec((tm, dp), lambda i, n: (i, 0)),
                  pl.BlockSpec((tm, tn), lambda i, n: (i, c0 + n)),
                  pl.BlockSpec((tm, tn), lambda i, n: (i, c1 + n)),
                  pl.BlockSpec((tm, tn), lambda i, n: (i, n)),
                  pl.BlockSpec((ds_, tn), lambda i, n: (0, n)),
                  pl.BlockSpec((dp, tn), lambda i, n: (0, n)),
                  pl.BlockSpec((tn, d), lambda i, n: (n, 0)),
                  pl.BlockSpec((1, d), lambda i, n: (0, 0)),
                  cast_in],
        out_specs=(pl.BlockSpec((tm, d), lambda i, n: (i, 0), pipeline_mode=resident), cast_out),
        scratch_shapes=[pltpu.VMEM((tm, d), F32)],
        compiler_params=_params(
            ("arbitrary", "arbitrary"),
            _vmem_limit(tm * (ds_ + dp) * 2 + tm * tn * 8 + (ds_ + dp) * tn * 2 + tn * d * 2
                        + cast_bytes, tm * d * 8, tm * tn * 16)),
        name="mix",
    )(ys, po, prest, prest, x2, wbs, wbp, wo, g, cast_src)


def _ffn_kernel(*refs, n_sub, n_tiles, tiles_per_seq, keep, up_row_blocks):
    h_ref, g3_ref = refs[:2]
    groups = [refs[2 + k * n_sub:2 + (k + 1) * n_sub] for k in range(7)]
    wa_refs, wb_refs, cwa_refs, cwb_refs, cba_refs, cbb_refs, wd_refs = groups
    g4_ref, o_ref, c_sc, carry_sc = refs[2 + 7 * n_sub:]
    i = pl.program_id(0)
    f = pl.program_id(1)
    tm = h_ref.shape[0]

    @pl.when(f == 0)
    def _():
        def rows(sl):
            c_sc[sl, :] = _rms(h_ref[sl, :], g3_ref[...]).astype(BF16)
            o_ref[sl, :] = jnp.zeros((NORM_ROWS, o_ref.shape[1]), F32)
        _for_row_blocks(tm, rows)

    seq_start = i % tiles_per_seq == 0

    rb = tm // up_row_blocks
    row_in_head = lax.broadcasted_iota(jnp.int32, (keep, wd_refs[0].shape[0]), 0)

    def up_blocks(w_ref):
        w = w_ref[...]
        return [jnp.dot(c_sc[r0:r0 + rb, :], w, preferred_element_type=F32)
                for r0 in range(0, tm, rb)]

    def conv(blocks, half, tile, cw_ref, cb_ref):
        prev = jnp.where(seq_start, 0.0, carry_sc[half, tile])
        carry_sc[half, tile] = blocks[-1][rb - keep:]
        cw = cw_ref[...]
        outs = []
        for up in blocks:
            z = cw[0:1] * up
            zp = cw[0:1] * prev
            for k in range(1, CONV_WIDTH):
                r = pltpu.roll(z, 1, 0)
                head = jnp.where(row_in_head == 0, zp[keep - 1:keep], r[0:keep])
                z = cw[k:k + 1] * up + jnp.concatenate([head, r[keep:]], axis=0)
                if k + 1 < CONV_WIDTH:
                    zp = cw[k:k + 1] * prev + pltpu.roll(zp, 1, 0)
            outs.append(z + cb_ref[...])
            prev = up[rb - keep:]
        return outs

    def step(n_act):
        ups_a = [up_blocks(wa_refs[s]) for s in range(n_act)]
        ups_b = [up_blocks(wb_refs[s]) for s in range(n_act)]
        ups = list(zip(ups_a, ups_b))
        fms = []
        for s in range(n_act):
            tile = f * n_sub + s
            ua = conv(ups[s][0], 0, tile, cwa_refs[s], cba_refs[s])
            ub = conv(ups[s][1], 1, tile, cwb_refs[s], cbb_refs[s])
            fms.append(jnp.concatenate(
                [(_gelu_tanh(a) * b).astype(BF16) for a, b in zip(ua, ub)], axis=0))
        fm = jnp.concatenate(fms, axis=1)
        wd = jnp.concatenate([wd_refs[s][...] for s in range(n_act)], axis=0)
        o_ref[...] += jnp.dot(fm, wd, preferred_element_type=F32)

    last = pl.num_programs(1) - 1
    tail = n_tiles - (n_tiles - 1) // n_sub * n_sub
    if tail == n_sub:
        step(n_sub)
    else:
        pl.when(f < last)(lambda: step(n_sub))
        pl.when(f == last)(lambda: step(tail))

    @pl.when(f == pl.num_programs(1) - 1)
    def _():
        def rows(sl):
            o_ref[sl, :] = h_ref[sl, :] + _rms(o_ref[sl, :], g4_ref[...])
        _for_row_blocks(tm, rows)


def _ffn(h1, g3, w_up, conv_w, conv_b, w_down, g4, seq_len, tm=512, sub=256, n_sub=2, keep=8,
         up_row_blocks=2):
    t, d = h1.shape
    d_ff = w_down.shape[0]
    n_tiles = d_ff // sub
    nf = pl.cdiv(n_tiles, n_sub)
    resident = pl.Buffered(1)

    def tile_of(s):
        return lambda f: jnp.minimum(f * n_sub + s, n_tiles - 1)

    def per_sub(shape, index):
        return [pl.BlockSpec(shape, functools.partial(index, tile_of(s))) for s in range(n_sub)]

    col_a = lambda tile, i, f: (0, tile(f))
    col_b = lambda tile, i, f: (0, n_tiles + tile(f))
    row_d = lambda tile, i, f: (tile(f), 0)
    in_specs = ([pl.BlockSpec((tm, d), lambda i, f: (i, 0), pipeline_mode=resident),
                 pl.BlockSpec((1, d), lambda i, f: (0, 0))]
                + per_sub((d, sub), col_a) + per_sub((d, sub), col_b)
                + per_sub((CONV_WIDTH, sub), col_a) + per_sub((CONV_WIDTH, sub), col_b)
                + per_sub((1, sub), col_a) + per_sub((1, sub), col_b)
                + per_sub((sub, d), row_d)
                + [pl.BlockSpec((1, d), lambda i, f: (0, 0))])
    operands = ([h1, g3] + [w_up] * (2 * n_sub) + [conv_w] * (2 * n_sub)
                + [conv_b] * (2 * n_sub) + [w_down] * n_sub + [g4])
    tf = sub * n_sub
    return pl.pallas_call(
        functools.partial(_ffn_kernel, n_sub=n_sub, n_tiles=n_tiles,
                          tiles_per_seq=seq_len // tm, keep=keep, up_row_blocks=up_row_blocks),
        out_shape=jax.ShapeDtypeStruct((t, d), F32),
        grid=(t // tm, nf),
        in_specs=in_specs,
        out_specs=pl.BlockSpec((tm, d), lambda i, f: (i, 0), pipeline_mode=resident),
        scratch_shapes=[pltpu.VMEM((tm, d), BF16),
                        pltpu.VMEM((2, nf * n_sub, keep, sub), F32)],
        compiler_params=_params(
            ("arbitrary", "arbitrary"),
            _vmem_limit(d * tf * 4 + tf * d * 2,
                        tm * d * 8 + tm * d * 2 + 2 * nf * tf * keep * 4, tm * tf * 32)),
        name="ffn",
    )(*operands)


def kernel(x, norm_pre_mix, w_in, ssm_lambda_re, ssm_lambda_im, ssm_log_step, ssm_b_re, ssm_b_im,
           ssm_c_re, ssm_c_im, ssm_d, ssm_glu_w, ssm_glu_b, pool_w, pool_b, pool_scale,
           w_branch_ssm, w_branch_pool, w_out, norm_post_mix, norm_pre_ffn, w_up, ffn_conv_w,
           ffn_conv_b, w_down, norm_post_ffn):
    bsz, seq_len, d = x.shape
    depth = w_in.shape[0]
    d_ssm = ssm_d.shape[1]
    d_pool = pool_scale.shape[1]
    n_groups = d_ssm // SSM_GROUP
    assert seq_len // CHUNK == 1 << N_SCAN_PASSES
    t = bsz * seq_len
    h = x.reshape(t, d)
    row = lambda v: v.reshape(1, -1)
    for i in range(depth):
        a = _norm(h, row(norm_pre_mix[i]))
        prest, w_up_b = _inproj(a, w_in[i], d_ssm, d_pool, w_up[i], 32)
        ut = _inproj_ssm(a, w_in[i], d_ssm, seq_len)
        m, rt, et, wsc = _ssm_prep(ssm_lambda_re[i], ssm_lambda_im[i], ssm_log_step[i],
                                   ssm_b_re[i], ssm_b_im[i], ssm_c_re[i], ssm_c_im[i])
        yt, w_out_b, w_bs_b, w_bp_b = _ssm(
            ut, m, rt, et, wsc, ssm_d[i].reshape(n_groups, SSM_GROUP, 1), seq_len // CHUNK,
            [w_out[i], w_branch_ssm[i], w_branch_pool[i]])
        ys = _glu(yt, ssm_glu_w[i].astype(BF16), row(ssm_glu_b[i]), seq_len)
        po = _pool(prest, pool_w[i].astype(BF16), pool_b[i][:, None, :], row(pool_scale[i]),
                   seq_len, d_pool)
        h, w_down_b = _mix(ys, po, prest, h, w_bs_b, w_bp_b, w_out_b,
                           row(norm_post_mix[i]), d_pool, w_down[i], 128)
        h = _ffn(h, row(norm_pre_ffn[i]), w_up_b, ffn_conv_w[i], row(ffn_conv_b[i]),
                 w_down_b, row(norm_post_ffn[i]), seq_len)
    return h.reshape(bsz, seq_len, d)
```

```python
import functools
import math

import jax
import jax.numpy as jnp
from jax import lax
from jax.experimental import pallas as pl
from jax.experimental.pallas import tpu as pltpu

F32 = jnp.float32
BF16 = jnp.bfloat16

EPS = 1e-6
MIN_NEG_REAL = -1e-4
SSM_GROUP = 16
SSM_STATE = 64
POOL_WINDOWS = (2, 4, 8, 16)
CONV_WIDTH = 3
CHUNK = 16
N_SCAN_PASSES = 7
LANES = 128
FFN_SUB = 256
FFN_STEP_COLS = 512

V7X_VMEM_BYTES = 64 * 1024 * 1024
VMEM_RESERVE = 6 * 1024 * 1024
VMEM_CEILING = V7X_VMEM_BYTES - VMEM_RESERVE
VMEM_SLACK = 4 * 1024 * 1024
SMALL_KERNEL_TEMP = 8 * 1024 * 1024


def _vmem_limit(pipelined_bytes, scratch_bytes, temp_bytes):
    return min(2 * pipelined_bytes + scratch_bytes + temp_bytes + VMEM_SLACK, VMEM_CEILING)


def _params(semantics, vmem):
    return pltpu.CompilerParams(dimension_semantics=semantics, vmem_limit_bytes=vmem)


def _rms(xf, g):
    ms = jnp.mean(xf * xf, axis=-1, keepdims=True)
    return xf * lax.rsqrt(ms + EPS) * g


NORM_ROWS = 32


def _for_row_blocks(n_rows, body):
    def step(r, carry):
        body(pl.ds(pl.multiple_of(r * NORM_ROWS, NORM_ROWS), NORM_ROWS))
        return carry
    lax.fori_loop(0, n_rows // NORM_ROWS, step, 0, unroll=2)


def _sigmoid(x):
    return 1.0 / (1.0 + jnp.exp(-x))


def _gelu_tanh(x):
    c = math.sqrt(2.0 / math.pi)
    return 0.5 * x * (1.0 + jnp.tanh(c * (x + 0.044715 * (x * x * x))))


def _norm_kernel(x_ref, g_ref, o_ref):
    def rows(sl):
        o_ref[sl, :] = _rms(x_ref[sl, :], g_ref[...]).astype(BF16)
    _for_row_blocks(x_ref.shape[0], rows)


def _norm(x2, g, tm=512):
    t, d = x2.shape
    return pl.pallas_call(
        _norm_kernel,
        out_shape=jax.ShapeDtypeStruct((t, d), BF16),
        grid=(t // tm,),
        in_specs=[pl.BlockSpec((tm, d), lambda i: (i, 0)),
                  pl.BlockSpec((1, d), lambda i: (0, 0))],
        out_specs=pl.BlockSpec((tm, d), lambda i: (i, 0)),
        compiler_params=_params(("arbitrary",), _vmem_limit(tm * d * 6, 0, tm * d * 8)),
        name="norm",
    )(x2, g)


def _side_cast_specs(src, rows, grid, pad_rows=0):
    n_blocks = src.shape[0] // rows
    n_out = n_blocks + pad_rows // rows
    assert src.shape[0] % rows == 0 and pad_rows % rows == 0 and n_out <= math.prod(grid)

    def step(*g):
        s = g[0]
        for k in range(1, len(grid)):
            s = s * grid[k] + g[k]
        return s

    in_spec = pl.BlockSpec((rows, src.shape[1]), lambda *g: (jnp.minimum(step(*g), n_blocks - 1), 0))
    out_spec = pl.BlockSpec((rows, src.shape[1]), lambda *g: (jnp.minimum(step(*g), n_out - 1), 0))
    return in_spec, out_spec, jax.ShapeDtypeStruct((src.shape[0] + pad_rows, src.shape[1]), BF16)


def _inproj_kernel(a_ref, w_ref, cast_ref, o_ref, cast_o_ref, *, n_plain, row_blocks):
    gate = pl.program_id(1) >= n_plain
    w = w_ref[...].astype(BF16)
    rows = a_ref.shape[0] // row_blocks
    for r0 in range(0, a_ref.shape[0], rows):
        p = jnp.dot(a_ref[r0:r0 + rows, :], w, preferred_element_type=F32)
        o_ref[r0:r0 + rows, :] = jnp.where(gate, _sigmoid(p), p).astype(BF16)
    cast_o_ref[...] = cast_ref[...].astype(BF16)


def _inproj(a, w, col0, n_plain_cols, cast_src, cast_rows, tm=1024, tn=512, row_blocks=4):
    t, d = a.shape
    n = w.shape[1] - col0
    c0 = col0 // tn
    grid = (t // tm, n // tn)
    cast_in, cast_out, cast_shape = _side_cast_specs(cast_src, cast_rows, grid)
    cast_bytes = cast_rows * cast_src.shape[1] * 6
    return pl.pallas_call(
        functools.partial(_inproj_kernel, n_plain=n_plain_cols // tn, row_blocks=row_blocks),
        out_shape=(jax.ShapeDtypeStruct((t, n), BF16), cast_shape),
        grid=grid,
        in_specs=[pl.BlockSpec((tm, d), lambda i, j: (i, 0)),
                  pl.BlockSpec((d, tn), lambda i, j: (0, c0 + j)),
                  cast_in],
        out_specs=(pl.BlockSpec((tm, tn), lambda i, j: (i, j)), cast_out),
        compiler_params=_params(("arbitrary", "arbitrary"),
                                _vmem_limit(tm * d * 2 + d * tn * 4 + tm * tn * 2 + cast_bytes, 0,
                                            d * tn * 2 + tm * tn * 8)),
        name="inproj",
    )(a, w, cast_src)


def _inproj_ssm_kernel(a_ref, w_ref, o_ref, p_sc, *, row_blocks):
    chunks = o_ref.shape[-1]
    n_slabs = p_sc.shape[0]
    w = w_ref[...].astype(BF16)
    rows = a_ref.shape[0] // row_blocks
    for r0 in range(0, a_ref.shape[0], rows):
        p = jnp.dot(a_ref[r0:r0 + rows, :], w, preferred_element_type=F32)
        for s in range(n_slabs):
            p_sc[s, r0:r0 + rows] = p[:, s * LANES:(s + 1) * LANES]
    gps = LANES // SSM_GROUP
    for q in range(CHUNK):
        for s in range(n_slabs):
            blk = p_sc[s, pl.ds(q, chunks, stride=CHUNK), :]
            o_ref[s * gps:(s + 1) * gps, q] = (
                blk.T.reshape(gps, SSM_GROUP, chunks).astype(BF16))


def _inproj_ssm(a, w, n, seq_len, tn=256, row_blocks=8):
    t, d = a.shape
    g = n // SSM_GROUP
    chunks = seq_len // CHUNK
    return pl.pallas_call(
        functools.partial(_inproj_ssm_kernel, row_blocks=row_blocks),
        out_shape=jax.ShapeDtypeStruct((g, CHUNK, SSM_GROUP, t // CHUNK), BF16),
        grid=(t // seq_len, n // tn),
        in_specs=[pl.BlockSpec((seq_len, d), lambda bi, j: (bi, 0)),
                  pl.BlockSpec((d, tn), lambda bi, j: (0, j))],
        out_specs=pl.BlockSpec((tn // SSM_GROUP, CHUNK, SSM_GROUP, chunks),
                               lambda bi, j: (j, 0, 0, bi)),
        scratch_shapes=[pltpu.VMEM((tn // LANES, seq_len, LANES), F32)],
        compiler_params=_params(("arbitrary", "arbitrary"),
                                _vmem_limit(seq_len * d * 2 + d * tn * 4 + tn * seq_len * 2,
                                            seq_len * tn * 4, d * tn * 2 + seq_len * tn * 16)),
        name="inproj_ssm",
    )(a, w)


def _ssm_prep_kernel(lre_ref, lim_ref, ls_ref, btr_ref, bti_ref, cr_ref, ci_ref,
                     m_ref, rt_ref, et_ref, w_ref, *, groups):
    p2 = 2 * SSM_STATE
    cq = CHUNK * SSM_GROUP
    lane = lax.broadcasted_iota(jnp.int32, (1, p2), 1)
    is_re = lane < SSM_STATE
    sgn = jnp.where(is_re, 1.0, -1.0).astype(F32)
    lane_blk = lax.broadcasted_iota(jnp.int32, (cq, cq), 1) // SSM_GROUP
    row8 = lax.broadcasted_iota(jnp.int32, (8, p2), 0)
    is_re8 = lax.broadcasted_iota(jnp.int32, (8, p2), 1) < SSM_STATE

    for gi in range(groups):
        lr = jnp.minimum(lre_ref[gi], MIN_NEG_REAL)
        li = lim_ref[gi]
        dt = jnp.exp(ls_ref[gi])
        mag = jnp.exp(lr * dt)
        ang = li * dt
        ar = mag * jnp.cos(ang)
        ai = mag * jnp.sin(ang)
        nr = ar - 1.0
        ni = ai
        den = lr * lr + li * li
        f_re = (nr * lr + ni * li) / den
        f_im = (ni * lr - nr * li) / den
        btr = btr_ref[gi]
        bti = bti_ref[gi]
        bbr = f_re * btr - f_im * bti
        bbi = f_re * bti + f_im * btr
        cr = cr_ref[gi]
        ci = ci_ref[gi]

        pa = [jnp.where(is_re, 1.0, 0.0).astype(F32)]
        pb = [jnp.where(is_re, 0.0, 1.0).astype(F32)]
        for _ in range(CHUNK):
            a_prev, b_prev = pa[-1], pb[-1]
            pa.append(a_prev * ar + b_prev * ai)
            pb.append(b_prev * ar - a_prev * ai)

        gk = [sgn * (cr * pa[k] + ci * pb[k]) for k in range(CHUNK + 1)]
        fs = jnp.concatenate(gk[:CHUNK], axis=0)
        et = jnp.concatenate(gk[1:], axis=0)
        rtt = jnp.concatenate(
            [bbr * pa[CHUNK - 1 - q] + bbi * pb[CHUNK - 1 - q] for q in range(CHUNK)], axis=0)
        bst = jnp.where(is_re, bbr, bbi)
        bst_t = jnp.concatenate([bst] * CHUNK, axis=0)
        kw = lax.dot_general(fs, bst_t, (((1,), (1,)), ((), ())),
                             precision=lax.Precision.HIGHEST, preferred_element_type=F32)
        m = jnp.zeros((cq, cq), F32)
        for q in range(CHUNK):
            if q == 0:
                shifted = kw
            else:
                shifted = jnp.concatenate(
                    [jnp.zeros((q * SSM_GROUP, cq), F32), kw[:cq - q * SSM_GROUP]], axis=0)
            m = jnp.where(lane_blk == q, shifted, m)
        m_ref[gi] = m.astype(BF16)
        rt_ref[gi] = rtt.T.astype(BF16)
        et_ref[gi] = et.astype(BF16)

        wa = jnp.broadcast_to(pa[CHUNK], (8, p2))
        rows = jnp.zeros((8, p2), F32)
        for k in range(N_SCAN_PASSES):
            rows = jnp.where(row8 == k, wa, rows)
            swapped = pltpu.roll(wa, SSM_STATE, 1)
            re2 = jnp.where(is_re8, wa, swapped)
            im2 = jnp.where(is_re8, swapped, wa)
            wb = jnp.where(is_re8, -swapped, swapped)
            wa = wa * re2 + wb * im2
        full = jnp.concatenate([rows, jnp.zeros((p2 - 8, p2), F32)], axis=0)
        w_ref[gi] = full.T


def _ssm_prep(lam_re, lam_im, log_step, b_re, b_im, c_re, c_im, groups=8):
    g, p = lam_re.shape
    p2 = 2 * p
    cq = CHUNK * SSM_GROUP
    dup = lambda v: jnp.concatenate([v, v], axis=-1)
    lre = dup(lam_re)[:, None, :]
    lim = dup(lam_im)[:, None, :]
    ls = jnp.broadcast_to(log_step[:, None, None], (g, 1, p2))
    btr = dup(jnp.swapaxes(b_re, 1, 2))
    bti = dup(jnp.swapaxes(b_im, 1, 2))
    cr = dup(c_re)
    ci = dup(c_im)
    row = pl.BlockSpec((groups, 1, p2), lambda i: (i, 0, 0))
    mat = pl.BlockSpec((groups, SSM_GROUP, p2), lambda i: (i, 0, 0))
    return pl.pallas_call(
        functools.partial(_ssm_prep_kernel, groups=groups),
        out_shape=(jax.ShapeDtypeStruct((g, cq, cq), BF16),
                   jax.ShapeDtypeStruct((g, p2, cq), BF16),
                   jax.ShapeDtypeStruct((g, cq, p2), BF16),
                   jax.ShapeDtypeStruct((g, p2, p2), F32)),
        grid=(g // groups,),
        in_specs=[row, row, row, mat, mat, mat, mat],
        out_specs=(pl.BlockSpec((groups, cq, cq), lambda i: (i, 0, 0)),
                   pl.BlockSpec((groups, p2, cq), lambda i: (i, 0, 0)),
                   pl.BlockSpec((groups, cq, p2), lambda i: (i, 0, 0)),
                   pl.BlockSpec((groups, p2, p2), lambda i: (i, 0, 0))),
        compiler_params=_params(("arbitrary",),
                                _vmem_limit(groups * cq * cq * 4, 0, SMALL_KERNEL_TEMP)),
        name="ssm_prep",
    )(lre, lim, ls, btr, bti, cr, ci)


def _ssm_kernel(*refs, groups, chunks_per_seq, n_cast):
    u_ref, m_ref, rt_ref, et_ref, w_ref, d_ref = refs[:6]
    o_ref = refs[6 + n_cast]
    for src, dst in zip(refs[6:6 + n_cast], refs[7 + n_cast:]):
        dst[...] = src[...].astype(BF16)
    cq = CHUNK * SSM_GROUP
    nc = u_ref.shape[-1]
    pos = lax.broadcasted_iota(jnp.int32, (SSM_STATE, nc), 1) % chunks_per_seq
    for gi in range(groups):
        x3 = u_ref[gi]
        x = x3.reshape(cq, nc)
        y = jnp.dot(m_ref[gi], x, preferred_element_type=F32)
        r = jnp.dot(rt_ref[gi], x, preferred_element_type=F32)
        s_re, s_im = r[:SSM_STATE], r[SSM_STATE:]
        for k in range(N_SCAN_PASSES):
            dist = 1 << k
            wr = w_ref[gi, 0:SSM_STATE, k:k + 1]
            wi = w_ref[gi, SSM_STATE:2 * SSM_STATE, k:k + 1]
            keep = pos >= dist
            sh_re = jnp.where(keep, pltpu.roll(s_re, dist, 1), 0.0)
            sh_im = jnp.where(keep, pltpu.roll(s_im, dist, 1), 0.0)
            s_re, s_im = (s_re + wr * sh_re - wi * sh_im,
                          s_im + wr * sh_im + wi * sh_re)
        first = pos >= 1
        sp = jnp.concatenate([jnp.where(first, pltpu.roll(s_re, 1, 1), 0.0),
                              jnp.where(first, pltpu.roll(s_im, 1, 1), 0.0)], axis=0)
        y = y + jnp.dot(et_ref[gi], sp.astype(BF16), preferred_element_type=F32)
        y3 = y.reshape(CHUNK, SSM_GROUP, nc) + d_ref[gi][None] * x3.astype(F32)
        o_ref[gi] = _gelu_tanh(y3).astype(BF16)


def _ssm(ut, m, rt, et, w, d, chunks_per_seq, cast_srcs, groups=4):
    g, q, j, nc = ut.shape
    cq = q * j
    p2 = rt.shape[1]
    grid = (g // groups,)
    blk = lambda *s: pl.BlockSpec((groups,) + s, lambda i: (i,) + (0,) * len(s))
    casts = [_side_cast_specs(src, src.shape[0] // grid[0], grid) for src in cast_srcs]
    cast_bytes = sum(src.size // grid[0] * 6 for src in cast_srcs)
    return pl.pallas_call(
        functools.partial(_ssm_kernel, groups=groups, chunks_per_seq=chunks_per_seq,
                          n_cast=len(cast_srcs)),
        out_shape=(jax.ShapeDtypeStruct(ut.shape, BF16), *[c[2] for c in casts]),
        grid=grid,
        in_specs=[blk(q, j, nc), blk(cq, cq), blk(p2, cq), blk(cq, p2), blk(p2, p2), blk(j, 1),
                  *[c[0] for c in casts]],
        out_specs=(blk(q, j, nc), *[c[1] for c in casts]),
        compiler_params=_params(("arbitrary",),
                                _vmem_limit(groups * cq * nc * 8 + cast_bytes, 0,
                                            SMALL_KERNEL_TEMP)),
        name="ssm",
    )(ut, m, rt, et, w, d, *cast_srcs)


def _glu_kernel(y_ref, w_ref, b_ref, o_ref, y_sc, *, tn):
    n = pl.program_id(1)
    chunks = y_ref.shape[-1]
    n_slabs = y_sc.shape[0]

    @pl.when(n == 0)
    def _():
        for q in range(CHUNK):
            yq = y_ref[:, q]
            yq = yq.reshape(yq.shape[0] * yq.shape[1], chunks).astype(F32).T
            for s in range(n_slabs):
                y_sc[s, pl.ds(q, chunks, stride=CHUNK), :] = yq[:, s * LANES:(s + 1) * LANES]

    y = jnp.concatenate([y_sc[s] for s in range(n_slabs)], axis=1).astype(BF16)
    z = jnp.dot(y, w_ref[...], preferred_element_type=F32) + b_ref[...]
    per_tile = tn // LANES
    yn = jnp.concatenate([y_sc[n * per_tile + s] for s in range(per_tile)], axis=1)
    o_ref[...] = (yn * _sigmoid(z)).astype(BF16)


def _glu(yt, w, b, seq_len, tn=512):
    g, q, j, nc = yt.shape
    d = g * j
    chunks = seq_len // q
    t = nc * q
    return pl.pallas_call(
        functools.partial(_glu_kernel, tn=tn),
        out_shape=jax.ShapeDtypeStruct((t, d), BF16),
        grid=(nc // chunks, d // tn),
        in_specs=[pl.BlockSpec((g, q, j, chunks), lambda bi, n: (0, 0, 0, bi)),
                  pl.BlockSpec((d, tn), lambda bi, n: (0, n)),
                  pl.BlockSpec((1, tn), lambda bi, n: (0, n))],
        out_specs=pl.BlockSpec((seq_len, tn), lambda bi, n: (bi, n)),
        scratch_shapes=[pltpu.VMEM((d // LANES, seq_len, LANES), F32)],
        compiler_params=_params(("arbitrary", "arbitrary"),
                                _vmem_limit((seq_len * d + d * tn + seq_len * tn) * 2,
                                            seq_len * d * 4, seq_len * d * 2 + seq_len * tn * 12)),
        name="glu",
    )(yt, w, b)


def _pool_kernel(u_ref, w_ref, b_ref, s_ref, o_ref, ext_sc, *, tiles_per_seq, halo):
    i = pl.program_id(0)
    tm = u_ref.shape[0]
    gp = w_ref.shape[1]

    @pl.when(i % tiles_per_seq == 0)
    def _():
        ext_sc[0:halo] = jnp.zeros((halo, ext_sc.shape[1]), BF16)

    @pl.when(i % tiles_per_seq != 0)
    def _():
        ext_sc[0:halo] = ext_sc[tm:tm + halo]

    ext_sc[halo:halo + tm] = u_ref[...]

    t_loc = lax.broadcasted_iota(jnp.int32, (tm, tm + halo), 0)
    s_loc = lax.broadcasted_iota(jnp.int32, (tm, tm + halo), 1)
    lag = t_loc + halo - s_loc
    t_seq = (i % tiles_per_seq) * tm + lax.broadcasted_iota(jnp.int32, (tm, 1), 0)
    for gi, win in enumerate(POOL_WINDOWS):
        band = jnp.where((lag >= 0) & (lag < win), 1.0, 0.0).astype(BF16)
        cols = slice(gi * gp, (gi + 1) * gp)
        wsum = jnp.dot(band, ext_sc[:, cols], preferred_element_type=F32)
        cnt = jnp.minimum(t_seq + 1, win).astype(F32)
        z = wsum / cnt - u_ref[:, cols].astype(F32)
        z = jnp.dot(z.astype(BF16), w_ref[gi], preferred_element_type=F32) + b_ref[gi]
        o_ref[:, cols] = (z * s_ref[:, cols]).astype(BF16)


def _pool(prest, w, b, scale, seq_len, d_pool, tm=512, halo=128):
    t = prest.shape[0]
    ng, gp, _ = w.shape
    return pl.pallas_call(
        functools.partial(_pool_kernel, tiles_per_seq=seq_len // tm, halo=halo),
        out_shape=jax.ShapeDtypeStruct((t, d_pool), BF16),
        grid=(t // tm,),
        in_specs=[pl.BlockSpec((tm, d_pool), lambda i: (i, 0)),
                  pl.BlockSpec((ng, gp, gp), lambda i: (0, 0, 0)),
                  pl.BlockSpec((ng, 1, gp), lambda i: (0, 0, 0)),
                  pl.BlockSpec((1, d_pool), lambda i: (0, 0))],
        out_specs=pl.BlockSpec((tm, d_pool), lambda i: (i, 0)),
        scratch_shapes=[pltpu.VMEM((tm + halo, d_pool), BF16)],
        compiler_params=_params(("arbitrary",),
                                _vmem_limit(tm * d_pool * 4 + ng * gp * gp * 2,
                                            (tm + halo) * d_pool * 2, SMALL_KERNEL_TEMP)),
        name="pool",
    )(prest, w, b, scale)


def _mix_kernel(ys_ref, po_ref, gs_ref, gp_ref, x_ref, wbs_ref, wbp_ref, wo_ref, g_ref, cast_ref,
                o_ref, cast_o_ref, x_sc, *, sub, n_cast_blocks):
    n = pl.program_id(1)
    step = pl.program_id(0) * pl.num_programs(1) + n
    cast_o_ref[...] = jnp.where(step < n_cast_blocks, cast_ref[...], 0.0).astype(BF16)
    tn = x_ref.shape[1]
    x_sc[:, pl.ds(pl.multiple_of(n * tn, tn), tn)] = x_ref[...]

    @pl.when(n == 0)
    def _():
        o_ref[...] = jnp.zeros_like(o_ref)

    subs = [slice(s * sub, (s + 1) * sub) for s in range(wo_ref.shape[0] // sub)]
    branch = [(jnp.dot(ys_ref[...], wbs_ref[:, cols], preferred_element_type=F32),
               jnp.dot(po_ref[...], wbp_ref[:, cols], preferred_element_type=F32))
              for cols in subs]
    merged = [(gs_ref[:, cols].astype(F32) * ms + gp_ref[:, cols].astype(F32) * mp).astype(BF16)
              for cols, (ms, mp) in zip(subs, branch)]
    o_ref[...] += jnp.dot(jnp.concatenate(merged, axis=1), wo_ref[...],
                          preferred_element_type=F32)

    @pl.when(n == pl.num_programs(1) - 1)
    def _():
        def rows(sl):
            o_ref[sl, :] = x_sc[sl, :] + _rms(o_ref[sl, :], g_ref[...])
        _for_row_blocks(o_ref.shape[0], rows)


def _mix(ys, po, prest, x2, wbs, wbp, wo, g, gate_col0, cast_src, cast_rows, cast_pad_rows,
         tm=512, tn=512, sub=256):
    t, d = x2.shape
    ds_ = ys.shape[1]
    dp = po.shape[1]
    c0 = gate_col0 // tn
    c1 = (gate_col0 + d) // tn
    resident = pl.Buffered(1)
    grid = (t // tm, d // tn)
    cast_in, cast_out, cast_shape = _side_cast_specs(cast_src, cast_rows, grid, cast_pad_rows)
    cast_bytes = cast_rows * cast_src.shape[1] * 6
    return pl.pallas_call(
        functools.partial(_mix_kernel, sub=sub, n_cast_blocks=cast_src.shape[0] // cast_rows),
        out_shape=(jax.ShapeDtypeStruct((t, d), F32), cast_shape),
        grid=grid,
        in_specs=[pl.BlockSpec((tm, ds_), lambda i, n: (i, 0)),
                  pl.BlockSpec((tm, dp), lambda i, n: (i, 0)),
                  pl.BlockSpec((tm, tn), lambda i, n: (i, c0 + n)),
                  pl.BlockSpec((tm, tn), lambda i, n: (i, c1 + n)),
                  pl.BlockSpec((tm, tn), lambda i, n: (i, n)),
                  pl.BlockSpec((ds_, tn), lambda i, n: (0, n)),
                  pl.BlockSpec((dp, tn), lambda i, n: (0, n)),
                  pl.BlockSpec((tn, d), lambda i, n: (n, 0)),
                  pl.BlockSpec((1, d), lambda i, n: (0, 0)),
                  cast_in],
        out_specs=(pl.BlockSpec((tm, d), lambda i, n: (i, 0), pipeline_mode=resident), cast_out),
        scratch_shapes=[pltpu.VMEM((tm, d), F32)],
        compiler_params=_params(
            ("arbitrary", "arbitrary"),
            _vmem_limit(tm * (ds_ + dp) * 2 + tm * tn * 8 + (ds_ + dp) * tn * 2 + tn * d * 2
                        + cast_bytes, tm * d * 8, tm * tn * 16)),
        name="mix",
    )(ys, po, prest, prest, x2, wbs, wbp, wo, g, cast_src)


def _ffn_up_kernel(*refs, n_sub, n_tiles, tiles_per_seq, keep, up_row_blocks):
    h_ref, g3_ref = refs[:2]
    groups = [refs[2 + k * n_sub:2 + (k + 1) * n_sub] for k in range(6)]
    wa_refs, wb_refs, cwa_refs, cwb_refs, cba_refs, cbb_refs = groups
    o_ref, c_sc, carry_sc = refs[2 + 6 * n_sub:]
    i = pl.program_id(0)
    f = pl.program_id(1)
    tm = h_ref.shape[0]
    sub = wa_refs[0].shape[1]

    @pl.when(f == 0)
    def _():
        def rows(sl):
            c_sc[sl, :] = _rms(h_ref[sl, :], g3_ref[...]).astype(BF16)
        _for_row_blocks(tm, rows)

    seq_start = i % tiles_per_seq == 0

    rb = tm // up_row_blocks
    row_in_head = lax.broadcasted_iota(jnp.int32, (keep, sub), 0)

    def up_blocks(w_ref):
        w = w_ref[...]
        return [jnp.dot(c_sc[r0:r0 + rb, :], w, preferred_element_type=F32)
                for r0 in range(0, tm, rb)]

    def conv(blocks, half, tile, cw_ref, cb_ref):
        prev = jnp.where(seq_start, 0.0, carry_sc[half, tile])
        carry_sc[half, tile] = blocks[-1][rb - keep:]
        cw = cw_ref[...]
        outs = []
        for up in blocks:
            z = cw[0:1] * up
            zp = cw[0:1] * prev
            for k in range(1, CONV_WIDTH):
                r = pltpu.roll(z, 1, 0)
                head = jnp.where(row_in_head == 0, zp[keep - 1:keep], r[0:keep])
                z = cw[k:k + 1] * up + jnp.concatenate([head, r[keep:]], axis=0)
                if k + 1 < CONV_WIDTH:
                    zp = cw[k:k + 1] * prev + pltpu.roll(zp, 1, 0)
            outs.append(z + cb_ref[...])
            prev = up[rb - keep:]
        return outs

    def step(n_act):
        ups_a = [up_blocks(wa_refs[s]) for s in range(n_act)]
        ups_b = [up_blocks(wb_refs[s]) for s in range(n_act)]
        ups = list(zip(ups_a, ups_b))
        fms = []
        for s in range(n_act):
            tile = f * n_sub + s
            ua = conv(ups[s][0], 0, tile, cwa_refs[s], cba_refs[s])
            ub = conv(ups[s][1], 1, tile, cwb_refs[s], cbb_refs[s])
            fms.append(jnp.concatenate(
                [(_gelu_tanh(a) * b).astype(BF16) for a, b in zip(ua, ub)], axis=0))
        fms += [jnp.zeros((tm, sub), BF16)] * (n_sub - n_act)
        o_ref[...] = jnp.concatenate(fms, axis=1)

    last = pl.num_programs(1) - 1
    tail = n_tiles - (n_tiles - 1) // n_sub * n_sub
    if tail == n_sub:
        step(n_sub)
    else:
        pl.when(f < last)(lambda: step(n_sub))
        pl.when(f == last)(lambda: step(tail))


def _ffn_up(h1, g3, w_up, conv_w, conv_b, d_ff, seq_len, tm=512, sub=256, n_sub=2, keep=8,
            up_row_blocks=2):
    t, d = h1.shape
    n_tiles = d_ff // sub
    nf = pl.cdiv(n_tiles, n_sub)
    tf = sub * n_sub
    resident = pl.Buffered(1)

    def tile_of(s):
        return lambda f: jnp.minimum(f * n_sub + s, n_tiles - 1)

    def per_sub(shape, index):
        return [pl.BlockSpec(shape, functools.partial(index, tile_of(s))) for s in range(n_sub)]

    col_a = lambda tile, i, f: (0, tile(f))
    col_b = lambda tile, i, f: (0, n_tiles + tile(f))
    in_specs = ([pl.BlockSpec((tm, d), lambda i, f: (i, 0), pipeline_mode=resident),
                 pl.BlockSpec((1, d), lambda i, f: (0, 0))]
                + per_sub((d, sub), col_a) + per_sub((d, sub), col_b)
                + per_sub((CONV_WIDTH, sub), col_a) + per_sub((CONV_WIDTH, sub), col_b)
                + per_sub((1, sub), col_a) + per_sub((1, sub), col_b))
    operands = [h1, g3] + [w_up] * (2 * n_sub) + [conv_w] * (2 * n_sub) + [conv_b] * (2 * n_sub)
    return pl.pallas_call(
        functools.partial(_ffn_up_kernel, n_sub=n_sub, n_tiles=n_tiles,
                          tiles_per_seq=seq_len // tm, keep=keep, up_row_blocks=up_row_blocks),
        out_shape=jax.ShapeDtypeStruct((t, nf * tf), BF16),
        grid=(t // tm, nf),
        in_specs=in_specs,
        out_specs=pl.BlockSpec((tm, tf), lambda i, f: (i, f)),
        scratch_shapes=[pltpu.VMEM((tm, d), BF16),
                        pltpu.VMEM((2, nf * n_sub, keep, sub), F32)],
        compiler_params=_params(
            ("arbitrary", "arbitrary"),
            _vmem_limit(d * tf * 4 + tm * tf * 2,
                        tm * d * 4 + tm * d * 2 + 2 * nf * tf * keep * 4, tm * tf * 32)),
        name="ffn_up",
    )(*operands)


def _ffn_down_kernel(fm_ref, w_ref, h_ref, g_ref, o_ref, h_sc):
    n = pl.program_id(1)
    tn = w_ref.shape[1]
    cols = pl.ds(pl.multiple_of(n * tn, tn), tn)
    h_sc[:, cols] = h_ref[...]
    o_ref[:, cols] = jnp.dot(fm_ref[...], w_ref[...], preferred_element_type=F32)

    @pl.when(n == pl.num_programs(1) - 1)
    def _():
        def rows(sl):
            o_ref[sl, :] = h_sc[sl, :] + _rms(o_ref[sl, :], g_ref[...])
        _for_row_blocks(o_ref.shape[0], rows)


def _ffn_down(fm, w_down, h1, g4, tm=512, tn=256):
    t, k = fm.shape
    d = w_down.shape[1]
    assert w_down.shape[0] == k
    resident = pl.Buffered(1)
    return pl.pallas_call(
        _ffn_down_kernel,
        out_shape=jax.ShapeDtypeStruct((t, d), F32),
        grid=(t // tm, d // tn),
        in_specs=[pl.BlockSpec((tm, k), lambda i, n: (i, 0)),
                  pl.BlockSpec((k, tn), lambda i, n: (0, n)),
                  pl.BlockSpec((tm, tn), lambda i, n: (i, n)),
                  pl.BlockSpec((1, d), lambda i, n: (0, 0))],
        out_specs=pl.BlockSpec((tm, d), lambda i, n: (i, 0), pipeline_mode=resident),
        scratch_shapes=[pltpu.VMEM((tm, d), F32)],
        compiler_params=_params(
            ("arbitrary", "arbitrary"),
            _vmem_limit(tm * k * 2 + k * tn * 2 + tm * tn * 4, tm * d * 8, tm * tn * 16)),
        name="ffn_down",
    )(fm, w_down, h1, g4)


def kernel(x, norm_pre_mix, w_in, ssm_lambda_re, ssm_lambda_im, ssm_log_step, ssm_b_re, ssm_b_im,
           ssm_c_re, ssm_c_im, ssm_d, ssm_glu_w, ssm_glu_b, pool_w, pool_b, pool_scale,
           w_branch_ssm, w_branch_pool, w_out, norm_post_mix, norm_pre_ffn, w_up, ffn_conv_w,
           ffn_conv_b, w_down, norm_post_ffn):
    bsz, seq_len, d = x.shape
    depth = w_in.shape[0]
    d_ssm = ssm_d.shape[1]
    d_pool = pool_scale.shape[1]
    n_groups = d_ssm // SSM_GROUP
    assert seq_len // CHUNK == 1 << N_SCAN_PASSES
    t = bsz * seq_len
    h = x.reshape(t, d)
    row = lambda v: v.reshape(1, -1)
    for i in range(depth):
        a = _norm(h, row(norm_pre_mix[i]))
        prest, w_up_b = _inproj(a, w_in[i], d_ssm, d_pool, w_up[i], 32)
        ut = _inproj_ssm(a, w_in[i], d_ssm, seq_len)
        m, rt, et, wsc = _ssm_prep(ssm_lambda_re[i], ssm_lambda_im[i], ssm_log_step[i],
                                   ssm_b_re[i], ssm_b_im[i], ssm_c_re[i], ssm_c_im[i])
        yt, w_out_b, w_bs_b, w_bp_b = _ssm(
            ut, m, rt, et, wsc, ssm_d[i].reshape(n_groups, SSM_GROUP, 1), seq_len // CHUNK,
            [w_out[i], w_branch_ssm[i], w_branch_pool[i]])
        ys = _glu(yt, ssm_glu_w[i].astype(BF16), row(ssm_glu_b[i]), seq_len)
        po = _pool(prest, pool_w[i].astype(BF16), pool_b[i][:, None, :], row(pool_scale[i]),
                   seq_len, d_pool)
        d_ff = w_down.shape[1]
        d_ff_pad = -d_ff % FFN_STEP_COLS
        h, w_down_b = _mix(ys, po, prest, h, w_bs_b, w_bp_b, w_out_b,
                           row(norm_post_mix[i]), d_pool, w_down[i], 128, d_ff_pad)
        fm = _ffn_up(h, row(norm_pre_ffn[i]), w_up_b, ffn_conv_w[i], row(ffn_conv_b[i]),
                     d_ff, seq_len, sub=FFN_SUB, n_sub=FFN_STEP_COLS // FFN_SUB)
        h = _ffn_down(fm, w_down_b, h, row(norm_post_ffn[i]))
    return h.reshape(bsz, seq_len, d)
```

```python
import functools
import math

import jax
import jax.numpy as jnp
from jax import lax
from jax.experimental import pallas as pl
from jax.experimental.pallas import tpu as pltpu

F32 = jnp.float32
BF16 = jnp.bfloat16

EPS = 1e-6
MIN_NEG_REAL = -1e-4
SSM_GROUP = 16
SSM_STATE = 64
POOL_WINDOWS = (2, 4, 8, 16)
CONV_WIDTH = 3
CHUNK = 16
N_SCAN_PASSES = 7
LANES = 128
SUBLANES = 8
TIME_GROUP = 128

V7X_VMEM_BYTES = 64 * 1024 * 1024
VMEM_RESERVE = 6 * 1024 * 1024
VMEM_CEILING = V7X_VMEM_BYTES - VMEM_RESERVE
VMEM_SLACK = 4 * 1024 * 1024
SMALL_KERNEL_TEMP = 8 * 1024 * 1024


def _vmem_limit(pipelined_bytes, scratch_bytes, temp_bytes):
    return min(2 * pipelined_bytes + scratch_bytes + temp_bytes + VMEM_SLACK, VMEM_CEILING)


def _params(semantics, vmem):
    return pltpu.CompilerParams(dimension_semantics=semantics, vmem_limit_bytes=vmem)


def _rms(xf, g):
    ms = jnp.mean(xf * xf, axis=-1, keepdims=True)
    return xf * lax.rsqrt(ms + EPS) * g


NORM_ROWS = 32


def _for_row_blocks(n_rows, body):
    def step(r, carry):
        body(pl.ds(pl.multiple_of(r * NORM_ROWS, NORM_ROWS), NORM_ROWS))
        return carry
    lax.fori_loop(0, n_rows // NORM_ROWS, step, 0, unroll=2)


def _sigmoid(x):
    return 1.0 / (1.0 + jnp.exp(-x))


def _gelu_tanh(x):
    c = math.sqrt(2.0 / math.pi)
    return 0.5 * x * (1.0 + jnp.tanh(c * (x + 0.044715 * (x * x * x))))


def _norm_kernel(x_ref, g_ref, o_ref):
    def rows(sl):
        o_ref[sl, :] = _rms(x_ref[sl, :], g_ref[...]).astype(BF16)
    _for_row_blocks(x_ref.shape[0], rows)


def _norm(x2, g, tm=512):
    t, d = x2.shape
    return pl.pallas_call(
        _norm_kernel,
        out_shape=jax.ShapeDtypeStruct((t, d), BF16),
        grid=(t // tm,),
        in_specs=[pl.BlockSpec((tm, d), lambda i: (i, 0)),
                  pl.BlockSpec((1, d), lambda i: (0, 0))],
        out_specs=pl.BlockSpec((tm, d), lambda i: (i, 0)),
        compiler_params=_params(("arbitrary",), _vmem_limit(tm * d * 6, 0, tm * d * 8)),
        name="norm",
    )(x2, g)


def _side_cast_specs(src, rows, grid):
    n_blocks = src.shape[0] // rows
    assert src.shape[0] % rows == 0 and n_blocks <= math.prod(grid)

    def index(*g):
        step = g[0]
        for k in range(1, len(grid)):
            step = step * grid[k] + g[k]
        return (jnp.minimum(step, n_blocks - 1), 0)

    spec = pl.BlockSpec((rows, src.shape[1]), index)
    return spec, spec, jax.ShapeDtypeStruct(src.shape, BF16)


def _inproj_kernel(a_ref, w_ref, cast_ref, o_ref, cast_o_ref, *, n_plain, row_blocks):
    gate = pl.program_id(1) >= n_plain
    w = w_ref[...].astype(BF16)
    rows = a_ref.shape[0] // row_blocks
    for r0 in range(0, a_ref.shape[0], rows):
        p = jnp.dot(a_ref[r0:r0 + rows, :], w, preferred_element_type=F32)
        o_ref[r0:r0 + rows, :] = jnp.where(gate, _sigmoid(p), p).astype(BF16)
    cast_o_ref[...] = cast_ref[...].astype(BF16)


def _inproj(a, w, col0, n_plain_cols, cast_src, cast_rows, tm=1024, tn=512, row_blocks=4):
    t, d = a.shape
    n = w.shape[1] - col0
    c0 = col0 // tn
    grid = (t // tm, n // tn)
    cast_in, cast_out, cast_shape = _side_cast_specs(cast_src, cast_rows, grid)
    cast_bytes = cast_rows * cast_src.shape[1] * 6
    return pl.pallas_call(
        functools.partial(_inproj_kernel, n_plain=n_plain_cols // tn, row_blocks=row_blocks),
        out_shape=(jax.ShapeDtypeStruct((t, n), BF16), cast_shape),
        grid=grid,
        in_specs=[pl.BlockSpec((tm, d), lambda i, j: (i, 0)),
                  pl.BlockSpec((d, tn), lambda i, j: (0, c0 + j)),
                  cast_in],
        out_specs=(pl.BlockSpec((tm, tn), lambda i, j: (i, j)), cast_out),
        compiler_params=_params(("arbitrary", "arbitrary"),
                                _vmem_limit(tm * d * 2 + d * tn * 4 + tm * tn * 2 + cast_bytes, 0,
                                            d * tn * 2 + tm * tn * 8)),
        name="inproj",
    )(a, w, cast_src)


def _inproj_ssm_kernel(a_ref, w_ref, o_ref, p_sc, *, row_blocks):
    chunks = o_ref.shape[-1]
    n_slabs = p_sc.shape[0]
    w = w_ref[...].astype(BF16)
    rows = a_ref.shape[0] // row_blocks
    for r0 in range(0, a_ref.shape[0], rows):
        p = jnp.dot(a_ref[r0:r0 + rows, :], w, preferred_element_type=F32)
        for s in range(n_slabs):
            p_sc[s, r0:r0 + rows] = p[:, s * LANES:(s + 1) * LANES]
    gps = LANES // SSM_GROUP
    for q in range(CHUNK):
        for s in range(n_slabs):
            blk = p_sc[s, pl.ds(q, chunks, stride=CHUNK), :]
            o_ref[s * gps:(s + 1) * gps, q] = (
                blk.T.reshape(gps, SSM_GROUP, chunks).astype(BF16))


def _inproj_ssm(a, w, n, seq_len, tn=256, row_blocks=8):
    t, d = a.shape
    g = n // SSM_GROUP
    chunks = seq_len // CHUNK
    return pl.pallas_call(
        functools.partial(_inproj_ssm_kernel, row_blocks=row_blocks),
        out_shape=jax.ShapeDtypeStruct((g, CHUNK, SSM_GROUP, t // CHUNK), BF16),
        grid=(t // seq_len, n // tn),
        in_specs=[pl.BlockSpec((seq_len, d), lambda bi, j: (bi, 0)),
                  pl.BlockSpec((d, tn), lambda bi, j: (0, j))],
        out_specs=pl.BlockSpec((tn // SSM_GROUP, CHUNK, SSM_GROUP, chunks),
                               lambda bi, j: (j, 0, 0, bi)),
        scratch_shapes=[pltpu.VMEM((tn // LANES, seq_len, LANES), F32)],
        compiler_params=_params(("arbitrary", "arbitrary"),
                                _vmem_limit(seq_len * d * 2 + d * tn * 4 + tn * seq_len * 2,
                                            seq_len * tn * 4, d * tn * 2 + seq_len * tn * 16)),
        name="inproj_ssm",
    )(a, w)


def _ssm_prep_kernel(lre_ref, lim_ref, ls_ref, btr_ref, bti_ref, cr_ref, ci_ref,
                     m_ref, rt_ref, et_ref, w_ref, *, groups):
    p2 = 2 * SSM_STATE
    cq = CHUNK * SSM_GROUP
    lane = lax.broadcasted_iota(jnp.int32, (1, p2), 1)
    is_re = lane < SSM_STATE
    sgn = jnp.where(is_re, 1.0, -1.0).astype(F32)
    lane_blk = lax.broadcasted_iota(jnp.int32, (cq, cq), 1) // SSM_GROUP
    row8 = lax.broadcasted_iota(jnp.int32, (8, p2), 0)
    is_re8 = lax.broadcasted_iota(jnp.int32, (8, p2), 1) < SSM_STATE

    for gi in range(groups):
        lr = jnp.minimum(lre_ref[gi], MIN_NEG_REAL)
        li = lim_ref[gi]
        dt = jnp.exp(ls_ref[gi])
        mag = jnp.exp(lr * dt)
        ang = li * dt
        ar = mag * jnp.cos(ang)
        ai = mag * jnp.sin(ang)
        nr = ar - 1.0
        ni = ai
        den = lr * lr + li * li
        f_re = (nr * lr + ni * li) / den
        f_im = (ni * lr - nr * li) / den
        btr = btr_ref[gi]
        bti = bti_ref[gi]
        bbr = f_re * btr - f_im * bti
        bbi = f_re * bti + f_im * btr
        cr = cr_ref[gi]
        ci = ci_ref[gi]

        pa = [jnp.where(is_re, 1.0, 0.0).astype(F32)]
        pb = [jnp.where(is_re, 0.0, 1.0).astype(F32)]
        for _ in range(CHUNK):
            a_prev, b_prev = pa[-1], pb[-1]
            pa.append(a_prev * ar + b_prev * ai)
            pb.append(b_prev * ar - a_prev * ai)

        gk = [sgn * (cr * pa[k] + ci * pb[k]) for k in range(CHUNK + 1)]
        fs = jnp.concatenate(gk[:CHUNK], axis=0)
        et = jnp.concatenate(gk[1:], axis=0)
        rtt = jnp.concatenate(
            [bbr * pa[CHUNK - 1 - q] + bbi * pb[CHUNK - 1 - q] for q in range(CHUNK)], axis=0)
        bst = jnp.where(is_re, bbr, bbi)
        bst_t = jnp.concatenate([bst] * CHUNK, axis=0)
        kw = lax.dot_general(fs, bst_t, (((1,), (1,)), ((), ())),
                             precision=lax.Precision.HIGHEST, preferred_element_type=F32)
        m = jnp.zeros((cq, cq), F32)
        for q in range(CHUNK):
            if q == 0:
                shifted = kw
            else:
                shifted = jnp.concatenate(
                    [jnp.zeros((q * SSM_GROUP, cq), F32), kw[:cq - q * SSM_GROUP]], axis=0)
            m = jnp.where(lane_blk == q, shifted, m)
        m_ref[gi] = m.astype(BF16)
        rt_ref[gi] = rtt.T.astype(BF16)
        et_ref[gi] = et.astype(BF16)

        wa = jnp.broadcast_to(pa[CHUNK], (8, p2))
        rows = jnp.zeros((8, p2), F32)
        for k in range(N_SCAN_PASSES):
            rows = jnp.where(row8 == k, wa, rows)
            swapped = pltpu.roll(wa, SSM_STATE, 1)
            re2 = jnp.where(is_re8, wa, swapped)
            im2 = jnp.where(is_re8, swapped, wa)
            wb = jnp.where(is_re8, -swapped, swapped)
            wa = wa * re2 + wb * im2
        full = jnp.concatenate([rows, jnp.zeros((p2 - 8, p2), F32)], axis=0)
        w_ref[gi] = full.T


def _ssm_prep(lam_re, lam_im, log_step, b_re, b_im, c_re, c_im, groups=8):
    g, p = lam_re.shape
    p2 = 2 * p
    cq = CHUNK * SSM_GROUP
    dup = lambda v: jnp.concatenate([v, v], axis=-1)
    lre = dup(lam_re)[:, None, :]
    lim = dup(lam_im)[:, None, :]
    ls = jnp.broadcast_to(log_step[:, None, None], (g, 1, p2))
    btr = dup(jnp.swapaxes(b_re, 1, 2))
    bti = dup(jnp.swapaxes(b_im, 1, 2))
    cr = dup(c_re)
    ci = dup(c_im)
    row = pl.BlockSpec((groups, 1, p2), lambda i: (i, 0, 0))
    mat = pl.BlockSpec((groups, SSM_GROUP, p2), lambda i: (i, 0, 0))
    return pl.pallas_call(
        functools.partial(_ssm_prep_kernel, groups=groups),
        out_shape=(jax.ShapeDtypeStruct((g, cq, cq), BF16),
                   jax.ShapeDtypeStruct((g, p2, cq), BF16),
                   jax.ShapeDtypeStruct((g, cq, p2), BF16),
                   jax.ShapeDtypeStruct((g, p2, p2), F32)),
        grid=(g // groups,),
        in_specs=[row, row, row, mat, mat, mat, mat],
        out_specs=(pl.BlockSpec((groups, cq, cq), lambda i: (i, 0, 0)),
                   pl.BlockSpec((groups, p2, cq), lambda i: (i, 0, 0)),
                   pl.BlockSpec((groups, cq, p2), lambda i: (i, 0, 0)),
                   pl.BlockSpec((groups, p2, p2), lambda i: (i, 0, 0))),
        compiler_params=_params(("arbitrary",),
                                _vmem_limit(groups * cq * cq * 4, 0, SMALL_KERNEL_TEMP)),
        name="ssm_prep",
    )(lre, lim, ls, btr, bti, cr, ci)


def _ssm_kernel(*refs, groups, chunks_per_seq, n_cast):
    u_ref, m_ref, rt_ref, et_ref, w_ref, d_ref = refs[:6]
    o_ref = refs[6 + n_cast]
    for src, dst in zip(refs[6:6 + n_cast], refs[7 + n_cast:]):
        dst[...] = src[...].astype(BF16)
    cq = CHUNK * SSM_GROUP
    nc = u_ref.shape[-1]
    pos = lax.broadcasted_iota(jnp.int32, (SSM_STATE, nc), 1) % chunks_per_seq
    for gi in range(groups):
        x3 = u_ref[gi]
        x = x3.reshape(cq, nc)
        y = jnp.dot(m_ref[gi], x, preferred_element_type=F32)
        r = jnp.dot(rt_ref[gi], x, preferred_element_type=F32)
        s_re, s_im = r[:SSM_STATE], r[SSM_STATE:]
        for k in range(N_SCAN_PASSES):
            dist = 1 << k
            wr = w_ref[gi, 0:SSM_STATE, k:k + 1]
            wi = w_ref[gi, SSM_STATE:2 * SSM_STATE, k:k + 1]
            keep = pos >= dist
            sh_re = jnp.where(keep, pltpu.roll(s_re, dist, 1), 0.0)
            sh_im = jnp.where(keep, pltpu.roll(s_im, dist, 1), 0.0)
            s_re, s_im = (s_re + wr * sh_re - wi * sh_im,
                          s_im + wr * sh_im + wi * sh_re)
        first = pos >= 1
        sp = jnp.concatenate([jnp.where(first, pltpu.roll(s_re, 1, 1), 0.0),
                              jnp.where(first, pltpu.roll(s_im, 1, 1), 0.0)], axis=0)
        y = y + jnp.dot(et_ref[gi], sp.astype(BF16), preferred_element_type=F32)
        y3 = y.reshape(CHUNK, SSM_GROUP, nc) + d_ref[gi][None] * x3.astype(F32)
        o_ref[gi] = _gelu_tanh(y3).astype(BF16)


def _ssm(ut, m, rt, et, w, d, chunks_per_seq, cast_srcs, groups=4):
    g, q, j, nc = ut.shape
    cq = q * j
    p2 = rt.shape[1]
    grid = (g // groups,)
    blk = lambda *s: pl.BlockSpec((groups,) + s, lambda i: (i,) + (0,) * len(s))
    casts = [_side_cast_specs(src, src.shape[0] // grid[0], grid) for src in cast_srcs]
    cast_bytes = sum(src.size // grid[0] * 6 for src in cast_srcs)
    return pl.pallas_call(
        functools.partial(_ssm_kernel, groups=groups, chunks_per_seq=chunks_per_seq,
                          n_cast=len(cast_srcs)),
        out_shape=(jax.ShapeDtypeStruct(ut.shape, BF16), *[c[2] for c in casts]),
        grid=grid,
        in_specs=[blk(q, j, nc), blk(cq, cq), blk(p2, cq), blk(cq, p2), blk(p2, p2), blk(j, 1),
                  *[c[0] for c in casts]],
        out_specs=(blk(q, j, nc), *[c[1] for c in casts]),
        compiler_params=_params(("arbitrary",),
                                _vmem_limit(groups * cq * nc * 8 + cast_bytes, 0,
                                            SMALL_KERNEL_TEMP)),
        name="ssm",
    )(ut, m, rt, et, w, d, *cast_srcs)


def _glu_kernel(y_ref, w_ref, b_ref, o_ref, y_sc, *, tn):
    n = pl.program_id(1)
    chunks = y_ref.shape[-1]
    n_slabs = y_sc.shape[0]

    @pl.when(n == 0)
    def _():
        for q in range(CHUNK):
            yq = y_ref[:, q]
            yq = yq.reshape(yq.shape[0] * yq.shape[1], chunks).astype(F32).T
            for s in range(n_slabs):
                y_sc[s, pl.ds(q, chunks, stride=CHUNK), :] = yq[:, s * LANES:(s + 1) * LANES]

    y = jnp.concatenate([y_sc[s] for s in range(n_slabs)], axis=1).astype(BF16)
    z = jnp.dot(y, w_ref[...], preferred_element_type=F32) + b_ref[...]
    per_tile = tn // LANES
    yn = jnp.concatenate([y_sc[n * per_tile + s] for s in range(per_tile)], axis=1)
    o_ref[...] = (yn * _sigmoid(z)).astype(BF16)


def _glu(yt, w, b, seq_len, tn=512):
    g, q, j, nc = yt.shape
    d = g * j
    chunks = seq_len // q
    t = nc * q
    return pl.pallas_call(
        functools.partial(_glu_kernel, tn=tn),
        out_shape=jax.ShapeDtypeStruct((t, d), BF16),
        grid=(nc // chunks, d // tn),
        in_specs=[pl.BlockSpec((g, q, j, chunks), lambda bi, n: (0, 0, 0, bi)),
                  pl.BlockSpec((d, tn), lambda bi, n: (0, n)),
                  pl.BlockSpec((1, tn), lambda bi, n: (0, n))],
        out_specs=pl.BlockSpec((seq_len, tn), lambda bi, n: (bi, n)),
        scratch_shapes=[pltpu.VMEM((d // LANES, seq_len, LANES), F32)],
        compiler_params=_params(("arbitrary", "arbitrary"),
                                _vmem_limit((seq_len * d + d * tn + seq_len * tn) * 2,
                                            seq_len * d * 4, seq_len * d * 2 + seq_len * tn * 12)),
        name="glu",
    )(yt, w, b)


def _pool_kernel(u_ref, w_ref, b_ref, s_ref, o_ref, ext_sc, *, tiles_per_seq, halo):
    i = pl.program_id(0)
    tm = u_ref.shape[0]
    gp = w_ref.shape[1]

    @pl.when(i % tiles_per_seq == 0)
    def _():
        ext_sc[0:halo] = jnp.zeros((halo, ext_sc.shape[1]), BF16)

    @pl.when(i % tiles_per_seq != 0)
    def _():
        ext_sc[0:halo] = ext_sc[tm:tm + halo]

    ext_sc[halo:halo + tm] = u_ref[...]

    t_loc = lax.broadcasted_iota(jnp.int32, (tm, tm + halo), 0)
    s_loc = lax.broadcasted_iota(jnp.int32, (tm, tm + halo), 1)
    lag = t_loc + halo - s_loc
    t_seq = (i % tiles_per_seq) * tm + lax.broadcasted_iota(jnp.int32, (tm, 1), 0)
    for gi, win in enumerate(POOL_WINDOWS):
        band = jnp.where((lag >= 0) & (lag < win), 1.0, 0.0).astype(BF16)
        cols = slice(gi * gp, (gi + 1) * gp)
        wsum = jnp.dot(band, ext_sc[:, cols], preferred_element_type=F32)
        cnt = jnp.minimum(t_seq + 1, win).astype(F32)
        z = wsum / cnt - u_ref[:, cols].astype(F32)
        z = jnp.dot(z.astype(BF16), w_ref[gi], preferred_element_type=F32) + b_ref[gi]
        o_ref[:, cols] = (z * s_ref[:, cols]).astype(BF16)


def _pool(prest, w, b, scale, seq_len, d_pool, tm=512, halo=128):
    t = prest.shape[0]
    ng, gp, _ = w.shape
    return pl.pallas_call(
        functools.partial(_pool_kernel, tiles_per_seq=seq_len // tm, halo=halo),
        out_shape=jax.ShapeDtypeStruct((t, d_pool), BF16),
        grid=(t // tm,),
        in_specs=[pl.BlockSpec((tm, d_pool), lambda i: (i, 0)),
                  pl.BlockSpec((ng, gp, gp), lambda i: (0, 0, 0)),
                  pl.BlockSpec((ng, 1, gp), lambda i: (0, 0, 0)),
                  pl.BlockSpec((1, d_pool), lambda i: (0, 0))],
        out_specs=pl.BlockSpec((tm, d_pool), lambda i: (i, 0)),
        scratch_shapes=[pltpu.VMEM((tm + halo, d_pool), BF16)],
        compiler_params=_params(("arbitrary",),
                                _vmem_limit(tm * d_pool * 4 + ng * gp * gp * 2,
                                            (tm + halo) * d_pool * 2, SMALL_KERNEL_TEMP)),
        name="pool",
    )(prest, w, b, scale)


def _mix_kernel(ys_ref, po_ref, gs_ref, gp_ref, x_ref, wbs_ref, wbp_ref, wo_ref, g_ref, cast_ref,
                o_ref, cast_o_ref, x_sc, *, sub):
    n = pl.program_id(1)
    cast_o_ref[...] = cast_ref[...].astype(BF16)
    tn = x_ref.shape[1]
    x_sc[:, pl.ds(pl.multiple_of(n * tn, tn), tn)] = x_ref[...]

    @pl.when(n == 0)
    def _():
        o_ref[...] = jnp.zeros_like(o_ref)

    subs = [slice(s * sub, (s + 1) * sub) for s in range(wo_ref.shape[0] // sub)]
    branch = [(jnp.dot(ys_ref[...], wbs_ref[:, cols], preferred_element_type=F32),
               jnp.dot(po_ref[...], wbp_ref[:, cols], preferred_element_type=F32))
              for cols in subs]
    merged = [(gs_ref[:, cols].astype(F32) * ms + gp_ref[:, cols].astype(F32) * mp).astype(BF16)
              for cols, (ms, mp) in zip(subs, branch)]
    o_ref[...] += jnp.dot(jnp.concatenate(merged, axis=1), wo_ref[...],
                          preferred_element_type=F32)

    @pl.when(n == pl.num_programs(1) - 1)
    def _():
        def rows(sl):
            o_ref[sl, :] = x_sc[sl, :] + _rms(o_ref[sl, :], g_ref[...])
        _for_row_blocks(o_ref.shape[0], rows)


def _mix(ys, po, prest, x2, wbs, wbp, wo, g, gate_col0, cast_src, cast_rows,
         tm=512, tn=512, sub=256):
    t, d = x2.shape
    ds_ = ys.shape[1]
    dp = po.shape[1]
    c0 = gate_col0 // tn
    c1 = (gate_col0 + d) // tn
    resident = pl.Buffered(1)
    grid = (t // tm, d // tn)
    cast_in, cast_out, cast_shape = _side_cast_specs(cast_src, cast_rows, grid)
    cast_bytes = cast_rows * cast_src.shape[1] * 6
    return pl.pallas_call(
        functools.partial(_mix_kernel, sub=sub),
        out_shape=(jax.ShapeDtypeStruct((t, d), F32), cast_shape),
        grid=grid,
        in_specs=[pl.BlockSpec((tm, ds_), lambda i, n: (i, 0)),
                  pl.BlockSpec((tm, dp), lambda i, n: (i, 0)),
                  pl.BlockSpec((tm, tn), lambda i, n: (i, c0 + n)),
                  pl.BlockSpec((tm, tn), lambda i, n: (i, c1 + n)),
                  pl.BlockSpec((tm, tn), lambda i, n: (i, n)),
                  pl.BlockSpec((ds_, tn), lambda i, n: (0, n)),
                  pl.BlockSpec((dp, tn), lambda i, n: (0, n)),
                  pl.BlockSpec((tn, d), lambda i, n: (n, 0)),
                  pl.BlockSpec((1, d), lambda i, n: (0, 0)),
                  cast_in],
        out_specs=(pl.BlockSpec((tm, d), lambda i, n: (i, 0), pipeline_mode=resident), cast_out),
        scratch_shapes=[pltpu.VMEM((tm, d), F32)],
        compiler_params=_params(
            ("arbitrary", "arbitrary"),
            _vmem_limit(tm * (ds_ + dp) * 2 + tm * tn * 8 + (ds_ + dp) * tn * 2 + tn * d * 2
                        + cast_bytes, tm * d * 8, tm * tn * 16)),
        name="mix",
    )(ys, po, prest, prest, x2, wbs, wbp, wo, g, cast_src)


def _ffn_kernel(*refs, n_sub, n_tiles, tiles_per_seq, keep, up_row_blocks):
    h_ref, g3_ref = refs[:2]
    groups = [refs[2 + k * n_sub:2 + (k + 1) * n_sub] for k in range(7)]
    wa_refs, wb_refs, cwa_refs, cwb_refs, cba_refs, cbb_refs, wd_refs = groups
    g4_ref, o_ref, c_sc, carry_sc, slab_sc = refs[2 + 7 * n_sub:]
    i = pl.program_id(0)
    f = pl.program_id(1)
    tm = h_ref.shape[0]
    d = h_ref.shape[1]
    sub = wd_refs[0].shape[0]
    n_slabs = d // LANES
    vregs = TIME_GROUP // SUBLANES

    def to_interleaved(x):
        for s in range(n_slabs):
            slab_sc[s] = x[:, s * LANES:(s + 1) * LANES]
        return jnp.concatenate(
            [jnp.concatenate([slab_sc[s, pl.ds(v, SUBLANES, stride=vregs), :]
                              for v in range(vregs)], axis=0)
             for s in range(n_slabs)], axis=1)

    def to_token_order(y):
        for s in range(n_slabs):
            for v in range(vregs):
                slab_sc[s, pl.ds(v, SUBLANES, stride=vregs), :] = (
                    y[v * SUBLANES:(v + 1) * SUBLANES, s * LANES:(s + 1) * LANES])
        return jnp.concatenate([slab_sc[s] for s in range(n_slabs)], axis=1)

    def for_time_groups(body):
        def step(g, carry):
            body(pl.ds(pl.multiple_of(g * TIME_GROUP, TIME_GROUP), TIME_GROUP))
            return carry
        lax.fori_loop(0, tm // TIME_GROUP, step, 0)

    @pl.when(f == 0)
    def _():
        def group(sl):
            c_sc[sl, :] = to_interleaved(_rms(h_ref[sl, :], g3_ref[...])).astype(BF16)
            o_ref[sl, :] = jnp.zeros((TIME_GROUP, d), F32)
        for_time_groups(group)

    seq_start = i % tiles_per_seq == 0

    rb = tm // up_row_blocks
    row_in_head = lax.broadcasted_iota(jnp.int32, (SUBLANES, sub), 0)

    def up_blocks(w_ref):
        w = w_ref[...]
        return [jnp.dot(c_sc[r0:r0 + rb, :], w, preferred_element_type=F32)
                for r0 in range(0, tm, rb)]

    def shift_time(z, before):
        pieces = []
        for g0 in range(0, tm, TIME_GROUP):
            last_vreg = pltpu.roll(z[g0 + TIME_GROUP - SUBLANES:g0 + TIME_GROUP], 1, 0)
            first = before if g0 == 0 else z[g0 - 1:g0]
            pieces += [jnp.where(row_in_head == 0, first, last_vreg),
                       z[g0:g0 + TIME_GROUP - SUBLANES]]
        return jnp.concatenate(pieces, axis=0)

    def conv(blocks, half, tile, cw_ref, cb_ref):
        up = jnp.concatenate(blocks, axis=0)
        tails = [jnp.where(seq_start, 0.0, carry_sc[half, tile, j]) for j in range(CONV_WIDTH - 1)]
        for j in range(CONV_WIDTH - 1):
            carry_sc[half, tile, j] = up[tm - (j + 1) * SUBLANES:tm - j * SUBLANES]
        cw = cw_ref[...]
        z = cw[0:1] * up
        for k in range(1, CONV_WIDTH):
            before = sum(cw[k - 1 - j:k - j] * tails[j][SUBLANES - 1:SUBLANES] for j in range(k))
            z = cw[k:k + 1] * up + shift_time(z, before)
        z = z + cb_ref[...]
        return [z[r0:r0 + rb] for r0 in range(0, tm, rb)]

    def step(n_act):
        ups_a = [up_blocks(wa_refs[s]) for s in range(n_act)]
        ups_b = [up_blocks(wb_refs[s]) for s in range(n_act)]
        ups = list(zip(ups_a, ups_b))
        fms = []
        for s in range(n_act):
            tile = f * n_sub + s
            ua = conv(ups[s][0], 0, tile, cwa_refs[s], cba_refs[s])
            ub = conv(ups[s][1], 1, tile, cwb_refs[s], cbb_refs[s])
            fms.append(jnp.concatenate(
                [(_gelu_tanh(a) * b).astype(BF16) for a, b in zip(ua, ub)], axis=0))
        fm = jnp.concatenate(fms, axis=1)
        wd = jnp.concatenate([wd_refs[s][...] for s in range(n_act)], axis=0)
        o_ref[...] += jnp.dot(fm, wd, preferred_element_type=F32)

    last = pl.num_programs(1) - 1
    tail = n_tiles - (n_tiles - 1) // n_sub * n_sub
    if tail == n_sub:
        step(n_sub)
    else:
        pl.when(f < last)(lambda: step(n_sub))
        pl.when(f == last)(lambda: step(tail))

    @pl.when(f == pl.num_programs(1) - 1)
    def _():
        def group(sl):
            o_ref[sl, :] = h_ref[sl, :] + to_token_order(_rms(o_ref[sl, :], g4_ref[...]))
        for_time_groups(group)


def _ffn(h1, g3, w_up, conv_w, conv_b, w_down, g4, seq_len, tm=512, sub=256, n_sub=2, keep=8,
         up_row_blocks=2):
    t, d = h1.shape
    d_ff = w_down.shape[0]
    n_tiles = d_ff // sub
    nf = pl.cdiv(n_tiles, n_sub)
    resident = pl.Buffered(1)

    def tile_of(s):
        return lambda f: jnp.minimum(f * n_sub + s, n_tiles - 1)

    def per_sub(shape, index):
        return [pl.BlockSpec(shape, functools.partial(index, tile_of(s))) for s in range(n_sub)]

    col_a = lambda tile, i, f: (0, tile(f))
    col_b = lambda tile, i, f: (0, n_tiles + tile(f))
    row_d = lambda tile, i, f: (tile(f), 0)
    in_specs = ([pl.BlockSpec((tm, d), lambda i, f: (i, 0), pipeline_mode=resident),
                 pl.BlockSpec((1, d), lambda i, f: (0, 0))]
                + per_sub((d, sub), col_a) + per_sub((d, sub), col_b)
                + per_sub((CONV_WIDTH, sub), col_a) + per_sub((CONV_WIDTH, sub), col_b)
                + per_sub((1, sub), col_a) + per_sub((1, sub), col_b)
                + per_sub((sub, d), row_d)
                + [pl.BlockSpec((1, d), lambda i, f: (0, 0))])
    operands = ([h1, g3] + [w_up] * (2 * n_sub) + [conv_w] * (2 * n_sub)
                + [conv_b] * (2 * n_sub) + [w_down] * n_sub + [g4])
    tf = sub * n_sub
    return pl.pallas_call(
        functools.partial(_ffn_kernel, n_sub=n_sub, n_tiles=n_tiles,
                          tiles_per_seq=seq_len // tm, keep=keep, up_row_blocks=up_row_blocks),
        out_shape=jax.ShapeDtypeStruct((t, d), F32),
        grid=(t // tm, nf),
        in_specs=in_specs,
        out_specs=pl.BlockSpec((tm, d), lambda i, f: (i, 0), pipeline_mode=resident),
        scratch_shapes=[pltpu.VMEM((tm, d), BF16),
                        pltpu.VMEM((2, nf * n_sub, CONV_WIDTH - 1, SUBLANES, sub), F32),
                        pltpu.VMEM((d // LANES, TIME_GROUP, LANES), F32)],
        compiler_params=_params(
            ("arbitrary", "arbitrary"),
            _vmem_limit(d * tf * 4 + tf * d * 2,
                        tm * d * 8 + tm * d * 2 + 2 * nf * tf * keep * 4 + TIME_GROUP * d * 4,
                        tm * tf * 32)),
        name="ffn",
    )(*operands)


def kernel(x, norm_pre_mix, w_in, ssm_lambda_re, ssm_lambda_im, ssm_log_step, ssm_b_re, ssm_b_im,
           ssm_c_re, ssm_c_im, ssm_d, ssm_glu_w, ssm_glu_b, pool_w, pool_b, pool_scale,
           w_branch_ssm, w_branch_pool, w_out, norm_post_mix, norm_pre_ffn, w_up, ffn_conv_w,
           ffn_conv_b, w_down, norm_post_ffn):
    bsz, seq_len, d = x.shape
    depth = w_in.shape[0]
    d_ssm = ssm_d.shape[1]
    d_pool = pool_scale.shape[1]
    n_groups = d_ssm // SSM_GROUP
    assert seq_len // CHUNK == 1 << N_SCAN_PASSES
    t = bsz * seq_len
    h = x.reshape(t, d)
    row = lambda v: v.reshape(1, -1)
    for i in range(depth):
        a = _norm(h, row(norm_pre_mix[i]))
        prest, w_up_b = _inproj(a, w_in[i], d_ssm, d_pool, w_up[i], 32)
        ut = _inproj_ssm(a, w_in[i], d_ssm, seq_len)
        m, rt, et, wsc = _ssm_prep(ssm_lambda_re[i], ssm_lambda_im[i], ssm_log_step[i],
                                   ssm_b_re[i], ssm_b_im[i], ssm_c_re[i], ssm_c_im[i])
        yt, w_out_b, w_bs_b, w_bp_b = _ssm(
            ut, m, rt, et, wsc, ssm_d[i].reshape(n_groups, SSM_GROUP, 1), seq_len // CHUNK,
            [w_out[i], w_branch_ssm[i], w_branch_pool[i]])
        ys = _glu(yt, ssm_glu_w[i].astype(BF16), row(ssm_glu_b[i]), seq_len)
        po = _pool(prest, pool_w[i].astype(BF16), pool_b[i][:, None, :], row(pool_scale[i]),
                   seq_len, d_pool)
        h, w_down_b = _mix(ys, po, prest, h, w_bs_b, w_bp_b, w_out_b,
                           row(norm_post_mix[i]), d_pool, w_down[i], 128)
        h = _ffn(h, row(norm_pre_ffn[i]), w_up_b, ffn_conv_w[i], row(ffn_conv_b[i]),
                 w_down_b, row(norm_post_ffn[i]), seq_len)
    return h.reshape(bsz, seq_len, d)
```

```python
import functools
import math

import jax
import jax.numpy as jnp
from jax import lax
from jax.experimental import pallas as pl
from jax.experimental.pallas import tpu as pltpu

F32 = jnp.float32
BF16 = jnp.bfloat16

EPS = 1e-6
MIN_NEG_REAL = -1e-4
SSM_GROUP = 16
SSM_STATE = 64
POOL_WINDOWS = (2, 4, 8, 16)
CONV_WIDTH = 3
CHUNK = 16
N_SCAN_PASSES = 7
LANES = 128
SUBLANES = 8
TIME_GROUP = 128

V7X_VMEM_BYTES = 64 * 1024 * 1024
VMEM_RESERVE = 6 * 1024 * 1024
VMEM_CEILING = V7X_VMEM_BYTES - VMEM_RESERVE
VMEM_SLACK = 4 * 1024 * 1024
SMALL_KERNEL_TEMP = 8 * 1024 * 1024


def _vmem_limit(pipelined_bytes, scratch_bytes, temp_bytes):
    return min(2 * pipelined_bytes + scratch_bytes + temp_bytes + VMEM_SLACK, VMEM_CEILING)


def _params(semantics, vmem):
    return pltpu.CompilerParams(dimension_semantics=semantics, vmem_limit_bytes=vmem)


def _rms(xf, g):
    ms = jnp.mean(xf * xf, axis=-1, keepdims=True)
    return xf * lax.rsqrt(ms + EPS) * g


NORM_ROWS = 32


def _for_row_blocks(n_rows, body):
    def step(r, carry):
        body(pl.ds(pl.multiple_of(r * NORM_ROWS, NORM_ROWS), NORM_ROWS))
        return carry
    lax.fori_loop(0, n_rows // NORM_ROWS, step, 0, unroll=2)


def _sigmoid(x):
    return 1.0 / (1.0 + jnp.exp(-x))


def _gelu_tanh(x):
    c = math.sqrt(2.0 / math.pi)
    return 0.5 * x * (1.0 + jnp.tanh(c * (x + 0.044715 * (x * x * x))))


def _norm_kernel(x_ref, g_ref, o_ref):
    def rows(sl):
        o_ref[sl, :] = _rms(x_ref[sl, :], g_ref[...]).astype(BF16)
    _for_row_blocks(x_ref.shape[0], rows)


def _norm(x2, g, tm=512):
    t, d = x2.shape
    return pl.pallas_call(
        _norm_kernel,
        out_shape=jax.ShapeDtypeStruct((t, d), BF16),
        grid=(t // tm,),
        in_specs=[pl.BlockSpec((tm, d), lambda i: (i, 0)),
                  pl.BlockSpec((1, d), lambda i: (0, 0))],
        out_specs=pl.BlockSpec((tm, d), lambda i: (i, 0)),
        compiler_params=_params(("arbitrary",), _vmem_limit(tm * d * 6, 0, tm * d * 8)),
        name="norm",
    )(x2, g)


def _side_cast_specs(src, rows, grid):
    n_blocks = src.shape[0] // rows
    assert src.shape[0] % rows == 0 and n_blocks <= math.prod(grid)

    def index(*g):
        step = g[0]
        for k in range(1, len(grid)):
            step = step * grid[k] + g[k]
        return (jnp.minimum(step, n_blocks - 1), 0)

    spec = pl.BlockSpec((rows, src.shape[1]), index)
    return spec, spec, jax.ShapeDtypeStruct(src.shape, BF16)


def _inproj_kernel(a_ref, w_ref, cast_ref, o_ref, cast_o_ref, *, n_plain, row_blocks):
    gate = pl.program_id(1) >= n_plain
    w = w_ref[...].astype(BF16)
    rows = a_ref.shape[0] // row_blocks
    for r0 in range(0, a_ref.shape[0], rows):
        p = jnp.dot(a_ref[r0:r0 + rows, :], w, preferred_element_type=F32)
        o_ref[r0:r0 + rows, :] = jnp.where(gate, _sigmoid(p), p).astype(BF16)
    cast_o_ref[...] = cast_ref[...].astype(BF16)


def _inproj(a, w, col0, n_plain_cols, cast_src, cast_rows, tm=1024, tn=512, row_blocks=4):
    t, d = a.shape
    n = w.shape[1] - col0
    c0 = col0 // tn
    grid = (t // tm, n // tn)
    cast_in, cast_out, cast_shape = _side_cast_specs(cast_src, cast_rows, grid)
    cast_bytes = cast_rows * cast_src.shape[1] * 6
    return pl.pallas_call(
        functools.partial(_inproj_kernel, n_plain=n_plain_cols // tn, row_blocks=row_blocks),
        out_shape=(jax.ShapeDtypeStruct((t, n), BF16), cast_shape),
        grid=grid,
        in_specs=[pl.BlockSpec((tm, d), lambda i, j: (i, 0)),
                  pl.BlockSpec((d, tn), lambda i, j: (0, c0 + j)),
                  cast_in],
        out_specs=(pl.BlockSpec((tm, tn), lambda i, j: (i, j)), cast_out),
        compiler_params=_params(("arbitrary", "arbitrary"),
                                _vmem_limit(tm * d * 2 + d * tn * 4 + tm * tn * 2 + cast_bytes, 0,
                                            d * tn * 2 + tm * tn * 8)),
        name="inproj",
    )(a, w, cast_src)


def _inproj_ssm_kernel(a_ref, w_ref, o_ref, p_sc, *, row_blocks):
    chunks = o_ref.shape[-1]
    n_slabs = p_sc.shape[0]
    w = w_ref[...].astype(BF16)
    rows = a_ref.shape[0] // row_blocks
    for r0 in range(0, a_ref.shape[0], rows):
        p = jnp.dot(a_ref[r0:r0 + rows, :], w, preferred_element_type=F32)
        for s in range(n_slabs):
            p_sc[s, r0:r0 + rows] = p[:, s * LANES:(s + 1) * LANES]
    gps = LANES // SSM_GROUP
    for q in range(CHUNK):
        for s in range(n_slabs):
            blk = p_sc[s, pl.ds(q, chunks, stride=CHUNK), :]
            o_ref[s * gps:(s + 1) * gps, q] = (
                blk.T.reshape(gps, SSM_GROUP, chunks).astype(BF16))


def _inproj_ssm(a, w, n, seq_len, tn=256, row_blocks=8):
    t, d = a.shape
    g = n // SSM_GROUP
    chunks = seq_len // CHUNK
    return pl.pallas_call(
        functools.partial(_inproj_ssm_kernel, row_blocks=row_blocks),
        out_shape=jax.ShapeDtypeStruct((g, CHUNK, SSM_GROUP, t // CHUNK), BF16),
        grid=(t // seq_len, n // tn),
        in_specs=[pl.BlockSpec((seq_len, d), lambda bi, j: (bi, 0)),
                  pl.BlockSpec((d, tn), lambda bi, j: (0, j))],
        out_specs=pl.BlockSpec((tn // SSM_GROUP, CHUNK, SSM_GROUP, chunks),
                               lambda bi, j: (j, 0, 0, bi)),
        scratch_shapes=[pltpu.VMEM((tn // LANES, seq_len, LANES), F32)],
        compiler_params=_params(("arbitrary", "arbitrary"),
                                _vmem_limit(seq_len * d * 2 + d * tn * 4 + tn * seq_len * 2,
                                            seq_len * tn * 4, d * tn * 2 + seq_len * tn * 16)),
        name="inproj_ssm",
    )(a, w)


def _ssm_prep_kernel(lre_ref, lim_ref, ls_ref, btr_ref, bti_ref, cr_ref, ci_ref,
                     m_ref, rt_ref, et_ref, w_ref, *, groups):
    p2 = 2 * SSM_STATE
    cq = CHUNK * SSM_GROUP
    lane = lax.broadcasted_iota(jnp.int32, (1, p2), 1)
    is_re = lane < SSM_STATE
    sgn = jnp.where(is_re, 1.0, -1.0).astype(F32)
    lane_blk = lax.broadcasted_iota(jnp.int32, (cq, cq), 1) // SSM_GROUP
    row8 = lax.broadcasted_iota(jnp.int32, (8, p2), 0)
    is_re8 = lax.broadcasted_iota(jnp.int32, (8, p2), 1) < SSM_STATE

    for gi in range(groups):
        lr = jnp.minimum(lre_ref[gi], MIN_NEG_REAL)
        li = lim_ref[gi]
        dt = jnp.exp(ls_ref[gi])
        mag = jnp.exp(lr * dt)
        ang = li * dt
        ar = mag * jnp.cos(ang)
        ai = mag * jnp.sin(ang)
        nr = ar - 1.0
        ni = ai
        den = lr * lr + li * li
        f_re = (nr * lr + ni * li) / den
        f_im = (ni * lr - nr * li) / den
        btr = btr_ref[gi]
        bti = bti_ref[gi]
        bbr = f_re * btr - f_im * bti
        bbi = f_re * bti + f_im * btr
        cr = cr_ref[gi]
        ci = ci_ref[gi]

        pa = [jnp.where(is_re, 1.0, 0.0).astype(F32)]
        pb = [jnp.where(is_re, 0.0, 1.0).astype(F32)]
        for _ in range(CHUNK):
            a_prev, b_prev = pa[-1], pb[-1]
            pa.append(a_prev * ar + b_prev * ai)
            pb.append(b_prev * ar - a_prev * ai)

        gk = [sgn * (cr * pa[k] + ci * pb[k]) for k in range(CHUNK + 1)]
        fs = jnp.concatenate(gk[:CHUNK], axis=0)
        et = jnp.concatenate(gk[1:], axis=0)
        rtt = jnp.concatenate(
            [bbr * pa[CHUNK - 1 - q] + bbi * pb[CHUNK - 1 - q] for q in range(CHUNK)], axis=0)
        bst = jnp.where(is_re, bbr, bbi)
        bst_t = jnp.concatenate([bst] * CHUNK, axis=0)
        kw = lax.dot_general(fs, bst_t, (((1,), (1,)), ((), ())),
                             precision=lax.Precision.HIGHEST, preferred_element_type=F32)
        m = jnp.zeros((cq, cq), F32)
        for q in range(CHUNK):
            if q == 0:
                shifted = kw
            else:
                shifted = jnp.concatenate(
                    [jnp.zeros((q * SSM_GROUP, cq), F32), kw[:cq - q * SSM_GROUP]], axis=0)
            m = jnp.where(lane_blk == q, shifted, m)
        m_ref[gi] = m.astype(BF16)
        rt_ref[gi] = rtt.T.astype(BF16)
        et_ref[gi] = et.astype(BF16)

        wa = jnp.broadcast_to(pa[CHUNK], (8, p2))
        rows = jnp.zeros((8, p2), F32)
        for k in range(N_SCAN_PASSES):
            rows = jnp.where(row8 == k, wa, rows)
            swapped = pltpu.roll(wa, SSM_STATE, 1)
            re2 = jnp.where(is_re8, wa, swapped)
            im2 = jnp.where(is_re8, swapped, wa)
            wb = jnp.where(is_re8, -swapped, swapped)
            wa = wa * re2 + wb * im2
        full = jnp.concatenate([rows, jnp.zeros((p2 - 8, p2), F32)], axis=0)
        w_ref[gi] = full.T


def _ssm_prep(lam_re, lam_im, log_step, b_re, b_im, c_re, c_im, groups=8):
    g, p = lam_re.shape
    p2 = 2 * p
    cq = CHUNK * SSM_GROUP
    dup = lambda v: jnp.concatenate([v, v], axis=-1)
    lre = dup(lam_re)[:, None, :]
    lim = dup(lam_im)[:, None, :]
    ls = jnp.broadcast_to(log_step[:, None, None], (g, 1, p2))
    btr = dup(jnp.swapaxes(b_re, 1, 2))
    bti = dup(jnp.swapaxes(b_im, 1, 2))
    cr = dup(c_re)
    ci = dup(c_im)
    row = pl.BlockSpec((groups, 1, p2), lambda i: (i, 0, 0))
    mat = pl.BlockSpec((groups, SSM_GROUP, p2), lambda i: (i, 0, 0))
    return pl.pallas_call(
        functools.partial(_ssm_prep_kernel, groups=groups),
        out_shape=(jax.ShapeDtypeStruct((g, cq, cq), BF16),
                   jax.ShapeDtypeStruct((g, p2, cq), BF16),
                   jax.ShapeDtypeStruct((g, cq, p2), BF16),
                   jax.ShapeDtypeStruct((g, p2, p2), F32)),
        grid=(g // groups,),
        in_specs=[row, row, row, mat, mat, mat, mat],
        out_specs=(pl.BlockSpec((groups, cq, cq), lambda i: (i, 0, 0)),
                   pl.BlockSpec((groups, p2, cq), lambda i: (i, 0, 0)),
                   pl.BlockSpec((groups, cq, p2), lambda i: (i, 0, 0)),
                   pl.BlockSpec((groups, p2, p2), lambda i: (i, 0, 0))),
        compiler_params=_params(("arbitrary",),
                                _vmem_limit(groups * cq * cq * 4, 0, SMALL_KERNEL_TEMP)),
        name="ssm_prep",
    )(lre, lim, ls, btr, bti, cr, ci)


def _ssm_kernel(*refs, groups, chunks_per_seq, n_cast):
    u_ref, m_ref, rt_ref, et_ref, w_ref, d_ref = refs[:6]
    o_ref = refs[6 + n_cast]
    for src, dst in zip(refs[6:6 + n_cast], refs[7 + n_cast:]):
        dst[...] = src[...].astype(BF16)
    cq = CHUNK * SSM_GROUP
    nc = u_ref.shape[-1]
    pos = lax.broadcasted_iota(jnp.int32, (SSM_STATE, nc), 1) % chunks_per_seq
    for gi in range(groups):
        x3 = u_ref[gi]
        x = x3.reshape(cq, nc)
        y = jnp.dot(m_ref[gi], x, preferred_element_type=F32)
        r = jnp.dot(rt_ref[gi], x, preferred_element_type=F32)
        s_re, s_im = r[:SSM_STATE], r[SSM_STATE:]
        for k in range(N_SCAN_PASSES):
            dist = 1 << k
            wr = w_ref[gi, 0:SSM_STATE, k:k + 1]
            wi = w_ref[gi, SSM_STATE:2 * SSM_STATE, k:k + 1]
            keep = pos >= dist
            sh_re = jnp.where(keep, pltpu.roll(s_re, dist, 1), 0.0)
            sh_im = jnp.where(keep, pltpu.roll(s_im, dist, 1), 0.0)
            s_re, s_im = (s_re + wr * sh_re - wi * sh_im,
                          s_im + wr * sh_im + wi * sh_re)
        first = pos >= 1
        sp = jnp.concatenate([jnp.where(first, pltpu.roll(s_re, 1, 1), 0.0),
                              jnp.where(first, pltpu.roll(s_im, 1, 1), 0.0)], axis=0)
        y = y + jnp.dot(et_ref[gi], sp.astype(BF16), preferred_element_type=F32)
        y3 = y.reshape(CHUNK, SSM_GROUP, nc) + d_ref[gi][None] * x3.astype(F32)
        o_ref[gi] = _gelu_tanh(y3).astype(BF16)


def _ssm(ut, m, rt, et, w, d, chunks_per_seq, cast_srcs, groups=4):
    g, q, j, nc = ut.shape
    cq = q * j
    p2 = rt.shape[1]
    grid = (g // groups,)
    blk = lambda *s: pl.BlockSpec((groups,) + s, lambda i: (i,) + (0,) * len(s))
    casts = [_side_cast_specs(src, src.shape[0] // grid[0], grid) for src in cast_srcs]
    cast_bytes = sum(src.size // grid[0] * 6 for src in cast_srcs)
    return pl.pallas_call(
        functools.partial(_ssm_kernel, groups=groups, chunks_per_seq=chunks_per_seq,
                          n_cast=len(cast_srcs)),
        out_shape=(jax.ShapeDtypeStruct(ut.shape, BF16), *[c[2] for c in casts]),
        grid=grid,
        in_specs=[blk(q, j, nc), blk(cq, cq), blk(p2, cq), blk(cq, p2), blk(p2, p2), blk(j, 1),
                  *[c[0] for c in casts]],
        out_specs=(blk(q, j, nc), *[c[1] for c in casts]),
        compiler_params=_params(("arbitrary",),
                                _vmem_limit(groups * cq * nc * 8 + cast_bytes, 0,
                                            SMALL_KERNEL_TEMP)),
        name="ssm",
    )(ut, m, rt, et, w, d, *cast_srcs)


def _glu_kernel(y_ref, w_ref, b_ref, o_ref, y_sc, *, tn):
    n = pl.program_id(1)
    chunks = y_ref.shape[-1]
    n_slabs = y_sc.shape[0]

    @pl.when(n == 0)
    def _():
        for q in range(CHUNK):
            yq = y_ref[:, q]
            yq = yq.reshape(yq.shape[0] * yq.shape[1], chunks).astype(F32).T
            for s in range(n_slabs):
                y_sc[s, pl.ds(q, chunks, stride=CHUNK), :] = yq[:, s * LANES:(s + 1) * LANES]

    y = jnp.concatenate([y_sc[s] for s in range(n_slabs)], axis=1).astype(BF16)
    z = jnp.dot(y, w_ref[...], preferred_element_type=F32) + b_ref[...]
    per_tile = tn // LANES
    yn = jnp.concatenate([y_sc[n * per_tile + s] for s in range(per_tile)], axis=1)
    o_ref[...] = (yn * _sigmoid(z)).astype(BF16)


def _glu(yt, w, b, seq_len, tn=512):
    g, q, j, nc = yt.shape
    d = g * j
    chunks = seq_len // q
    t = nc * q
    return pl.pallas_call(
        functools.partial(_glu_kernel, tn=tn),
        out_shape=jax.ShapeDtypeStruct((t, d), BF16),
        grid=(nc // chunks, d // tn),
        in_specs=[pl.BlockSpec((g, q, j, chunks), lambda bi, n: (0, 0, 0, bi)),
                  pl.BlockSpec((d, tn), lambda bi, n: (0, n)),
                  pl.BlockSpec((1, tn), lambda bi, n: (0, n))],
        out_specs=pl.BlockSpec((seq_len, tn), lambda bi, n: (bi, n)),
        scratch_shapes=[pltpu.VMEM((d // LANES, seq_len, LANES), F32)],
        compiler_params=_params(("arbitrary", "arbitrary"),
                                _vmem_limit((seq_len * d + d * tn + seq_len * tn) * 2,
                                            seq_len * d * 4, seq_len * d * 2 + seq_len * tn * 12)),
        name="glu",
    )(yt, w, b)


def _pool_kernel(u_ref, w_ref, b_ref, s_ref, o_ref, ext_sc, *, tiles_per_seq, halo):
    i = pl.program_id(0)
    tm = u_ref.shape[0]
    gp = w_ref.shape[1]

    @pl.when(i % tiles_per_seq == 0)
    def _():
        ext_sc[0:halo] = jnp.zeros((halo, ext_sc.shape[1]), BF16)

    @pl.when(i % tiles_per_seq != 0)
    def _():
        ext_sc[0:halo] = ext_sc[tm:tm + halo]

    ext_sc[halo:halo + tm] = u_ref[...]

    t_loc = lax.broadcasted_iota(jnp.int32, (tm, tm + halo), 0)
    s_loc = lax.broadcasted_iota(jnp.int32, (tm, tm + halo), 1)
    lag = t_loc + halo - s_loc
    t_seq = (i % tiles_per_seq) * tm + lax.broadcasted_iota(jnp.int32, (tm, 1), 0)
    for gi, win in enumerate(POOL_WINDOWS):
        band = jnp.where((lag >= 0) & (lag < win), 1.0, 0.0).astype(BF16)
        cols = slice(gi * gp, (gi + 1) * gp)
        wsum = jnp.dot(band, ext_sc[:, cols], preferred_element_type=F32)
        cnt = jnp.minimum(t_seq + 1, win).astype(F32)
        z = wsum / cnt - u_ref[:, cols].astype(F32)
        z = jnp.dot(z.astype(BF16), w_ref[gi], preferred_element_type=F32) + b_ref[gi]
        o_ref[:, cols] = (z * s_ref[:, cols]).astype(BF16)


def _pool(prest, w, b, scale, seq_len, d_pool, tm=512, halo=128):
    t = prest.shape[0]
    ng, gp, _ = w.shape
    return pl.pallas_call(
        functools.partial(_pool_kernel, tiles_per_seq=seq_len // tm, halo=halo),
        out_shape=jax.ShapeDtypeStruct((t, d_pool), BF16),
        grid=(t // tm,),
        in_specs=[pl.BlockSpec((tm, d_pool), lambda i: (i, 0)),
                  pl.BlockSpec((ng, gp, gp), lambda i: (0, 0, 0)),
                  pl.BlockSpec((ng, 1, gp), lambda i: (0, 0, 0)),
                  pl.BlockSpec((1, d_pool), lambda i: (0, 0))],
        out_specs=pl.BlockSpec((tm, d_pool), lambda i: (i, 0)),
        scratch_shapes=[pltpu.VMEM((tm + halo, d_pool), BF16)],
        compiler_params=_params(("arbitrary",),
                                _vmem_limit(tm * d_pool * 4 + ng * gp * gp * 2,
                                            (tm + halo) * d_pool * 2, SMALL_KERNEL_TEMP)),
        name="pool",
    )(prest, w, b, scale)


def _mix_kernel(ys_ref, po_ref, gs_ref, gp_ref, x_ref, wbs_ref, wbp_ref, wo_ref, g_ref, cast_ref,
                o_ref, cast_o_ref, x_sc, *, sub):
    n = pl.program_id(1)
    cast_o_ref[...] = cast_ref[...].astype(BF16)
    tn = x_ref.shape[1]
    x_sc[:, pl.ds(pl.multiple_of(n * tn, tn), tn)] = x_ref[...]

    @pl.when(n == 0)
    def _():
        o_ref[...] = jnp.zeros_like(o_ref)

    subs = [slice(s * sub, (s + 1) * sub) for s in range(wo_ref.shape[0] // sub)]
    branch = [(jnp.dot(ys_ref[...], wbs_ref[:, cols], preferred_element_type=F32),
               jnp.dot(po_ref[...], wbp_ref[:, cols], preferred_element_type=F32))
              for cols in subs]
    merged = [(gs_ref[:, cols].astype(F32) * ms + gp_ref[:, cols].astype(F32) * mp).astype(BF16)
              for cols, (ms, mp) in zip(subs, branch)]
    o_ref[...] += jnp.dot(jnp.concatenate(merged, axis=1), wo_ref[...],
                          preferred_element_type=F32)

    @pl.when(n == pl.num_programs(1) - 1)
    def _():
        def rows(sl):
            o_ref[sl, :] = x_sc[sl, :] + _rms(o_ref[sl, :], g_ref[...])
        _for_row_blocks(o_ref.shape[0], rows)


def _mix(ys, po, prest, x2, wbs, wbp, wo, g, gate_col0, cast_src, cast_rows,
         tm=512, tn=512, sub=256):
    t, d = x2.shape
    ds_ = ys.shape[1]
    dp = po.shape[1]
    c0 = gate_col0 // tn
    c1 = (gate_col0 + d) // tn
    resident = pl.Buffered(1)
    grid = (t // tm, d // tn)
    cast_in, cast_out, cast_shape = _side_cast_specs(cast_src, cast_rows, grid)
    cast_bytes = cast_rows * cast_src.shape[1] * 6
    return pl.pallas_call(
        functools.partial(_mix_kernel, sub=sub),
        out_shape=(jax.ShapeDtypeStruct((t, d), F32), cast_shape),
        grid=grid,
        in_specs=[pl.BlockSpec((tm, ds_), lambda i, n: (i, 0)),
                  pl.BlockSpec((tm, dp), lambda i, n: (i, 0)),
                  pl.BlockSpec((tm, tn), lambda i, n: (i, c0 + n)),
                  pl.BlockSpec((tm, tn), lambda i, n: (i, c1 + n)),
                  pl.BlockSpec((tm, tn), lambda i, n: (i, n)),
                  pl.BlockSpec((ds_, tn), lambda i, n: (0, n)),
                  pl.BlockSpec((dp, tn), lambda i, n: (0, n)),
                  pl.BlockSpec((tn, d), lambda i, n: (n, 0)),
                  pl.BlockSpec((1, d), lambda i, n: (0, 0)),
                  cast_in],
        out_specs=(pl.BlockSpec((tm, d), lambda i, n: (i, 0), pipeline_mode=resident), cast_out),
        scratch_shapes=[pltpu.VMEM((tm, d), F32)],
        compiler_params=_params(
            ("arbitrary", "arbitrary"),
            _vmem_limit(tm * (ds_ + dp) * 2 + tm * tn * 8 + (ds_ + dp) * tn * 2 + tn * d * 2
                        + cast_bytes, tm * d * 8, tm * tn * 16)),
        name="mix",
    )(ys, po, prest, prest, x2, wbs, wbp, wo, g, cast_src)


def _ffn_kernel(*refs, n_sub, n_tiles, tiles_per_seq, keep, up_row_blocks):
    h_ref, g3_ref = refs[:2]
    groups = [refs[2 + k * n_sub:2 + (k + 1) * n_sub] for k in range(7)]
    wa_refs, wb_refs, cwa_refs, cwb_refs, cba_refs, cbb_refs, wd_refs = groups
    g4_ref, o_ref, c_sc, carry_sc, slab_sc = refs[2 + 7 * n_sub:]
    i = pl.program_id(0)
    f = pl.program_id(1)
    tm = h_ref.shape[0]
    d = h_ref.shape[1]
    sub = wd_refs[0].shape[0]
    n_slabs = d // LANES
    vregs = TIME_GROUP // SUBLANES

    def to_interleaved(x):
        r = lax.broadcasted_iota(jnp.int32, (TIME_GROUP, TIME_GROUP), 0)
        t = lax.broadcasted_iota(jnp.int32, (TIME_GROUP, TIME_GROUP), 1)
        pick = jnp.where(t == (r % SUBLANES) * vregs + r // SUBLANES, 1.0, 0.0).astype(BF16)
        return jnp.dot(pick, x, preferred_element_type=F32).astype(BF16)

    def to_token_order(y):
        for s in range(n_slabs):
            for v in range(vregs):
                slab_sc[s, pl.ds(v, SUBLANES, stride=vregs), :] = (
                    y[v * SUBLANES:(v + 1) * SUBLANES, s * LANES:(s + 1) * LANES])
        return jnp.concatenate([slab_sc[s] for s in range(n_slabs)], axis=1)

    def for_time_groups(body):
        def step(g, carry):
            body(pl.ds(pl.multiple_of(g * TIME_GROUP, TIME_GROUP), TIME_GROUP))
            return carry
        lax.fori_loop(0, tm // TIME_GROUP, step, 0)

    @pl.when(f == 0)
    def _():
        def group(sl):
            c_sc[sl, :] = to_interleaved(_rms(h_ref[sl, :], g3_ref[...]).astype(BF16))
            o_ref[sl, :] = jnp.zeros((TIME_GROUP, d), F32)
        for_time_groups(group)

    seq_start = i % tiles_per_seq == 0

    rb = tm // up_row_blocks
    row_in_head = lax.broadcasted_iota(jnp.int32, (SUBLANES, sub), 0)

    def up_blocks(w_ref):
        w = w_ref[...]
        return [jnp.dot(c_sc[r0:r0 + rb, :], w, preferred_element_type=F32)
                for r0 in range(0, tm, rb)]

    def shift_time(z, before):
        pieces = []
        for g0 in range(0, tm, TIME_GROUP):
            last_vreg = pltpu.roll(z[g0 + TIME_GROUP - SUBLANES:g0 + TIME_GROUP], 1, 0)
            first = before if g0 == 0 else z[g0 - 1:g0]
            pieces += [jnp.where(row_in_head == 0, first, last_vreg),
                       z[g0:g0 + TIME_GROUP - SUBLANES]]
        return jnp.concatenate(pieces, axis=0)

    def conv(blocks, half, tile, cw_ref, cb_ref):
        up = jnp.concatenate(blocks, axis=0)
        tails = [jnp.where(seq_start, 0.0, carry_sc[half, tile, j]) for j in range(CONV_WIDTH - 1)]
        for j in range(CONV_WIDTH - 1):
            carry_sc[half, tile, j] = up[tm - (j + 1) * SUBLANES:tm - j * SUBLANES]
        cw = cw_ref[...]
        z = cw[0:1] * up
        for k in range(1, CONV_WIDTH):
            before = sum(cw[k - 1 - j:k - j] * tails[j][SUBLANES - 1:SUBLANES] for j in range(k))
            z = cw[k:k + 1] * up + shift_time(z, before)
        z = z + cb_ref[...]
        return [z[r0:r0 + rb] for r0 in range(0, tm, rb)]

    def step(n_act):
        ups_a = [up_blocks(wa_refs[s]) for s in range(n_act)]
        ups_b = [up_blocks(wb_refs[s]) for s in range(n_act)]
        ups = list(zip(ups_a, ups_b))
        fms = []
        for s in range(n_act):
            tile = f * n_sub + s
            ua = conv(ups[s][0], 0, tile, cwa_refs[s], cba_refs[s])
            ub = conv(ups[s][1], 1, tile, cwb_refs[s], cbb_refs[s])
            fms.append(jnp.concatenate(
                [(_gelu_tanh(a) * b).astype(BF16) for a, b in zip(ua, ub)], axis=0))
        fm = jnp.concatenate(fms, axis=1)
        wd = jnp.concatenate([wd_refs[s][...] for s in range(n_act)], axis=0)
        o_ref[...] += jnp.dot(fm, wd, preferred_element_type=F32)

    last = pl.num_programs(1) - 1
    tail = n_tiles - (n_tiles - 1) // n_sub * n_sub
    if tail == n_sub:
        step(n_sub)
    else:
        pl.when(f < last)(lambda: step(n_sub))
        pl.when(f == last)(lambda: step(tail))

    @pl.when(f == pl.num_programs(1) - 1)
    def _():
        def group(sl):
            o_ref[sl, :] = h_ref[sl, :] + to_token_order(_rms(o_ref[sl, :], g4_ref[...]))
        for_time_groups(group)


def _ffn(h1, g3, w_up, conv_w, conv_b, w_down, g4, seq_len, tm=512, sub=256, n_sub=2, keep=8,
         up_row_blocks=2):
    t, d = h1.shape
    d_ff = w_down.shape[0]
    n_tiles = d_ff // sub
    nf = pl.cdiv(n_tiles, n_sub)
    resident = pl.Buffered(1)

    def tile_of(s):
        return lambda f: jnp.minimum(f * n_sub + s, n_tiles - 1)

    def per_sub(shape, index):
        return [pl.BlockSpec(shape, functools.partial(index, tile_of(s))) for s in range(n_sub)]

    col_a = lambda tile, i, f: (0, tile(f))
    col_b = lambda tile, i, f: (0, n_tiles + tile(f))
    row_d = lambda tile, i, f: (tile(f), 0)
    in_specs = ([pl.BlockSpec((tm, d), lambda i, f: (i, 0), pipeline_mode=resident),
                 pl.BlockSpec((1, d), lambda i, f: (0, 0))]
                + per_sub((d, sub), col_a) + per_sub((d, sub), col_b)
                + per_sub((CONV_WIDTH, sub), col_a) + per_sub((CONV_WIDTH, sub), col_b)
                + per_sub((1, sub), col_a) + per_sub((1, sub), col_b)
                + per_sub((sub, d), row_d)
                + [pl.BlockSpec((1, d), lambda i, f: (0, 0))])
    operands = ([h1, g3] + [w_up] * (2 * n_sub) + [conv_w] * (2 * n_sub)
                + [conv_b] * (2 * n_sub) + [w_down] * n_sub + [g4])
    tf = sub * n_sub
    return pl.pallas_call(
        functools.partial(_ffn_kernel, n_sub=n_sub, n_tiles=n_tiles,
                          tiles_per_seq=seq_len // tm, keep=keep, up_row_blocks=up_row_blocks),
        out_shape=jax.ShapeDtypeStruct((t, d), F32),
        grid=(t // tm, nf),
        in_specs=in_specs,
        out_specs=pl.BlockSpec((tm, d), lambda i, f: (i, 0), pipeline_mode=resident),
        scratch_shapes=[pltpu.VMEM((tm, d), BF16),
                        pltpu.VMEM((2, nf * n_sub, CONV_WIDTH - 1, SUBLANES, sub), F32),
                        pltpu.VMEM((d // LANES, TIME_GROUP, LANES), F32)],
        compiler_params=_params(
            ("arbitrary", "arbitrary"),
            _vmem_limit(d * tf * 4 + tf * d * 2,
                        tm * d * 8 + tm * d * 2 + 2 * nf * tf * keep * 4 + TIME_GROUP * d * 4,
                        tm * tf * 32)),
        name="ffn",
    )(*operands)


def kernel(x, norm_pre_mix, w_in, ssm_lambda_re, ssm_lambda_im, ssm_log_step, ssm_b_re, ssm_b_im,
           ssm_c_re, ssm_c_im, ssm_d, ssm_glu_w, ssm_glu_b, pool_w, pool_b, pool_scale,
           w_branch_ssm, w_branch_pool, w_out, norm_post_mix, norm_pre_ffn, w_up, ffn_conv_w,
           ffn_conv_b, w_down, norm_post_ffn):
    bsz, seq_len, d = x.shape
    depth = w_in.shape[0]
    d_ssm = ssm_d.shape[1]
    d_pool = pool_scale.shape[1]
    n_groups = d_ssm // SSM_GROUP
    assert seq_len // CHUNK == 1 << N_SCAN_PASSES
    t = bsz * seq_len
    h = x.reshape(t, d)
    row = lambda v: v.reshape(1, -1)
    for i in range(depth):
        a = _norm(h, row(norm_pre_mix[i]))
        prest, w_up_b = _inproj(a, w_in[i], d_ssm, d_pool, w_up[i], 32)
        ut = _inproj_ssm(a, w_in[i], d_ssm, seq_len)
        m, rt, et, wsc = _ssm_prep(ssm_lambda_re[i], ssm_lambda_im[i], ssm_log_step[i],
                                   ssm_b_re[i], ssm_b_im[i], ssm_c_re[i], ssm_c_im[i])
        yt, w_out_b, w_bs_b, w_bp_b = _ssm(
            ut, m, rt, et, wsc, ssm_d[i].reshape(n_groups, SSM_GROUP, 1), seq_len // CHUNK,
            [w_out[i], w_branch_ssm[i], w_branch_pool[i]])
        ys = _glu(yt, ssm_glu_w[i].astype(BF16), row(ssm_glu_b[i]), seq_len)
        po = _pool(prest, pool_w[i].astype(BF16), pool_b[i][:, None, :], row(pool_scale[i]),
                   seq_len, d_pool)
        h, w_down_b = _mix(ys, po, prest, h, w_bs_b, w_bp_b, w_out_b,
                           row(norm_post_mix[i]), d_pool, w_down[i], 128)
        h = _ffn(h, row(norm_pre_ffn[i]), w_up_b, ffn_conv_w[i], row(ffn_conv_b[i]),
                 w_down_b, row(norm_post_ffn[i]), seq_len)
    return h.reshape(bsz, seq_len, d)
```

```python
import functools
import math

import jax
import jax.numpy as jnp
from jax import lax
from jax.experimental import pallas as pl
from jax.experimental.pallas import tpu as pltpu

F32 = jnp.float32
BF16 = jnp.bfloat16

EPS = 1e-6
MIN_NEG_REAL = -1e-4
SSM_GROUP = 16
SSM_STATE = 64
POOL_WINDOWS = (2, 4, 8, 16)
CONV_WIDTH = 3
CHUNK = 16
N_SCAN_PASSES = 7
LANES = 128
SUBLANES = 8
TIME_GROUP = 128

V7X_VMEM_BYTES = 64 * 1024 * 1024
VMEM_RESERVE = 6 * 1024 * 1024
VMEM_CEILING = V7X_VMEM_BYTES - VMEM_RESERVE
VMEM_SLACK = 4 * 1024 * 1024
SMALL_KERNEL_TEMP = 8 * 1024 * 1024


def _vmem_limit(pipelined_bytes, scratch_bytes, temp_bytes):
    return min(2 * pipelined_bytes + scratch_bytes + temp_bytes + VMEM_SLACK, VMEM_CEILING)


def _params(semantics, vmem):
    return pltpu.CompilerParams(dimension_semantics=semantics, vmem_limit_bytes=vmem)


def _rms(xf, g):
    ms = jnp.mean(xf * xf, axis=-1, keepdims=True)
    return xf * lax.rsqrt(ms + EPS) * g


NORM_ROWS = 32


def _for_row_blocks(n_rows, body):
    def step(r, carry):
        body(pl.ds(pl.multiple_of(r * NORM_ROWS, NORM_ROWS), NORM_ROWS))
        return carry
    lax.fori_loop(0, n_rows // NORM_ROWS, step, 0, unroll=2)


def _sigmoid(x):
    return 1.0 / (1.0 + jnp.exp(-x))


def _gelu_tanh(x):
    c = math.sqrt(2.0 / math.pi)
    return 0.5 * x * (1.0 + jnp.tanh(x * (c + (c * 0.044715) * (x * x))))


def _norm_kernel(x_ref, g_ref, o_ref):
    def rows(sl):
        o_ref[sl, :] = _rms(x_ref[sl, :], g_ref[...]).astype(BF16)
    _for_row_blocks(x_ref.shape[0], rows)


def _norm(x2, g, tm=512):
    t, d = x2.shape
    return pl.pallas_call(
        _norm_kernel,
        out_shape=jax.ShapeDtypeStruct((t, d), BF16),
        grid=(t // tm,),
        in_specs=[pl.BlockSpec((tm, d), lambda i: (i, 0)),
                  pl.BlockSpec((1, d), lambda i: (0, 0))],
        out_specs=pl.BlockSpec((tm, d), lambda i: (i, 0)),
        compiler_params=_params(("arbitrary",), _vmem_limit(tm * d * 6, 0, tm * d * 8)),
        name="norm",
    )(x2, g)


def _side_cast_specs(src, rows, grid):
    n_blocks = src.shape[0] // rows
    assert src.shape[0] % rows == 0 and n_blocks <= math.prod(grid)

    def index(*g):
        step = g[0]
        for k in range(1, len(grid)):
            step = step * grid[k] + g[k]
        return (jnp.minimum(step, n_blocks - 1), 0)

    spec = pl.BlockSpec((rows, src.shape[1]), index)
    return spec, spec, jax.ShapeDtypeStruct(src.shape, BF16)


def _inproj_kernel(a_ref, w_ref, cast_ref, o_ref, cast_o_ref, *, n_plain, row_blocks):
    gate = pl.program_id(1) >= n_plain
    w = w_ref[...].astype(BF16)
    rows = a_ref.shape[0] // row_blocks
    for r0 in range(0, a_ref.shape[0], rows):
        p = jnp.dot(a_ref[r0:r0 + rows, :], w, preferred_element_type=F32)
        o_ref[r0:r0 + rows, :] = jnp.where(gate, _sigmoid(p), p).astype(BF16)
    cast_o_ref[...] = cast_ref[...].astype(BF16)


def _inproj(a, w, col0, n_plain_cols, cast_src, cast_rows, tm=1024, tn=512, row_blocks=4):
    t, d = a.shape
    n = w.shape[1] - col0
    c0 = col0 // tn
    grid = (t // tm, n // tn)
    cast_in, cast_out, cast_shape = _side_cast_specs(cast_src, cast_rows, grid)
    cast_bytes = cast_rows * cast_src.shape[1] * 6
    return pl.pallas_call(
        functools.partial(_inproj_kernel, n_plain=n_plain_cols // tn, row_blocks=row_blocks),
        out_shape=(jax.ShapeDtypeStruct((t, n), BF16), cast_shape),
        grid=grid,
        in_specs=[pl.BlockSpec((tm, d), lambda i, j: (i, 0)),
                  pl.BlockSpec((d, tn), lambda i, j: (0, c0 + j)),
                  cast_in],
        out_specs=(pl.BlockSpec((tm, tn), lambda i, j: (i, j)), cast_out),
        compiler_params=_params(("arbitrary", "arbitrary"),
                                _vmem_limit(tm * d * 2 + d * tn * 4 + tm * tn * 2 + cast_bytes, 0,
                                            d * tn * 2 + tm * tn * 8)),
        name="inproj",
    )(a, w, cast_src)


def _inproj_ssm_kernel(a_ref, w_ref, o_ref, p_sc, *, row_blocks):
    chunks = o_ref.shape[-1]
    n_slabs = p_sc.shape[0]
    w = w_ref[...].astype(BF16)
    rows = a_ref.shape[0] // row_blocks
    for r0 in range(0, a_ref.shape[0], rows):
        p = jnp.dot(a_ref[r0:r0 + rows, :], w, preferred_element_type=F32)
        for s in range(n_slabs):
            p_sc[s, r0:r0 + rows] = p[:, s * LANES:(s + 1) * LANES]
    gps = LANES // SSM_GROUP
    for q in range(CHUNK):
        for s in range(n_slabs):
            blk = p_sc[s, pl.ds(q, chunks, stride=CHUNK), :]
            o_ref[s * gps:(s + 1) * gps, q] = (
                blk.T.reshape(gps, SSM_GROUP, chunks).astype(BF16))


def _inproj_ssm(a, w, n, seq_len, tn=256, row_blocks=8):
    t, d = a.shape
    g = n // SSM_GROUP
    chunks = seq_len // CHUNK
    return pl.pallas_call(
        functools.partial(_inproj_ssm_kernel, row_blocks=row_blocks),
        out_shape=jax.ShapeDtypeStruct((g, CHUNK, SSM_GROUP, t // CHUNK), BF16),
        grid=(t // seq_len, n // tn),
        in_specs=[pl.BlockSpec((seq_len, d), lambda bi, j: (bi, 0)),
                  pl.BlockSpec((d, tn), lambda bi, j: (0, j))],
        out_specs=pl.BlockSpec((tn // SSM_GROUP, CHUNK, SSM_GROUP, chunks),
                               lambda bi, j: (j, 0, 0, bi)),
        scratch_shapes=[pltpu.VMEM((tn // LANES, seq_len, LANES), F32)],
        compiler_params=_params(("arbitrary", "arbitrary"),
                                _vmem_limit(seq_len * d * 2 + d * tn * 4 + tn * seq_len * 2,
                                            seq_len * tn * 4, d * tn * 2 + seq_len * tn * 16)),
        name="inproj_ssm",
    )(a, w)


def _ssm_prep_kernel(lre_ref, lim_ref, ls_ref, btr_ref, bti_ref, cr_ref, ci_ref,
                     m_ref, rt_ref, et_ref, w_ref, *, groups):
    p2 = 2 * SSM_STATE
    cq = CHUNK * SSM_GROUP
    lane = lax.broadcasted_iota(jnp.int32, (1, p2), 1)
    is_re = lane < SSM_STATE
    sgn = jnp.where(is_re, 1.0, -1.0).astype(F32)
    lane_blk = lax.broadcasted_iota(jnp.int32, (cq, cq), 1) // SSM_GROUP
    row8 = lax.broadcasted_iota(jnp.int32, (8, p2), 0)
    is_re8 = lax.broadcasted_iota(jnp.int32, (8, p2), 1) < SSM_STATE

    for gi in range(groups):
        lr = jnp.minimum(lre_ref[gi], MIN_NEG_REAL)
        li = lim_ref[gi]
        dt = jnp.exp(ls_ref[gi])
        mag = jnp.exp(lr * dt)
        ang = li * dt
        ar = mag * jnp.cos(ang)
        ai = mag * jnp.sin(ang)
        nr = ar - 1.0
        ni = ai
        den = lr * lr + li * li
        f_re = (nr * lr + ni * li) / den
        f_im = (ni * lr - nr * li) / den
        btr = btr_ref[gi]
        bti = bti_ref[gi]
        bbr = f_re * btr - f_im * bti
        bbi = f_re * bti + f_im * btr
        cr = cr_ref[gi]
        ci = ci_ref[gi]

        pa = [jnp.where(is_re, 1.0, 0.0).astype(F32)]
        pb = [jnp.where(is_re, 0.0, 1.0).astype(F32)]
        for _ in range(CHUNK):
            a_prev, b_prev = pa[-1], pb[-1]
            pa.append(a_prev * ar + b_prev * ai)
            pb.append(b_prev * ar - a_prev * ai)

        gk = [sgn * (cr * pa[k] + ci * pb[k]) for k in range(CHUNK + 1)]
        fs = jnp.concatenate(gk[:CHUNK], axis=0)
        et = jnp.concatenate(gk[1:], axis=0)
        rtt = jnp.concatenate(
            [bbr * pa[CHUNK - 1 - q] + bbi * pb[CHUNK - 1 - q] for q in range(CHUNK)], axis=0)
        bst = jnp.where(is_re, bbr, bbi)
        bst_t = jnp.concatenate([bst] * CHUNK, axis=0)
        kw = lax.dot_general(fs, bst_t, (((1,), (1,)), ((), ())),
                             precision=lax.Precision.HIGHEST, preferred_element_type=F32)
        m = jnp.zeros((cq, cq), F32)
        for q in range(CHUNK):
            if q == 0:
                shifted = kw
            else:
                shifted = jnp.concatenate(
                    [jnp.zeros((q * SSM_GROUP, cq), F32), kw[:cq - q * SSM_GROUP]], axis=0)
            m = jnp.where(lane_blk == q, shifted, m)
        m_ref[gi] = m.astype(BF16)
        rt_ref[gi] = rtt.T.astype(BF16)
        et_ref[gi] = et.astype(BF16)

        wa = jnp.broadcast_to(pa[CHUNK], (8, p2))
        rows = jnp.zeros((8, p2), F32)
        for k in range(N_SCAN_PASSES):
            rows = jnp.where(row8 == k, wa, rows)
            swapped = pltpu.roll(wa, SSM_STATE, 1)
            re2 = jnp.where(is_re8, wa, swapped)
            im2 = jnp.where(is_re8, swapped, wa)
            wb = jnp.where(is_re8, -swapped, swapped)
            wa = wa * re2 + wb * im2
        full = jnp.concatenate([rows, jnp.zeros((p2 - 8, p2), F32)], axis=0)
        w_ref[gi] = full.T


def _ssm_prep(lam_re, lam_im, log_step, b_re, b_im, c_re, c_im, groups=8):
    g, p = lam_re.shape
    p2 = 2 * p
    cq = CHUNK * SSM_GROUP
    dup = lambda v: jnp.concatenate([v, v], axis=-1)
    lre = dup(lam_re)[:, None, :]
    lim = dup(lam_im)[:, None, :]
    ls = jnp.broadcast_to(log_step[:, None, None], (g, 1, p2))
    btr = dup(jnp.swapaxes(b_re, 1, 2))
    bti = dup(jnp.swapaxes(b_im, 1, 2))
    cr = dup(c_re)
    ci = dup(c_im)
    row = pl.BlockSpec((groups, 1, p2), lambda i: (i, 0, 0))
    mat = pl.BlockSpec((groups, SSM_GROUP, p2), lambda i: (i, 0, 0))
    return pl.pallas_call(
        functools.partial(_ssm_prep_kernel, groups=groups),
        out_shape=(jax.ShapeDtypeStruct((g, cq, cq), BF16),
                   jax.ShapeDtypeStruct((g, p2, cq), BF16),
                   jax.ShapeDtypeStruct((g, cq, p2), BF16),
                   jax.ShapeDtypeStruct((g, p2, p2), F32)),
        grid=(g // groups,),
        in_specs=[row, row, row, mat, mat, mat, mat],
        out_specs=(pl.BlockSpec((groups, cq, cq), lambda i: (i, 0, 0)),
                   pl.BlockSpec((groups, p2, cq), lambda i: (i, 0, 0)),
                   pl.BlockSpec((groups, cq, p2), lambda i: (i, 0, 0)),
                   pl.BlockSpec((groups, p2, p2), lambda i: (i, 0, 0))),
        compiler_params=_params(("arbitrary",),
                                _vmem_limit(groups * cq * cq * 4, 0, SMALL_KERNEL_TEMP)),
        name="ssm_prep",
    )(lre, lim, ls, btr, bti, cr, ci)


def _ssm_kernel(*refs, groups, chunks_per_seq, n_cast):
    u_ref, m_ref, rt_ref, et_ref, w_ref, d_ref = refs[:6]
    o_ref = refs[6 + n_cast]
    for src, dst in zip(refs[6:6 + n_cast], refs[7 + n_cast:]):
        dst[...] = src[...].astype(BF16)
    cq = CHUNK * SSM_GROUP
    nc = u_ref.shape[-1]
    pos = lax.broadcasted_iota(jnp.int32, (SSM_STATE, nc), 1) % chunks_per_seq
    for gi in range(groups):
        x3 = u_ref[gi]
        x = x3.reshape(cq, nc)
        y = jnp.dot(m_ref[gi], x, preferred_element_type=F32)
        r = jnp.dot(rt_ref[gi], x, preferred_element_type=F32)
        s_re, s_im = r[:SSM_STATE], r[SSM_STATE:]
        for k in range(N_SCAN_PASSES):
            dist = 1 << k
            wr = w_ref[gi, 0:SSM_STATE, k:k + 1]
            wi = w_ref[gi, SSM_STATE:2 * SSM_STATE, k:k + 1]
            keep = pos >= dist
            sh_re = jnp.where(keep, pltpu.roll(s_re, dist, 1), 0.0)
            sh_im = jnp.where(keep, pltpu.roll(s_im, dist, 1), 0.0)
            s_re, s_im = (s_re + wr * sh_re - wi * sh_im,
                          s_im + wr * sh_im + wi * sh_re)
        first = pos >= 1
        sp = jnp.concatenate([jnp.where(first, pltpu.roll(s_re, 1, 1), 0.0),
                              jnp.where(first, pltpu.roll(s_im, 1, 1), 0.0)], axis=0)
        y = y + jnp.dot(et_ref[gi], sp.astype(BF16), preferred_element_type=F32)
        y3 = y.reshape(CHUNK, SSM_GROUP, nc) + d_ref[gi][None] * x3.astype(F32)
        o_ref[gi] = _gelu_tanh(y3).astype(BF16)


def _ssm(ut, m, rt, et, w, d, chunks_per_seq, cast_srcs, groups=4):
    g, q, j, nc = ut.shape
    cq = q * j
    p2 = rt.shape[1]
    grid = (g // groups,)
    blk = lambda *s: pl.BlockSpec((groups,) + s, lambda i: (i,) + (0,) * len(s))
    casts = [_side_cast_specs(src, src.shape[0] // grid[0], grid) for src in cast_srcs]
    cast_bytes = sum(src.size // grid[0] * 6 for src in cast_srcs)
    return pl.pallas_call(
        functools.partial(_ssm_kernel, groups=groups, chunks_per_seq=chunks_per_seq,
                          n_cast=len(cast_srcs)),
        out_shape=(jax.ShapeDtypeStruct(ut.shape, BF16), *[c[2] for c in casts]),
        grid=grid,
        in_specs=[blk(q, j, nc), blk(cq, cq), blk(p2, cq), blk(cq, p2), blk(p2, p2), blk(j, 1),
                  *[c[0] for c in casts]],
        out_specs=(blk(q, j, nc), *[c[1] for c in casts]),
        compiler_params=_params(("arbitrary",),
                                _vmem_limit(groups * cq * nc * 8 + cast_bytes, 0,
                                            SMALL_KERNEL_TEMP)),
        name="ssm",
    )(ut, m, rt, et, w, d, *cast_srcs)


def _glu_kernel(y_ref, w_ref, b_ref, o_ref, y_sc, *, tn):
    n = pl.program_id(1)
    chunks = y_ref.shape[-1]
    n_slabs = y_sc.shape[0]

    @pl.when(n == 0)
    def _():
        for q in range(CHUNK):
            yq = y_ref[:, q]
            yq = yq.reshape(yq.shape[0] * yq.shape[1], chunks).astype(F32).T
            for s in range(n_slabs):
                y_sc[s, pl.ds(q, chunks, stride=CHUNK), :] = yq[:, s * LANES:(s + 1) * LANES]

    y = jnp.concatenate([y_sc[s] for s in range(n_slabs)], axis=1).astype(BF16)
    z = jnp.dot(y, w_ref[...], preferred_element_type=F32) + b_ref[...]
    per_tile = tn // LANES
    yn = jnp.concatenate([y_sc[n * per_tile + s] for s in range(per_tile)], axis=1)
    o_ref[...] = (yn * _sigmoid(z)).astype(BF16)


def _glu(yt, w, b, seq_len, tn=512):
    g, q, j, nc = yt.shape
    d = g * j
    chunks = seq_len // q
    t = nc * q
    return pl.pallas_call(
        functools.partial(_glu_kernel, tn=tn),
        out_shape=jax.ShapeDtypeStruct((t, d), BF16),
        grid=(nc // chunks, d // tn),
        in_specs=[pl.BlockSpec((g, q, j, chunks), lambda bi, n: (0, 0, 0, bi)),
                  pl.BlockSpec((d, tn), lambda bi, n: (0, n)),
                  pl.BlockSpec((1, tn), lambda bi, n: (0, n))],
        out_specs=pl.BlockSpec((seq_len, tn), lambda bi, n: (bi, n)),
        scratch_shapes=[pltpu.VMEM((d // LANES, seq_len, LANES), F32)],
        compiler_params=_params(("arbitrary", "arbitrary"),
                                _vmem_limit((seq_len * d + d * tn + seq_len * tn) * 2,
                                            seq_len * d * 4, seq_len * d * 2 + seq_len * tn * 12)),
        name="glu",
    )(yt, w, b)


def _pool_kernel(u_ref, w_ref, b_ref, s_ref, o_ref, ext_sc, *, tiles_per_seq, halo):
    i = pl.program_id(0)
    tm = u_ref.shape[0]
    gp = w_ref.shape[1]

    @pl.when(i % tiles_per_seq == 0)
    def _():
        ext_sc[0:halo] = jnp.zeros((halo, ext_sc.shape[1]), BF16)

    @pl.when(i % tiles_per_seq != 0)
    def _():
        ext_sc[0:halo] = ext_sc[tm:tm + halo]

    ext_sc[halo:halo + tm] = u_ref[...]

    t_loc = lax.broadcasted_iota(jnp.int32, (tm, tm + halo), 0)
    s_loc = lax.broadcasted_iota(jnp.int32, (tm, tm + halo), 1)
    lag = t_loc + halo - s_loc
    t_seq = (i % tiles_per_seq) * tm + lax.broadcasted_iota(jnp.int32, (tm, 1), 0)
    for gi, win in enumerate(POOL_WINDOWS):
        band = jnp.where((lag >= 0) & (lag < win), 1.0, 0.0).astype(BF16)
        cols = slice(gi * gp, (gi + 1) * gp)
        wsum = jnp.dot(band, ext_sc[:, cols], preferred_element_type=F32)
        cnt = jnp.minimum(t_seq + 1, win).astype(F32)
        z = wsum / cnt - u_ref[:, cols].astype(F32)
        z = jnp.dot(z.astype(BF16), w_ref[gi], preferred_element_type=F32) + b_ref[gi]
        o_ref[:, cols] = (z * s_ref[:, cols]).astype(BF16)


def _pool(prest, w, b, scale, seq_len, d_pool, tm=512, halo=128):
    t = prest.shape[0]
    ng, gp, _ = w.shape
    return pl.pallas_call(
        functools.partial(_pool_kernel, tiles_per_seq=seq_len // tm, halo=halo),
        out_shape=jax.ShapeDtypeStruct((t, d_pool), BF16),
        grid=(t // tm,),
        in_specs=[pl.BlockSpec((tm, d_pool), lambda i: (i, 0)),
                  pl.BlockSpec((ng, gp, gp), lambda i: (0, 0, 0)),
                  pl.BlockSpec((ng, 1, gp), lambda i: (0, 0, 0)),
                  pl.BlockSpec((1, d_pool), lambda i: (0, 0))],
        out_specs=pl.BlockSpec((tm, d_pool), lambda i: (i, 0)),
        scratch_shapes=[pltpu.VMEM((tm + halo, d_pool), BF16)],
        compiler_params=_params(("arbitrary",),
                                _vmem_limit(tm * d_pool * 4 + ng * gp * gp * 2,
                                            (tm + halo) * d_pool * 2, SMALL_KERNEL_TEMP)),
        name="pool",
    )(prest, w, b, scale)


def _mix_kernel(ys_ref, po_ref, gs_ref, gp_ref, x_ref, wbs_ref, wbp_ref, wo_ref, g_ref, cast_ref,
                o_ref, cast_o_ref, x_sc, *, sub):
    n = pl.program_id(1)
    cast_o_ref[...] = cast_ref[...].astype(BF16)
    tn = x_ref.shape[1]
    x_sc[:, pl.ds(pl.multiple_of(n * tn, tn), tn)] = x_ref[...]

    @pl.when(n == 0)
    def _():
        o_ref[...] = jnp.zeros_like(o_ref)

    subs = [slice(s * sub, (s + 1) * sub) for s in range(wo_ref.shape[0] // sub)]
    branch = [(jnp.dot(ys_ref[...], wbs_ref[:, cols], preferred_element_type=F32),
               jnp.dot(po_ref[...], wbp_ref[:, cols], preferred_element_type=F32))
              for cols in subs]
    merged = [(gs_ref[:, cols].astype(F32) * ms + gp_ref[:, cols].astype(F32) * mp).astype(BF16)
              for cols, (ms, mp) in zip(subs, branch)]
    o_ref[...] += jnp.dot(jnp.concatenate(merged, axis=1), wo_ref[...],
                          preferred_element_type=F32)

    @pl.when(n == pl.num_programs(1) - 1)
    def _():
        def rows(sl):
            o_ref[sl, :] = x_sc[sl, :] + _rms(o_ref[sl, :], g_ref[...])
        _for_row_blocks(o_ref.shape[0], rows)


def _mix(ys, po, prest, x2, wbs, wbp, wo, g, gate_col0, cast_src, cast_rows,
         tm=512, tn=512, sub=256):
    t, d = x2.shape
    ds_ = ys.shape[1]
    dp = po.shape[1]
    c0 = gate_col0 // tn
    c1 = (gate_col0 + d) // tn
    resident = pl.Buffered(1)
    grid = (t // tm, d // tn)
    cast_in, cast_out, cast_shape = _side_cast_specs(cast_src, cast_rows, grid)
    cast_bytes = cast_rows * cast_src.shape[1] * 6
    return pl.pallas_call(
        functools.partial(_mix_kernel, sub=sub),
        out_shape=(jax.ShapeDtypeStruct((t, d), F32), cast_shape),
        grid=grid,
        in_specs=[pl.BlockSpec((tm, ds_), lambda i, n: (i, 0)),
                  pl.BlockSpec((tm, dp), lambda i, n: (i, 0)),
                  pl.BlockSpec((tm, tn), lambda i, n: (i, c0 + n)),
                  pl.BlockSpec((tm, tn), lambda i, n: (i, c1 + n)),
                  pl.BlockSpec((tm, tn), lambda i, n: (i, n)),
                  pl.BlockSpec((ds_, tn), lambda i, n: (0, n)),
                  pl.BlockSpec((dp, tn), lambda i, n: (0, n)),
                  pl.BlockSpec((tn, d), lambda i, n: (n, 0)),
                  pl.BlockSpec((1, d), lambda i, n: (0, 0)),
                  cast_in],
        out_specs=(pl.BlockSpec((tm, d), lambda i, n: (i, 0), pipeline_mode=resident), cast_out),
        scratch_shapes=[pltpu.VMEM((tm, d), F32)],
        compiler_params=_params(
            ("arbitrary", "arbitrary"),
            _vmem_limit(tm * (ds_ + dp) * 2 + tm * tn * 8 + (ds_ + dp) * tn * 2 + tn * d * 2
                        + cast_bytes, tm * d * 8, tm * tn * 16)),
        name="mix",
    )(ys, po, prest, prest, x2, wbs, wbp, wo, g, cast_src)


def _ffn_kernel(*refs, n_sub, n_tiles, tiles_per_seq, keep, up_row_blocks):
    h_ref, g3_ref = refs[:2]
    groups = [refs[2 + k * n_sub:2 + (k + 1) * n_sub] for k in range(7)]
    wa_refs, wb_refs, cwa_refs, cwb_refs, cba_refs, cbb_refs, wd_refs = groups
    g4_ref, o_ref, c_sc, carry_sc, slab_sc = refs[2 + 7 * n_sub:]
    i = pl.program_id(0)
    f = pl.program_id(1)
    tm = h_ref.shape[0]
    d = h_ref.shape[1]
    sub = wd_refs[0].shape[0]
    n_slabs = d // LANES
    vregs = TIME_GROUP // SUBLANES

    def to_interleaved(x):
        r = lax.broadcasted_iota(jnp.int32, (TIME_GROUP, TIME_GROUP), 0)
        t = lax.broadcasted_iota(jnp.int32, (TIME_GROUP, TIME_GROUP), 1)
        pick = jnp.where(t == (r % SUBLANES) * vregs + r // SUBLANES, 1.0, 0.0).astype(BF16)
        return jnp.dot(pick, x, preferred_element_type=F32).astype(BF16)

    def to_token_order(y):
        for s in range(n_slabs):
            for v in range(vregs):
                slab_sc[s, pl.ds(v, SUBLANES, stride=vregs), :] = (
                    y[v * SUBLANES:(v + 1) * SUBLANES, s * LANES:(s + 1) * LANES])
        return jnp.concatenate([slab_sc[s] for s in range(n_slabs)], axis=1)

    def for_time_groups(body):
        def step(g, carry):
            body(pl.ds(pl.multiple_of(g * TIME_GROUP, TIME_GROUP), TIME_GROUP))
            return carry
        lax.fori_loop(0, tm // TIME_GROUP, step, 0)

    @pl.when(f == 0)
    def _():
        def group(sl):
            c_sc[sl, :] = to_interleaved(_rms(h_ref[sl, :], g3_ref[...]).astype(BF16))
            o_ref[sl, :] = jnp.zeros((TIME_GROUP, d), F32)
        for_time_groups(group)

    seq_start = i % tiles_per_seq == 0

    rb = tm // up_row_blocks
    row_in_head = lax.broadcasted_iota(jnp.int32, (SUBLANES, sub), 0)

    def up_blocks(w_ref):
        w = w_ref[...]
        return [jnp.dot(c_sc[r0:r0 + rb, :], w, preferred_element_type=F32)
                for r0 in range(0, tm, rb)]

    def shift_time(z, before):
        pieces = []
        for g0 in range(0, tm, TIME_GROUP):
            last_vreg = pltpu.roll(z[g0 + TIME_GROUP - SUBLANES:g0 + TIME_GROUP], 1, 0)
            first = before if g0 == 0 else z[g0 - 1:g0]
            pieces += [jnp.where(row_in_head == 0, first, last_vreg),
                       z[g0:g0 + TIME_GROUP - SUBLANES]]
        return jnp.concatenate(pieces, axis=0)

    def conv(blocks, half, tile, cw_ref, cb_ref):
        up = jnp.concatenate(blocks, axis=0)
        tails = [jnp.where(seq_start, 0.0, carry_sc[half, tile, j]) for j in range(CONV_WIDTH - 1)]
        for j in range(CONV_WIDTH - 1):
            carry_sc[half, tile, j] = up[tm - (j + 1) * SUBLANES:tm - j * SUBLANES]
        cw = cw_ref[...]
        z = cw[0:1] * up
        for k in range(1, CONV_WIDTH):
            before = sum(cw[k - 1 - j:k - j] * tails[j][SUBLANES - 1:SUBLANES] for j in range(k))
            z = cw[k:k + 1] * up + shift_time(z, before)
        z = z + cb_ref[...]
        return [z[r0:r0 + rb] for r0 in range(0, tm, rb)]

    def step(n_act):
        ups_a = [up_blocks(wa_refs[s]) for s in range(n_act)]
        ups_b = [up_blocks(wb_refs[s]) for s in range(n_act)]
        ups = list(zip(ups_a, ups_b))
        fms = []
        for s in range(n_act):
            tile = f * n_sub + s
            ua = conv(ups[s][0], 0, tile, cwa_refs[s], cba_refs[s])
            ub = conv(ups[s][1], 1, tile, cwb_refs[s], cbb_refs[s])
            fms.append(jnp.concatenate(
                [(_gelu_tanh(a) * b).astype(BF16) for a, b in zip(ua, ub)], axis=0))
        fm = jnp.concatenate(fms, axis=1)
        wd = jnp.concatenate([wd_refs[s][...] for s in range(n_act)], axis=0)
        o_ref[...] += jnp.dot(fm, wd, preferred_element_type=F32)

    last = pl.num_programs(1) - 1
    tail = n_tiles - (n_tiles - 1) // n_sub * n_sub
    if tail == n_sub:
        step(n_sub)
    else:
        pl.when(f < last)(lambda: step(n_sub))
        pl.when(f == last)(lambda: step(tail))

    @pl.when(f == pl.num_programs(1) - 1)
    def _():
        def group(sl):
            o_ref[sl, :] = h_ref[sl, :] + to_token_order(_rms(o_ref[sl, :], g4_ref[...]))
        for_time_groups(group)


def _ffn(h1, g3, w_up, conv_w, conv_b, w_down, g4, seq_len, tm=512, sub=256, n_sub=2, keep=8,
         up_row_blocks=2):
    t, d = h1.shape
    d_ff = w_down.shape[0]
    n_tiles = d_ff // sub
    nf = pl.cdiv(n_tiles, n_sub)
    resident = pl.Buffered(1)

    def tile_of(s):
        return lambda f: jnp.minimum(f * n_sub + s, n_tiles - 1)

    def per_sub(shape, index):
        return [pl.BlockSpec(shape, functools.partial(index, tile_of(s))) for s in range(n_sub)]

    col_a = lambda tile, i, f: (0, tile(f))
    col_b = lambda tile, i, f: (0, n_tiles + tile(f))
    row_d = lambda tile, i, f: (tile(f), 0)
    in_specs = ([pl.BlockSpec((tm, d), lambda i, f: (i, 0), pipeline_mode=resident),
                 pl.BlockSpec((1, d), lambda i, f: (0, 0))]
                + per_sub((d, sub), col_a) + per_sub((d, sub), col_b)
                + per_sub((CONV_WIDTH, sub), col_a) + per_sub((CONV_WIDTH, sub), col_b)
                + per_sub((1, sub), col_a) + per_sub((1, sub), col_b)
                + per_sub((sub, d), row_d)
                + [pl.BlockSpec((1, d), lambda i, f: (0, 0))])
    operands = ([h1, g3] + [w_up] * (2 * n_sub) + [conv_w] * (2 * n_sub)
                + [conv_b] * (2 * n_sub) + [w_down] * n_sub + [g4])
    tf = sub * n_sub
    return pl.pallas_call(
        functools.partial(_ffn_kernel, n_sub=n_sub, n_tiles=n_tiles,
                          tiles_per_seq=seq_len // tm, keep=keep, up_row_blocks=up_row_blocks),
        out_shape=jax.ShapeDtypeStruct((t, d), F32),
        grid=(t // tm, nf),
        in_specs=in_specs,
        out_specs=pl.BlockSpec((tm, d), lambda i, f: (i, 0), pipeline_mode=resident),
        scratch_shapes=[pltpu.VMEM((tm, d), BF16),
                        pltpu.VMEM((2, nf * n_sub, CONV_WIDTH - 1, SUBLANES, sub), F32),
                        pltpu.VMEM((d // LANES, TIME_GROUP, LANES), F32)],
        compiler_params=_params(
            ("arbitrary", "arbitrary"),
            _vmem_limit(d * tf * 4 + tf * d * 2,
                        tm * d * 8 + tm * d * 2 + 2 * nf * tf * keep * 4 + TIME_GROUP * d * 4,
                        tm * tf * 32)),
        name="ffn",
    )(*operands)


def kernel(x, norm_pre_mix, w_in, ssm_lambda_re, ssm_lambda_im, ssm_log_step, ssm_b_re, ssm_b_im,
           ssm_c_re, ssm_c_im, ssm_d, ssm_glu_w, ssm_glu_b, pool_w, pool_b, pool_scale,
           w_branch_ssm, w_branch_pool, w_out, norm_post_mix, norm_pre_ffn, w_up, ffn_conv_w,
           ffn_conv_b, w_down, norm_post_ffn):
    bsz, seq_len, d = x.shape
    depth = w_in.shape[0]
    d_ssm = ssm_d.shape[1]
    d_pool = pool_scale.shape[1]
    n_groups = d_ssm // SSM_GROUP
    assert seq_len // CHUNK == 1 << N_SCAN_PASSES
    t = bsz * seq_len
    h = x.reshape(t, d)
    row = lambda v: v.reshape(1, -1)
    for i in range(depth):
        a = _norm(h, row(norm_pre_mix[i]))
        prest, w_up_b = _inproj(a, w_in[i], d_ssm, d_pool, w_up[i], 32)
        ut = _inproj_ssm(a, w_in[i], d_ssm, seq_len)
        m, rt, et, wsc = _ssm_prep(ssm_lambda_re[i], ssm_lambda_im[i], ssm_log_step[i],
                                   ssm_b_re[i], ssm_b_im[i], ssm_c_re[i], ssm_c_im[i])
        yt, w_out_b, w_bs_b, w_bp_b = _ssm(
            ut, m, rt, et, wsc, ssm_d[i].reshape(n_groups, SSM_GROUP, 1), seq_len // CHUNK,
            [w_out[i], w_branch_ssm[i], w_branch_pool[i]])
        ys = _glu(yt, ssm_glu_w[i].astype(BF16), row(ssm_glu_b[i]), seq_len)
        po = _pool(prest, pool_w[i].astype(BF16), pool_b[i][:, None, :], row(pool_scale[i]),
                   seq_len, d_pool)
        h, w_down_b = _mix(ys, po, prest, h, w_bs_b, w_bp_b, w_out_b,
                           row(norm_post_mix[i]), d_pool, w_down[i], 128)
        h = _ffn(h, row(norm_pre_ffn[i]), w_up_b, ffn_conv_w[i], row(ffn_conv_b[i]),
                 w_down_b, row(norm_post_ffn[i]), seq_len)
    return h.reshape(bsz, seq_len, d)
```
